```python
import math
import jax, jax.numpy as jnp
from jax import lax
import numpy as np

D_MODEL = 2048
BATCH = 4
SEQ = 2048
DEPTH = 4

N_MIXERS = 3
PLE_DIM = 256
D_FF = 4 * D_MODEL
NORM_EPS = 1e-6
ROPE_THETA = 10000.0
MAX_POS_OFFSET = 1024

GDN_HEAD_DIM = 128
GDN_QK_HEADS = D_MODEL // 128
GDN_V_HEADS = 2 * GDN_QK_HEADS
GDN_CONV = 4
GDN_CHUNK = 64
GDN_KEY_DIM = GDN_QK_HEADS * GDN_HEAD_DIM
GDN_VAL_DIM = GDN_V_HEADS * GDN_HEAD_DIM
GDN_CONV_DIM = 2 * GDN_KEY_DIM + GDN_VAL_DIM
GDN_IN_DIM = GDN_CONV_DIM + GDN_VAL_DIM + 2 * GDN_V_HEADS

RET_HEADS = 8
RET_QK_DIM = D_MODEL
RET_V_DIM = 2 * D_MODEL
RET_QK_HEAD = RET_QK_DIM // RET_HEADS
RET_V_HEAD = RET_V_DIM // RET_HEADS
RET_CHUNK = 128
RET_IN_DIM = 2 * RET_QK_DIM + 2 * RET_V_DIM

DSA_HEAD_DIM = 128
DSA_HEADS = D_MODEL // DSA_HEAD_DIM
DSA_KV_HEADS = 4
IDX_HEADS = 16
IDX_HEAD_DIM = 64
DSA_TOPK_MAX = 256
DSA_BLOCK = 128
DSA_Q_DIM = DSA_HEADS * DSA_HEAD_DIM
DSA_KV_DIM = DSA_KV_HEADS * DSA_HEAD_DIM
DSA_IN_DIM = DSA_Q_DIM + 2 * DSA_KV_DIM + IDX_HEADS * IDX_HEAD_DIM + IDX_HEAD_DIM + IDX_HEADS

kernel_name = 'hybrid_gdn_retention_dsa_trunk'


def rms_norm(x, gain=None):
    xf = x.astype(jnp.float32)
    y = xf * lax.rsqrt(jnp.mean(xf * xf, axis=-1, keepdims=True) + NORM_EPS)
    if gain is not None:
        y = y * gain.astype(jnp.float32)
    return y


def l2_normalize(x):
    xf = x.astype(jnp.float32)
    return xf * lax.rsqrt(jnp.sum(xf * xf, axis=-1, keepdims=True) + NORM_EPS)


def rope_angles(positions, dim):
    inv_freq = ROPE_THETA ** (-jnp.arange(0, dim, 2, dtype=jnp.float32) / dim)
    ang = positions.astype(jnp.float32)[..., None] * inv_freq
    return jnp.cos(ang), jnp.sin(ang)


def apply_rope(x, cos, sin):
    x1, x2 = jnp.split(x.astype(jnp.float32), 2, axis=-1)
    c, s = cos[:, :, None, :], sin[:, :, None, :]
    return jnp.concatenate([x1 * c - x2 * s, x2 * c + x1 * s], axis=-1).astype(x.dtype)


def causal_conv_silu(x, w):
    k_w, ch = w.shape
    y = lax.conv_general_dilated(x, w[:, None, :].astype(x.dtype), window_strides=(1,), padding=[(k_w - 1, 0)],
                                 dimension_numbers=('NWC', 'WIO', 'NWC'), feature_group_count=ch)
    return jax.nn.silu(y)


def gated_delta_rule_chunked(q, k, v, g, beta, chunk):
    bsz, h, l, dk = q.shape
    dv = v.shape[-1]
    n = l // chunk
    rs = lambda t: t.reshape(bsz, h, n, chunk, *t.shape[3:])
    q, k, v, g, beta = rs(q), rs(k), rs(v), rs(g), rs(beta)
    g = jnp.cumsum(g, axis=-1)
    tri_incl = jnp.tril(jnp.ones((chunk, chunk), bool))
    tri_strict = jnp.tril(jnp.ones((chunk, chunk), bool), -1)
    decay = jnp.exp(jnp.where(tri_incl, g[..., :, None] - g[..., None, :], -jnp.inf))
    k_beta = k * beta[..., None]
    v_beta = v * beta[..., None]
    a = jnp.where(tri_strict, jnp.einsum('bhncd,bhnsd->bhncs', k_beta, k) * decay, 0.0)
    rhs = jnp.concatenate([v_beta, k_beta * jnp.exp(g)[..., None]], axis=-1)
    sol = lax.linalg.triangular_solve(a + jnp.eye(chunk, dtype=jnp.float32), rhs,
                                      left_side=True, lower=True, unit_diagonal=True)
    w_val, k_cum = sol[..., :dv], sol[..., dv:]
    intra = jnp.einsum('bhncd,bhnsd->bhncs', q, k) * decay

    def step(state, inp):
        q_c, k_c, w_c, kc_c, g_c, at_c = inp
        v_new = w_c - jnp.einsum('bhcd,bhde->bhce', kc_c, state)
        o_c = (jnp.einsum('bhcd,bhde->bhce', q_c * jnp.exp(g_c)[..., None], state)
               + jnp.einsum('bhcs,bhse->bhce', at_c, v_new))
        g_last = g_c[..., -1]
        state = (state * jnp.exp(g_last)[..., None, None]
                 + jnp.einsum('bhcd,bhce->bhde', k_c * jnp.exp(g_last[..., None] - g_c)[..., None], v_new))
        return state, o_c

    xs = tuple(jnp.moveaxis(t, 2, 0) for t in (q, k, w_val, k_cum, g, intra))
    _, o = lax.scan(step, jnp.zeros((bsz, h, dk, dv), jnp.float32), xs)
    return jnp.moveaxis(o, 0, 2).reshape(bsz, h, l, dv)


def gdn_mixer(u, w_in, conv_w, a_log, dt_bias, norm_w, w_out):
    bsz, l, _ = u.shape
    qkv, z, beta_logit, a = jnp.split(
        u @ w_in, [GDN_CONV_DIM, GDN_CONV_DIM + GDN_VAL_DIM, GDN_CONV_DIM + GDN_VAL_DIM + GDN_V_HEADS], axis=-1)
    qkv = causal_conv_silu(qkv, conv_w)
    q, k, v = jnp.split(qkv, [GDN_KEY_DIM, 2 * GDN_KEY_DIM], axis=-1)
    rep = GDN_V_HEADS // GDN_QK_HEADS
    q = jnp.repeat(l2_normalize(q.reshape(bsz, l, GDN_QK_HEADS, GDN_HEAD_DIM)), rep, axis=2) * GDN_HEAD_DIM ** -0.5
    k = jnp.repeat(l2_normalize(k.reshape(bsz, l, GDN_QK_HEADS, GDN_HEAD_DIM)), rep, axis=2)
    v = v.reshape(bsz, l, GDN_V_HEADS, GDN_HEAD_DIM).astype(jnp.float32)
    beta = jax.nn.sigmoid(beta_logit.astype(jnp.float32))
    g = -jnp.exp(a_log.astype(jnp.float32)) * jax.nn.softplus(a.astype(jnp.float32) + dt_bias.astype(jnp.float32))
    bhl = lambda t: jnp.swapaxes(t, 1, 2)
    o = gated_delta_rule_chunked(bhl(q), bhl(k), bhl(v), bhl(g), bhl(beta), GDN_CHUNK)
    o = rms_norm(bhl(o), norm_w) * jax.nn.silu(z.astype(jnp.float32)).reshape(bsz, l, GDN_V_HEADS, GDN_HEAD_DIM)
    return o.reshape(bsz, l, GDN_VAL_DIM).astype(u.dtype) @ w_out


def retention_chunked(q, k, v, log_gamma, chunk):
    bsz, l, h, dk = q.shape
    dv = v.shape[-1]
    n = l // chunk
    q, k, v = (t.astype(jnp.float32).reshape(bsz, n, chunk, h, t.shape[-1]) for t in (q, k, v))
    idx = jnp.arange(chunk, dtype=jnp.float32)
    rel = idx[:, None] - idx[None, :]
    inner_decay = jnp.where(rel >= 0, jnp.exp(log_gamma[:, None, None] * jnp.maximum(rel, 0.0)), 0.0)
    q_decay = jnp.exp(log_gamma[None, :] * (idx[:, None] + 1.0))
    k_decay = jnp.exp(log_gamma[None, :] * (chunk - 1.0 - idx[:, None]))
    chunk_decay = jnp.exp(log_gamma * chunk)
    o_inner = jnp.einsum('bnhts,bnshe->bnthe', jnp.einsum('bnthd,bnshd->bnhts', q, k) * inner_decay, v)

    def step(state, inp):
        q_c, k_c, v_c = inp
        o_c = jnp.einsum('bthd,bhde->bthe', q_c, state)
        state = state * chunk_decay[:, None, None] + jnp.einsum('bshd,bshe->bhde', k_c, v_c)
        return state, o_c

    xs = (jnp.moveaxis(q * q_decay[:, :, None], 1, 0), jnp.moveaxis(k * k_decay[:, :, None], 1, 0),
          jnp.moveaxis(v, 1, 0))
    _, o_cross = lax.scan(step, jnp.zeros((bsz, h, dk, dv), jnp.float32), xs)
    return (o_inner + jnp.moveaxis(o_cross, 0, 1)).reshape(bsz, l, h, dv)


def retention_mixer(u, positions, w_in, w_out):
    bsz, l, _ = u.shape
    q, k, v, gate = jnp.split(u @ w_in, [RET_QK_DIM, 2 * RET_QK_DIM, 2 * RET_QK_DIM + RET_V_DIM], axis=-1)
    cos, sin = rope_angles(positions, RET_QK_HEAD)
    q = apply_rope(q.reshape(bsz, l, RET_HEADS, RET_QK_HEAD), cos, sin)
    k = apply_rope(k.reshape(bsz, l, RET_HEADS, RET_QK_HEAD), cos, sin) * RET_QK_HEAD ** -0.5
    v = v.reshape(bsz, l, RET_HEADS, RET_V_HEAD)
    log_gamma = jnp.log1p(-jnp.exp2(-5.0 - jnp.arange(RET_HEADS, dtype=jnp.float32)))
    o = rms_norm(retention_chunked(q, k, v, log_gamma, RET_CHUNK))
    o = jax.nn.silu(gate.astype(jnp.float32)) * o.reshape(bsz, l, RET_V_DIM)
    return o.astype(u.dtype) @ w_out


def dsa_mixer(u, positions, w_in, w_out):
    bsz, l, _ = u.shape
    topk = min(DSA_TOPK_MAX, l // 4)
    group = DSA_HEADS // DSA_KV_HEADS
    q, k, v, qi, ki, wi = jnp.split(
        u @ w_in,
        [DSA_Q_DIM, DSA_Q_DIM + DSA_KV_DIM, DSA_Q_DIM + 2 * DSA_KV_DIM,
         DSA_Q_DIM + 2 * DSA_KV_DIM + IDX_HEADS * IDX_HEAD_DIM,
         DSA_Q_DIM + 2 * DSA_KV_DIM + IDX_HEADS * IDX_HEAD_DIM + IDX_HEAD_DIM], axis=-1)
    cos, sin = rope_angles(positions, DSA_HEAD_DIM)
    q = apply_rope(q.reshape(bsz, l, DSA_HEADS, DSA_HEAD_DIM), cos, sin)
    k = apply_rope(k.reshape(bsz, l, DSA_KV_HEADS, DSA_HEAD_DIM), cos, sin)
    kv = jnp.concatenate([k, v.reshape(bsz, l, DSA_KV_HEADS, DSA_HEAD_DIM)], axis=-1)
    cos_i, sin_i = rope_angles(positions, IDX_HEAD_DIM)
    qi = apply_rope(qi.reshape(bsz, l, IDX_HEADS, IDX_HEAD_DIM), cos_i, sin_i)
    ki = apply_rope(ki.reshape(bsz, l, 1, IDX_HEAD_DIM), cos_i, sin_i)[:, :, 0].astype(jnp.float32)
    wi = wi.astype(jnp.float32) * (IDX_HEADS ** -0.5 * IDX_HEAD_DIM ** -0.5)
    n_blocks = l // DSA_BLOCK
    blocks = lambda t: jnp.moveaxis(t.reshape(bsz, n_blocks, DSA_BLOCK, *t.shape[2:]), 1, 0)
    key_idx = jnp.arange(l)

    def attend_block(inp):
        qb, qib, wib, tb = inp
        score = jax.nn.relu(jnp.einsum('bthd,bsd->bths', qib.astype(jnp.float32), ki))
        score = jnp.einsum('bths,bth->bts', score, wib)
        score = jnp.where((key_idx[None, :] <= tb[:, None])[None], score, -jnp.inf)
        _, sel = lax.top_k(score, topk)
        valid = sel <= tb[None, :, None]
        kv_sel = jax.vmap(lambda kvb, ib: kvb[ib])(kv, sel).astype(jnp.float32)
        k_sel, v_sel = kv_sel[..., :DSA_HEAD_DIM], kv_sel[..., DSA_HEAD_DIM:]
        qg = qb.astype(jnp.float32).reshape(bsz, DSA_BLOCK, DSA_KV_HEADS, group, DSA_HEAD_DIM)
        logits = jnp.einsum('bthgd,btshd->bthgs', qg, k_sel) * DSA_HEAD_DIM ** -0.5
        logits = jnp.where(valid[:, :, None, None, :], logits, -jnp.inf)
        probs = jax.nn.softmax(logits, axis=-1)
        out = jnp.einsum('bthgs,btshd->bthgd', probs, v_sel)
        return out.reshape(bsz, DSA_BLOCK, DSA_Q_DIM).astype(u.dtype)

    o = lax.map(attend_block, (blocks(q), blocks(qi), blocks(wi), key_idx.reshape(n_blocks, DSA_BLOCK)))
    return jnp.moveaxis(o, 0, 1).reshape(bsz, l, DSA_Q_DIM) @ w_out


def _layer_count(kind):
    return len(range(kind, DEPTH, N_MIXERS))


def setup_inputs(seed: int = 0) -> dict:
    key = jax.random.key(seed)
    ks = jax.random.split(key, 24)
    f32 = jnp.float32
    dense = lambda k, shape: jax.random.normal(k, shape, f32) * shape[-2] ** -0.5
    gain = lambda k, shape: 1.0 + 0.02 * jax.random.normal(k, shape, f32)
    n_a, n_b, n_c = _layer_count(0), _layer_count(1), _layer_count(2)
    x = jax.random.normal(ks[0], (BATCH, SEQ, D_MODEL), f32)
    p = jax.random.normal(ks[1], (DEPTH, BATCH, SEQ, PLE_DIM), f32)
    offset = jax.random.randint(ks[2], (BATCH, 1), 0, MAX_POS_OFFSET, jnp.int32)
    positions = offset + jnp.arange(SEQ, dtype=jnp.int32)[None, :]
    a_log = jnp.log(jax.random.uniform(ks[10], (n_a, GDN_V_HEADS), f32, 1.0, 16.0))
    dt = jnp.exp(jax.random.uniform(ks[11], (n_a, GDN_V_HEADS), f32, math.log(1e-3), math.log(1e-1)))
    dt_bias = dt + jnp.log(-jnp.expm1(-dt))
    return {
        'x': x,
        'p': p,
        'positions': positions,
        'norm_mix': gain(ks[3], (DEPTH, D_MODEL)),
        'norm_mlp': gain(ks[4], (DEPTH, D_MODEL)),
        'norm_ple': gain(ks[5], (DEPTH, D_MODEL)),
        'norm_final': gain(ks[6], (D_MODEL,)),
        'gdn_w_in': dense(ks[7], (n_a, D_MODEL, GDN_IN_DIM)),
        'gdn_conv_w': jax.random.normal(ks[8], (n_a, GDN_CONV, GDN_CONV_DIM), f32) * GDN_CONV ** -0.5,
        'gdn_a_log': a_log,
        'gdn_dt_bias': dt_bias,
        'gdn_norm': gain(ks[9], (n_a, GDN_HEAD_DIM)),
        'gdn_w_out': dense(ks[12], (n_a, GDN_VAL_DIM, D_MODEL)),
        'ret_w_in': dense(ks[13], (n_b, D_MODEL, RET_IN_DIM)),
        'ret_w_out': dense(ks[14], (n_b, RET_V_DIM, D_MODEL)),
        'dsa_w_in': dense(ks[15], (n_c, D_MODEL, DSA_IN_DIM)),
        'dsa_w_out': dense(ks[16], (n_c, DSA_Q_DIM, D_MODEL)),
        'mlp_w_up': dense(ks[17], (DEPTH, D_MODEL, D_FF)),
        'mlp_w_down': dense(ks[18], (DEPTH, D_FF, D_MODEL)),
        'ple_w_gate': dense(ks[19], (DEPTH, D_MODEL, D_MODEL)),
        'ple_w_proj': dense(ks[20], (DEPTH, PLE_DIM, D_MODEL)),
    }


def reference(x, p, positions, norm_mix, norm_mlp, norm_ple, norm_final, gdn_w_in, gdn_conv_w, gdn_a_log,
              gdn_dt_bias, gdn_norm, gdn_w_out, ret_w_in, ret_w_out, dsa_w_in, dsa_w_out, mlp_w_up, mlp_w_down,
              ple_w_gate, ple_w_proj):
    h = x
    for i in range(DEPTH):
        kind, j = i % N_MIXERS, i // N_MIXERS
        u = rms_norm(h, norm_mix[i]).astype(h.dtype)
        if kind == 0:
            y = gdn_mixer(u, gdn_w_in[j], gdn_conv_w[j], gdn_a_log[j], gdn_dt_bias[j], gdn_norm[j], gdn_w_out[j])
        elif kind == 1:
            y = retention_mixer(u, positions, ret_w_in[j], ret_w_out[j])
        else:
            y = dsa_mixer(u, positions, dsa_w_in[j], dsa_w_out[j])
        h = h + y
        u = rms_norm(h, norm_mlp[i]).astype(h.dtype)
        h = h + jnp.square(jax.nn.relu(u @ mlp_w_up[i])) @ mlp_w_down[i]
        u = rms_norm(h, norm_ple[i]).astype(h.dtype)
        h = h + jax.nn.sigmoid(u @ ple_w_gate[i]) * (p[i] @ ple_w_proj[i])
    return rms_norm(h, norm_final).astype(x.dtype)
```

```python
import functools
import math

import jax
import jax.numpy as jnp
from jax import lax
from jax.experimental import pallas as pl
from jax.experimental.pallas import tpu as pltpu

F32 = jnp.float32
BF16 = jnp.bfloat16

D_MODEL = 2048
DEPTH = 4
N_MIXERS = 3
NORM_EPS = 1e-6
ROPE_THETA = 10000.0

GDN_HEAD_DIM = 128
GDN_QK_HEADS = D_MODEL // 128
GDN_V_HEADS = 2 * GDN_QK_HEADS
GDN_CHUNK = 64
GDN_KEY_DIM = GDN_QK_HEADS * GDN_HEAD_DIM
GDN_VAL_DIM = GDN_V_HEADS * GDN_HEAD_DIM
GDN_CONV_DIM = 2 * GDN_KEY_DIM + GDN_VAL_DIM

RET_HEADS = 8
RET_QK_DIM = D_MODEL
RET_V_DIM = 2 * D_MODEL
RET_QK_HEAD = RET_QK_DIM // RET_HEADS
RET_V_HEAD = RET_V_DIM // RET_HEADS
RET_CHUNK = 128

DSA_HEAD_DIM = 128
DSA_HEADS = D_MODEL // DSA_HEAD_DIM
DSA_KV_HEADS = 4
IDX_HEADS = 16
IDX_HEAD_DIM = 64
DSA_TOPK_MAX = 256
DSA_BLOCK = 128
DSA_Q_DIM = DSA_HEADS * DSA_HEAD_DIM
DSA_KV_DIM = DSA_KV_HEADS * DSA_HEAD_DIM

VMEM_LIMIT_BYTES = 56 * 1024 * 1024


def _mm_kernel(x_ref, w_ref, o_ref, *, nk):
    prod = jnp.dot(x_ref[...].astype(BF16), w_ref[...].astype(BF16), preferred_element_type=F32)
    if nk == 1:
        o_ref[...] = prod
    else:
        k = pl.program_id(2)

        @pl.when(k == 0)
        def _():
            o_ref[...] = prod

        @pl.when(k > 0)
        def _():
            o_ref[...] += prod


def _matmul(x, w, *, bm, bn, bk, n_outer=True):
    m, kdim = x.shape
    _, n = w.shape
    assert m % bm == 0 and n % bn == 0 and kdim % bk == 0
    nk = kdim // bk
    if n_outer:
        grid = (n // bn, m // bm, nk)
        x_map = lambda j, i, k: (i, k)
        w_map = lambda j, i, k: (k, j)
        o_map = lambda j, i, k: (i, j)
    else:
        grid = (m // bm, n // bn, nk)
        x_map = lambda i, j, k: (i, k)
        w_map = lambda i, j, k: (k, j)
        o_map = lambda i, j, k: (i, j)
    return pl.pallas_call(
        functools.partial(_mm_kernel, nk=nk),
        out_shape=jax.ShapeDtypeStruct((m, n), F32),
        grid=grid,
        in_specs=[pl.BlockSpec((bm, bk), x_map), pl.BlockSpec((bk, bn), w_map)],
        out_specs=pl.BlockSpec((bm, bn), o_map),
        compiler_params=pltpu.CompilerParams(
            dimension_semantics=("arbitrary", "arbitrary", "arbitrary"),
            vmem_limit_bytes=VMEM_LIMIT_BYTES),
        name="matmul",
    )(x, w)


def _pad_cols(w, mult):
    n = w.shape[-1]
    pad = (-n) % mult
    if pad:
        w = jnp.pad(w, ((0, 0), (0, pad)))
    return w


def _proj(x, w):
    m, kdim = x.shape
    n = w.shape[1]
    wp = _pad_cols(w, 512)
    npad = wp.shape[1]
    bn = 1024 if npad % 1024 == 0 else 512
    if kdim <= 2048:
        out = _matmul(x, wp, bm=1024, bn=bn, bk=kdim, n_outer=True)
    else:
        out = _matmul(x, wp, bm=2048, bn=bn, bk=512, n_outer=False)
    return out[:, :n] if npad != n else out


def _rms_norm(x, gain=None):
    xf = x.astype(F32)
    y = xf * lax.rsqrt(jnp.mean(xf * xf, axis=-1, keepdims=True) + NORM_EPS)
    if gain is not None:
        y = y * gain.astype(F32)
    return y


def _l2_normalize(x):
    xf = x.astype(F32)
    return xf * lax.rsqrt(jnp.sum(xf * xf, axis=-1, keepdims=True) + NORM_EPS)


def _rope_angles(positions, dim):
    inv_freq = ROPE_THETA ** (-jnp.arange(0, dim, 2, dtype=F32) / dim)
    ang = positions.astype(F32)[..., None] * inv_freq
    return jnp.cos(ang), jnp.sin(ang)


def _apply_rope(x, cos, sin):
    x1, x2 = jnp.split(x.astype(F32), 2, axis=-1)
    c, s = cos[:, :, None, :], sin[:, :, None, :]
    return jnp.concatenate([x1 * c - x2 * s, x2 * c + x1 * s], axis=-1).astype(x.dtype)


def _causal_conv_silu(x, w):
    k_w, ch = w.shape
    y = lax.conv_general_dilated(x, w[:, None, :].astype(x.dtype), window_strides=(1,), padding=[(k_w - 1, 0)],
                                 dimension_numbers=('NWC', 'WIO', 'NWC'), feature_group_count=ch)
    return jax.nn.silu(y)


def _gated_delta_rule_chunked(q, k, v, g, beta, chunk):
    bsz, h, l, dk = q.shape
    dv = v.shape[-1]
    n = l // chunk
    rs = lambda t: t.reshape(bsz, h, n, chunk, *t.shape[3:])
    q, k, v, g, beta = rs(q), rs(k), rs(v), rs(g), rs(beta)
    g = jnp.cumsum(g, axis=-1)
    tri_incl = jnp.tril(jnp.ones((chunk, chunk), bool))
    tri_strict = jnp.tril(jnp.ones((chunk, chunk), bool), -1)
    decay = jnp.exp(jnp.where(tri_incl, g[..., :, None] - g[..., None, :], -jnp.inf))
    k_beta = k * beta[..., None]
    v_beta = v * beta[..., None]
    a = jnp.where(tri_strict, jnp.einsum('bhncd,bhnsd->bhncs', k_beta, k) * decay, 0.0)
    rhs = jnp.concatenate([v_beta, k_beta * jnp.exp(g)[..., None]], axis=-1)
    sol = lax.linalg.triangular_solve(a + jnp.eye(chunk, dtype=F32), rhs,
                                      left_side=True, lower=True, unit_diagonal=True)
    w_val, k_cum = sol[..., :dv], sol[..., dv:]
    intra = jnp.einsum('bhncd,bhnsd->bhncs', q, k) * decay

    def step(state, inp):
        q_c, k_c, w_c, kc_c, g_c, at_c = inp
        v_new = w_c - jnp.einsum('bhcd,bhde->bhce', kc_c, state)
        o_c = (jnp.einsum('bhcd,bhde->bhce', q_c * jnp.exp(g_c)[..., None], state)
               + jnp.einsum('bhcs,bhse->bhce', at_c, v_new))
        g_last = g_c[..., -1]
        state = (state * jnp.exp(g_last)[..., None, None]
                 + jnp.einsum('bhcd,bhce->bhde', k_c * jnp.exp(g_last[..., None] - g_c)[..., None], v_new))
        return state, o_c

    xs = tuple(jnp.moveaxis(t, 2, 0) for t in (q, k, w_val, k_cum, g, intra))
    _, o = lax.scan(step, jnp.zeros((bsz, h, dk, dv), F32), xs)
    return jnp.moveaxis(o, 0, 2).reshape(bsz, h, l, dv)


def _gdn_mixer(u, w_in, conv_w, a_log, dt_bias, norm_w, w_out):
    bsz, l, _ = u.shape
    y = _proj(u.reshape(bsz * l, -1), w_in).reshape(bsz, l, -1)
    qkv, z, beta_logit, a = jnp.split(
        y, [GDN_CONV_DIM, GDN_CONV_DIM + GDN_VAL_DIM, GDN_CONV_DIM + GDN_VAL_DIM + GDN_V_HEADS], axis=-1)
    qkv = _causal_conv_silu(qkv, conv_w)
    q, k, v = jnp.split(qkv, [GDN_KEY_DIM, 2 * GDN_KEY_DIM], axis=-1)
    rep = GDN_V_HEADS // GDN_QK_HEADS
    q = jnp.repeat(_l2_normalize(q.reshape(bsz, l, GDN_QK_HEADS, GDN_HEAD_DIM)), rep, axis=2) * GDN_HEAD_DIM ** -0.5
    k = jnp.repeat(_l2_normalize(k.reshape(bsz, l, GDN_QK_HEADS, GDN_HEAD_DIM)), rep, axis=2)
    v = v.reshape(bsz, l, GDN_V_HEADS, GDN_HEAD_DIM).astype(F32)
    beta = jax.nn.sigmoid(beta_logit.astype(F32))
    g = -jnp.exp(a_log.astype(F32)) * jax.nn.softplus(a.astype(F32) + dt_bias.astype(F32))
    bhl = lambda t: jnp.swapaxes(t, 1, 2)
    o = _gated_delta_rule_chunked(bhl(q), bhl(k), bhl(v), bhl(g), bhl(beta), GDN_CHUNK)
    o = _rms_norm(bhl(o), norm_w) * jax.nn.silu(z.astype(F32)).reshape(bsz, l, GDN_V_HEADS, GDN_HEAD_DIM)
    return _proj(o.reshape(bsz * l, GDN_VAL_DIM), w_out).reshape(bsz, l, D_MODEL)


def _retention_chunked(q, k, v, log_gamma, chunk):
    bsz, l, h, dk = q.shape
    dv = v.shape[-1]
    n = l // chunk
    q, k, v = (t.astype(F32).reshape(bsz, n, chunk, h, t.shape[-1]) for t in (q, k, v))
    idx = jnp.arange(chunk, dtype=F32)
    rel = idx[:, None] - idx[None, :]
    inner_decay = jnp.where(rel >= 0, jnp.exp(log_gamma[:, None, None] * jnp.maximum(rel, 0.0)), 0.0)
    q_decay = jnp.exp(log_gamma[None, :] * (idx[:, None] + 1.0))
    k_decay = jnp.exp(log_gamma[None, :] * (chunk - 1.0 - idx[:, None]))
    chunk_decay = jnp.exp(log_gamma * chunk)
    o_inner = jnp.einsum('bnhts,bnshe->bnthe', jnp.einsum('bnthd,bnshd->bnhts', q, k) * inner_decay, v)

    def step(state, inp):
        q_c, k_c, v_c = inp
        o_c = jnp.einsum('bthd,bhde->bthe', q_c, state)
        state = state * chunk_decay[:, None, None] + jnp.einsum('bshd,bshe->bhde', k_c, v_c)
        return state, o_c

    xs = (jnp.moveaxis(q * q_decay[:, :, None], 1, 0), jnp.moveaxis(k * k_decay[:, :, None], 1, 0),
          jnp.moveaxis(v, 1, 0))
    _, o_cross = lax.scan(step, jnp.zeros((bsz, h, dk, dv), F32), xs)
    return (o_inner + jnp.moveaxis(o_cross, 0, 1)).reshape(bsz, l, h, dv)


def _retention_mixer(u, positions, w_in, w_out):
    bsz, l, _ = u.shape
    y = _proj(u.reshape(bsz * l, -1), w_in).reshape(bsz, l, -1)
    q, k, v, gate = jnp.split(y, [RET_QK_DIM, 2 * RET_QK_DIM, 2 * RET_QK_DIM + RET_V_DIM], axis=-1)
    cos, sin = _rope_angles(positions, RET_QK_HEAD)
    q = _apply_rope(q.reshape(bsz, l, RET_HEADS, RET_QK_HEAD), cos, sin)
    k = _apply_rope(k.reshape(bsz, l, RET_HEADS, RET_QK_HEAD), cos, sin) * RET_QK_HEAD ** -0.5
    v = v.reshape(bsz, l, RET_HEADS, RET_V_HEAD)
    log_gamma = jnp.log1p(-jnp.exp2(-5.0 - jnp.arange(RET_HEADS, dtype=F32)))
    o = _rms_norm(_retention_chunked(q, k, v, log_gamma, RET_CHUNK))
    o = jax.nn.silu(gate.astype(F32)) * o.reshape(bsz, l, RET_V_DIM)
    return _proj(o.reshape(bsz * l, RET_V_DIM), w_out).reshape(bsz, l, D_MODEL)


def _dsa_mixer(u, positions, w_in, w_out):
    bsz, l, _ = u.shape
    topk = min(DSA_TOPK_MAX, l // 4)
    group = DSA_HEADS // DSA_KV_HEADS
    y = _proj(u.reshape(bsz * l, -1), w_in).reshape(bsz, l, -1)
    q, k, v, qi, ki, wi = jnp.split(
        y,
        [DSA_Q_DIM, DSA_Q_DIM + DSA_KV_DIM, DSA_Q_DIM + 2 * DSA_KV_DIM,
         DSA_Q_DIM + 2 * DSA_KV_DIM + IDX_HEADS * IDX_HEAD_DIM,
         DSA_Q_DIM + 2 * DSA_KV_DIM + IDX_HEADS * IDX_HEAD_DIM + IDX_HEAD_DIM], axis=-1)
    cos, sin = _rope_angles(positions, DSA_HEAD_DIM)
    q = _apply_rope(q.reshape(bsz, l, DSA_HEADS, DSA_HEAD_DIM), cos, sin)
    k = _apply_rope(k.reshape(bsz, l, DSA_KV_HEADS, DSA_HEAD_DIM), cos, sin)
    kv = jnp.concatenate([k, v.reshape(bsz, l, DSA_KV_HEADS, DSA_HEAD_DIM)], axis=-1)
    cos_i, sin_i = _rope_angles(positions, IDX_HEAD_DIM)
    qi = _apply_rope(qi.reshape(bsz, l, IDX_HEADS, IDX_HEAD_DIM), cos_i, sin_i)
    ki = _apply_rope(ki.reshape(bsz, l, 1, IDX_HEAD_DIM), cos_i, sin_i)[:, :, 0].astype(F32)
    wi = wi.astype(F32) * (IDX_HEADS ** -0.5 * IDX_HEAD_DIM ** -0.5)
    n_blocks = l // DSA_BLOCK
    blocks = lambda t: jnp.moveaxis(t.reshape(bsz, n_blocks, DSA_BLOCK, *t.shape[2:]), 1, 0)
    key_idx = jnp.arange(l)

    def attend_block(inp):
        qb, qib, wib, tb = inp
        score = jax.nn.relu(jnp.einsum('bthd,bsd->bths', qib.astype(F32), ki))
        score = jnp.einsum('bths,bth->bts', score, wib)
        score = jnp.where((key_idx[None, :] <= tb[:, None])[None], score, -jnp.inf)
        _, sel = lax.top_k(score, topk)
        valid = sel <= tb[None, :, None]
        kv_sel = jax.vmap(lambda kvb, ib: kvb[ib])(kv, sel).astype(F32)
        k_sel, v_sel = kv_sel[..., :DSA_HEAD_DIM], kv_sel[..., DSA_HEAD_DIM:]
        qg = qb.astype(F32).reshape(bsz, DSA_BLOCK, DSA_KV_HEADS, group, DSA_HEAD_DIM)
        logits = jnp.einsum('bthgd,btshd->bthgs', qg, k_sel) * DSA_HEAD_DIM ** -0.5
        logits = jnp.where(valid[:, :, None, None, :], logits, -jnp.inf)
        probs = jax.nn.softmax(logits, axis=-1)
        out = jnp.einsum('bthgs,btshd->bthgd', probs, v_sel)
        return out.reshape(bsz, DSA_BLOCK, DSA_Q_DIM).astype(u.dtype)

    o = lax.map(attend_block, (blocks(q), blocks(qi), blocks(wi), key_idx.reshape(n_blocks, DSA_BLOCK)))
    o = jnp.moveaxis(o, 0, 1).reshape(bsz * l, DSA_Q_DIM)
    return _proj(o, w_out).reshape(bsz, l, D_MODEL)


def kernel(x, p, positions, norm_mix, norm_mlp, norm_ple, norm_final, gdn_w_in, gdn_conv_w, gdn_a_log,
           gdn_dt_bias, gdn_norm, gdn_w_out, ret_w_in, ret_w_out, dsa_w_in, dsa_w_out, mlp_w_up, mlp_w_down,
           ple_w_gate, ple_w_proj):
    bsz, l, d = x.shape
    h = x
    for i in range(DEPTH):
        kind, j = i % N_MIXERS, i // N_MIXERS
        u = _rms_norm(h, norm_mix[i]).astype(h.dtype)
        if kind == 0:
            y = _gdn_mixer(u, gdn_w_in[j], gdn_conv_w[j], gdn_a_log[j], gdn_dt_bias[j], gdn_norm[j], gdn_w_out[j])
        elif kind == 1:
            y = _retention_mixer(u, positions, ret_w_in[j], ret_w_out[j])
        else:
            y = _dsa_mixer(u, positions, dsa_w_in[j], dsa_w_out[j])
        h = h + y
        u = _rms_norm(h, norm_mlp[i]).astype(h.dtype)
        a = jnp.square(jax.nn.relu(_proj(u.reshape(bsz * l, d), mlp_w_up[i])))
        h = h + _proj(a, mlp_w_down[i]).reshape(bsz, l, d)
        u = _rms_norm(h, norm_ple[i]).astype(h.dtype)
        gate = jax.nn.sigmoid(_proj(u.reshape(bsz * l, d), ple_w_gate[i]))
        proj = _proj(p[i].reshape(bsz * l, -1), ple_w_proj[i])
        h = h + (gate * proj).reshape(bsz, l, d)
    return _rms_norm(h, norm_final).astype(x.dtype)
```

```python
import functools
import math

import jax
import jax.numpy as jnp
from jax import lax
from jax.experimental import pallas as pl
from jax.experimental.pallas import tpu as pltpu

F32 = jnp.float32
BF16 = jnp.bfloat16

D_MODEL = 2048
DEPTH = 4
N_MIXERS = 3
NORM_EPS = 1e-6
ROPE_THETA = 10000.0

GDN_HEAD_DIM = 128
GDN_QK_HEADS = D_MODEL // 128
GDN_V_HEADS = 2 * GDN_QK_HEADS
GDN_CHUNK = 64
GDN_KEY_DIM = GDN_QK_HEADS * GDN_HEAD_DIM
GDN_VAL_DIM = GDN_V_HEADS * GDN_HEAD_DIM
GDN_CONV_DIM = 2 * GDN_KEY_DIM + GDN_VAL_DIM

RET_HEADS = 8
RET_QK_DIM = D_MODEL
RET_V_DIM = 2 * D_MODEL
RET_QK_HEAD = RET_QK_DIM // RET_HEADS
RET_V_HEAD = RET_V_DIM // RET_HEADS
RET_CHUNK = 128

DSA_HEAD_DIM = 128
DSA_HEADS = D_MODEL // DSA_HEAD_DIM
DSA_KV_HEADS = 4
IDX_HEADS = 16
IDX_HEAD_DIM = 64
DSA_TOPK_MAX = 256
DSA_BLOCK = 128
DSA_KV_STEP = 512
LANES = 128
DSA_Q_DIM = DSA_HEADS * DSA_HEAD_DIM
DSA_KV_DIM = DSA_KV_HEADS * DSA_HEAD_DIM

VMEM_LIMIT_BYTES = 56 * 1024 * 1024


def _mm_kernel(x_ref, w_ref, o_ref, *, nk):
    prod = jnp.dot(x_ref[...].astype(BF16), w_ref[...].astype(BF16), preferred_element_type=F32)
    if nk == 1:
        o_ref[...] = prod
    else:
        k = pl.program_id(2)

        @pl.when(k == 0)
        def _():
            o_ref[...] = prod

        @pl.when(k > 0)
        def _():
            o_ref[...] += prod


def _matmul(x, w, *, bm, bn, bk, n_outer=True):
    m, kdim = x.shape
    _, n = w.shape
    assert m % bm == 0 and n % bn == 0 and kdim % bk == 0
    nk = kdim // bk
    if n_outer:
        grid = (n // bn, m // bm, nk)
        x_map = lambda j, i, k: (i, k)
        w_map = lambda j, i, k: (k, j)
        o_map = lambda j, i, k: (i, j)
    else:
        grid = (m // bm, n // bn, nk)
        x_map = lambda i, j, k: (i, k)
        w_map = lambda i, j, k: (k, j)
        o_map = lambda i, j, k: (i, j)
    return pl.pallas_call(
        functools.partial(_mm_kernel, nk=nk),
        out_shape=jax.ShapeDtypeStruct((m, n), F32),
        grid=grid,
        in_specs=[pl.BlockSpec((bm, bk), x_map), pl.BlockSpec((bk, bn), w_map)],
        out_specs=pl.BlockSpec((bm, bn), o_map),
        compiler_params=pltpu.CompilerParams(
            dimension_semantics=("arbitrary", "arbitrary", "arbitrary"),
            vmem_limit_bytes=VMEM_LIMIT_BYTES),
        name="matmul",
    )(x, w)


def _pad_cols(w, mult):
    n = w.shape[-1]
    pad = (-n) % mult
    if pad:
        w = jnp.pad(w, ((0, 0), (0, pad)))
    return w


def _proj(x, w, keep_pad=False):
    m, kdim = x.shape
    n = w.shape[1]
    wp = _pad_cols(w, 512)
    npad = wp.shape[1]
    bn = 1024 if npad % 1024 == 0 else 512
    if kdim <= 2048:
        out = _matmul(x, wp, bm=min(m, 1024), bn=bn, bk=kdim, n_outer=True)
    else:
        out = _matmul(x, wp, bm=min(m, 2048), bn=bn, bk=512, n_outer=False)
    return out[:, :n] if (npad != n and not keep_pad) else out


def _rms_norm(x, gain=None):
    xf = x.astype(F32)
    y = xf * lax.rsqrt(jnp.mean(xf * xf, axis=-1, keepdims=True) + NORM_EPS)
    if gain is not None:
        y = y * gain.astype(F32)
    return y


def _l2_normalize(x):
    xf = x.astype(F32)
    return xf * lax.rsqrt(jnp.sum(xf * xf, axis=-1, keepdims=True) + NORM_EPS)


def _rope_angles(positions, dim):
    inv_freq = ROPE_THETA ** (-jnp.arange(0, dim, 2, dtype=F32) / dim)
    ang = positions.astype(F32)[..., None] * inv_freq
    return jnp.cos(ang), jnp.sin(ang)


def _apply_rope(x, cos, sin):
    x1, x2 = jnp.split(x.astype(F32), 2, axis=-1)
    c, s = cos[:, :, None, :], sin[:, :, None, :]
    return jnp.concatenate([x1 * c - x2 * s, x2 * c + x1 * s], axis=-1).astype(x.dtype)


def _causal_conv_silu(x, w):
    k_w, ch = w.shape
    y = lax.conv_general_dilated(x, w[:, None, :].astype(x.dtype), window_strides=(1,), padding=[(k_w - 1, 0)],
                                 dimension_numbers=('NWC', 'WIO', 'NWC'), feature_group_count=ch)
    return jax.nn.silu(y)


def _gated_delta_rule_chunked(q, k, v, g, beta, chunk):
    bsz, h, l, dk = q.shape
    dv = v.shape[-1]
    n = l // chunk
    rs = lambda t: t.reshape(bsz, h, n, chunk, *t.shape[3:])
    q, k, v, g, beta = rs(q), rs(k), rs(v), rs(g), rs(beta)
    g = jnp.cumsum(g, axis=-1)
    tri_incl = jnp.tril(jnp.ones((chunk, chunk), bool))
    tri_strict = jnp.tril(jnp.ones((chunk, chunk), bool), -1)
    decay = jnp.exp(jnp.where(tri_incl, g[..., :, None] - g[..., None, :], -jnp.inf))
    k_beta = k * beta[..., None]
    v_beta = v * beta[..., None]
    a = jnp.where(tri_strict, jnp.einsum('bhncd,bhnsd->bhncs', k_beta, k) * decay, 0.0)
    rhs = jnp.concatenate([v_beta, k_beta * jnp.exp(g)[..., None]], axis=-1)
    sol = lax.linalg.triangular_solve(a + jnp.eye(chunk, dtype=F32), rhs,
                                      left_side=True, lower=True, unit_diagonal=True)
    w_val, k_cum = sol[..., :dv], sol[..., dv:]
    intra = jnp.einsum('bhncd,bhnsd->bhncs', q, k) * decay

    def step(state, inp):
        q_c, k_c, w_c, kc_c, g_c, at_c = inp
        v_new = w_c - jnp.einsum('bhcd,bhde->bhce', kc_c, state)
        o_c = (jnp.einsum('bhcd,bhde->bhce', q_c * jnp.exp(g_c)[..., None], state)
               + jnp.einsum('bhcs,bhse->bhce', at_c, v_new))
        g_last = g_c[..., -1]
        state = (state * jnp.exp(g_last)[..., None, None]
                 + jnp.einsum('bhcd,bhce->bhde', k_c * jnp.exp(g_last[..., None] - g_c)[..., None], v_new))
        return state, o_c

    xs = tuple(jnp.moveaxis(t, 2, 0) for t in (q, k, w_val, k_cum, g, intra))
    _, o = lax.scan(step, jnp.zeros((bsz, h, dk, dv), F32), xs)
    return jnp.moveaxis(o, 0, 2).reshape(bsz, h, l, dv)


def _gdn_mixer(u, w_in, conv_w, a_log, dt_bias, norm_w, w_out):
    bsz, l, _ = u.shape
    y = _proj(u.reshape(bsz * l, -1), w_in).reshape(bsz, l, -1)
    qkv, z, beta_logit, a = jnp.split(
        y, [GDN_CONV_DIM, GDN_CONV_DIM + GDN_VAL_DIM, GDN_CONV_DIM + GDN_VAL_DIM + GDN_V_HEADS], axis=-1)
    qkv = _causal_conv_silu(qkv, conv_w)
    q, k, v = jnp.split(qkv, [GDN_KEY_DIM, 2 * GDN_KEY_DIM], axis=-1)
    rep = GDN_V_HEADS // GDN_QK_HEADS
    q = jnp.repeat(_l2_normalize(q.reshape(bsz, l, GDN_QK_HEADS, GDN_HEAD_DIM)), rep, axis=2) * GDN_HEAD_DIM ** -0.5
    k = jnp.repeat(_l2_normalize(k.reshape(bsz, l, GDN_QK_HEADS, GDN_HEAD_DIM)), rep, axis=2)
    v = v.reshape(bsz, l, GDN_V_HEADS, GDN_HEAD_DIM).astype(F32)
    beta = jax.nn.sigmoid(beta_logit.astype(F32))
    g = -jnp.exp(a_log.astype(F32)) * jax.nn.softplus(a.astype(F32) + dt_bias.astype(F32))
    bhl = lambda t: jnp.swapaxes(t, 1, 2)
    o = _gated_delta_rule_chunked(bhl(q), bhl(k), bhl(v), bhl(g), bhl(beta), GDN_CHUNK)
    o = _rms_norm(bhl(o), norm_w) * jax.nn.silu(z.astype(F32)).reshape(bsz, l, GDN_V_HEADS, GDN_HEAD_DIM)
    return _proj(o.reshape(bsz * l, GDN_VAL_DIM), w_out).reshape(bsz, l, D_MODEL)


def _retention_chunked(q, k, v, log_gamma, chunk):
    bsz, l, h, dk = q.shape
    dv = v.shape[-1]
    n = l // chunk
    q, k, v = (t.astype(F32).reshape(bsz, n, chunk, h, t.shape[-1]) for t in (q, k, v))
    idx = jnp.arange(chunk, dtype=F32)
    rel = idx[:, None] - idx[None, :]
    inner_decay = jnp.where(rel >= 0, jnp.exp(log_gamma[:, None, None] * jnp.maximum(rel, 0.0)), 0.0)
    q_decay = jnp.exp(log_gamma[None, :] * (idx[:, None] + 1.0))
    k_decay = jnp.exp(log_gamma[None, :] * (chunk - 1.0 - idx[:, None]))
    chunk_decay = jnp.exp(log_gamma * chunk)
    o_inner = jnp.einsum('bnhts,bnshe->bnthe', jnp.einsum('bnthd,bnshd->bnhts', q, k) * inner_decay, v)

    def step(state, inp):
        q_c, k_c, v_c = inp
        o_c = jnp.einsum('bthd,bhde->bthe', q_c, state)
        state = state * chunk_decay[:, None, None] + jnp.einsum('bshd,bshe->bhde', k_c, v_c)
        return state, o_c

    xs = (jnp.moveaxis(q * q_decay[:, :, None], 1, 0), jnp.moveaxis(k * k_decay[:, :, None], 1, 0),
          jnp.moveaxis(v, 1, 0))
    _, o_cross = lax.scan(step, jnp.zeros((bsz, h, dk, dv), F32), xs)
    return (o_inner + jnp.moveaxis(o_cross, 0, 1)).reshape(bsz, l, h, dv)


def _retention_mixer(u, positions, w_in, w_out):
    bsz, l, _ = u.shape
    y = _proj(u.reshape(bsz * l, -1), w_in).reshape(bsz, l, -1)
    q, k, v, gate = jnp.split(y, [RET_QK_DIM, 2 * RET_QK_DIM, 2 * RET_QK_DIM + RET_V_DIM], axis=-1)
    cos, sin = _rope_angles(positions, RET_QK_HEAD)
    q = _apply_rope(q.reshape(bsz, l, RET_HEADS, RET_QK_HEAD), cos, sin)
    k = _apply_rope(k.reshape(bsz, l, RET_HEADS, RET_QK_HEAD), cos, sin) * RET_QK_HEAD ** -0.5
    v = v.reshape(bsz, l, RET_HEADS, RET_V_HEAD)
    log_gamma = jnp.log1p(-jnp.exp2(-5.0 - jnp.arange(RET_HEADS, dtype=F32)))
    o = _rms_norm(_retention_chunked(q, k, v, log_gamma, RET_CHUNK))
    o = jax.nn.silu(gate.astype(F32)) * o.reshape(bsz, l, RET_V_DIM)
    return _proj(o.reshape(bsz * l, RET_V_DIM), w_out).reshape(bsz, l, D_MODEL)


def _rope_tables(positions, dim):
    inv_freq = ROPE_THETA ** (-jnp.arange(0, dim, 2, dtype=F32) / dim)
    ang = positions.astype(F32)[..., None] * inv_freq
    c, s = jnp.cos(ang), jnp.sin(ang)
    reps = LANES // dim
    cos_full = jnp.tile(jnp.concatenate([c, c], axis=-1), (1, 1, reps))
    sin_signed = jnp.tile(jnp.concatenate([-s, s], axis=-1), (1, 1, reps))
    return cos_full, sin_signed


def _rope128(x, cos, sin_signed):
    return x * cos + pltpu.roll(x, 64, 1) * sin_signed


def _rope64(x, cos, sin_signed, low_half):
    partner = jnp.where(low_half, pltpu.roll(x, 96, 1), pltpu.roll(x, 32, 1))
    return x * cos + partner * sin_signed


def _lane_iota(shape):
    return lax.broadcasted_iota(jnp.int32, shape, 1)


def _dsa_prep_kernel(k_ref, v_ref, ki_ref, ck_ref, sk_ref, cik_ref, sik_ref, kb_ref, vb_ref, ki2_ref):
    ck, sk = ck_ref[0], sk_ref[0]
    for g in range(DSA_KV_HEADS):
        sl = slice(g * DSA_HEAD_DIM, (g + 1) * DSA_HEAD_DIM)
        kb_ref[0, g] = _rope128(k_ref[0, :, sl], ck, sk).astype(BF16)
        vb_ref[0, g] = v_ref[0, :, sl].astype(BF16)
    x = ki_ref[0]
    lane = _lane_iota(x.shape)
    xr = _rope64(x, cik_ref[0], sik_ref[0], (lane % IDX_HEAD_DIM) < IDX_HEAD_DIM // 2)
    ki2_ref[0] = jnp.where(lane < IDX_HEAD_DIM, xr, pltpu.roll(xr, IDX_HEAD_DIM, 1)).astype(BF16)


def _dsa_block(s_eff, q_ref, qi_ref, wi_ref, cq_ref, sq_ref, ciq_ref, siq_ref, kb_ref, vb_ref, ki2_ref, o_ref,
               key_ref, *, topk):
    t_blk = DSA_BLOCK
    qb = pl.program_id(1)
    nt = (((1,), (1,)), ((), ()))
    lane = _lane_iota((t_blk, LANES))
    low_half = (lane % IDX_HEAD_DIM) < IDX_HEAD_DIM // 2
    lo64 = lane < IDX_HEAD_DIM

    wi = wi_ref[0] * (IDX_HEADS ** -0.5 * IDX_HEAD_DIM ** -0.5)
    ciq, siq = ciq_ref[0], siq_ref[0]
    ki2 = ki2_ref[0, :s_eff, :]
    score = jnp.zeros((t_blk, s_eff), F32)
    for quad in range(IDX_HEADS // 4):
        rows = []
        for pr in (2 * quad, 2 * quad + 1):
            xr = _rope64(qi_ref[0, :, pr * LANES:(pr + 1) * LANES], ciq, siq, low_half)
            rows.append(jnp.where(lo64, xr, 0.0).astype(BF16))
            rows.append(jnp.where(lo64, 0.0, xr).astype(BF16))
        s4 = lax.dot_general(jnp.concatenate(rows, axis=0), ki2, nt, preferred_element_type=F32)
        for j in range(4):
            hh = 4 * quad + j
            w_col = wi[:, IDX_HEAD_DIM + hh:IDX_HEAD_DIM + hh + 1]
            score = score + w_col * jnp.maximum(s4[j * t_blk:(j + 1) * t_blk], 0.0)

    t_idx = qb * t_blk + lax.broadcasted_iota(jnp.int32, (t_blk, 1), 0)
    s_idx = _lane_iota((t_blk, s_eff))
    score = jnp.where(s_idx <= t_idx, score, -jnp.inf)
    bits = lax.bitcast_convert_type(score, jnp.int32)
    key_ref[:, :s_eff] = bits ^ ((bits >> 31) & jnp.int32(0x7FFFFFFF))

    kk = jnp.minimum(t_idx + 1, topk).astype(F32)
    int_min = jnp.int32(-2 ** 31)

    def bisect(i, t_u):
        cand_u = t_u | lax.shift_left(jnp.int32(1), 31 - i)
        hit = key_ref[:, :s_eff] >= (cand_u ^ int_min)
        cnt = jnp.sum(jnp.where(hit, 1.0, 0.0), axis=1, keepdims=True)
        return jnp.where(cnt >= kk, cand_u, t_u)

    t_u = lax.fori_loop(0, 32, bisect, jnp.zeros((t_blk, 1), jnp.int32))
    sel = key_ref[:, :s_eff] >= (t_u ^ int_min)

    cq, sq = cq_ref[0], sq_ref[0]
    group = DSA_HEADS // DSA_KV_HEADS
    for g in range(DSA_KV_HEADS):
        qs = []
        for j in range(group):
            hh = group * g + j
            qs.append(_rope128(q_ref[0, :, hh * DSA_HEAD_DIM:(hh + 1) * DSA_HEAD_DIM], cq, sq).astype(BF16))
        logits = lax.dot_general(jnp.concatenate(qs, axis=0), kb_ref[0, g, :s_eff, :], nt,
                                 preferred_element_type=F32) * DSA_HEAD_DIM ** -0.5
        ps, dens = [], []
        for j in range(group):
            lj = jnp.where(sel, logits[j * t_blk:(j + 1) * t_blk], -jnp.inf)
            pj = jnp.exp(lj - jnp.max(lj, axis=1, keepdims=True))
            dens.append(jnp.sum(pj, axis=1, keepdims=True))
            ps.append(pj.astype(BF16))
        pv = jnp.dot(jnp.concatenate(ps, axis=0), vb_ref[0, g, :s_eff, :], preferred_element_type=F32)
        for j in range(group):
            hh = group * g + j
            o_ref[0, :, hh * DSA_HEAD_DIM:(hh + 1) * DSA_HEAD_DIM] = (
                pv[j * t_blk:(j + 1) * t_blk] / dens[j]).astype(o_ref.dtype)


def _dsa_kernel(q_ref, qi_ref, wi_ref, cq_ref, sq_ref, ciq_ref, siq_ref, kb_ref, vb_ref, ki2_ref, o_ref, key_ref,
                *, topk, seq):
    qb = pl.program_id(1)
    blocks_per_step = DSA_KV_STEP // DSA_BLOCK
    for br in range(seq // DSA_KV_STEP):
        @pl.when(qb // blocks_per_step == br)
        def _():
            _dsa_block((br + 1) * DSA_KV_STEP, q_ref, qi_ref, wi_ref, cq_ref, sq_ref, ciq_ref, siq_ref, kb_ref,
                       vb_ref, ki2_ref, o_ref, key_ref, topk=topk)


def _dsa_core(y, positions):
    bsz, l, _ = y.shape
    topk = min(DSA_TOPK_MAX, l // 4)
    cos_q, sin_q = _rope_tables(positions, DSA_HEAD_DIM)
    cos_i, sin_i = _rope_tables(positions, IDX_HEAD_DIM)
    k_blk = DSA_Q_DIM // DSA_KV_DIM
    ki_blk = (DSA_Q_DIM + 2 * DSA_KV_DIM + IDX_HEADS * IDX_HEAD_DIM) // LANES
    ps = DSA_KV_STEP
    tab = pl.BlockSpec((1, ps, LANES), lambda b, i: (b, i, 0))
    kb, vb, ki2 = pl.pallas_call(
        _dsa_prep_kernel,
        out_shape=(jax.ShapeDtypeStruct((bsz, DSA_KV_HEADS, l, DSA_HEAD_DIM), BF16),
                   jax.ShapeDtypeStruct((bsz, DSA_KV_HEADS, l, DSA_HEAD_DIM), BF16),
                   jax.ShapeDtypeStruct((bsz, l, LANES), BF16)),
        grid=(bsz, l // ps),
        in_specs=[pl.BlockSpec((1, ps, DSA_KV_DIM), lambda b, i: (b, i, k_blk)),
                  pl.BlockSpec((1, ps, DSA_KV_DIM), lambda b, i: (b, i, k_blk + 1)),
                  pl.BlockSpec((1, ps, LANES), lambda b, i: (b, i, ki_blk)),
                  tab, tab, tab, tab],
        out_specs=(pl.BlockSpec((1, DSA_KV_HEADS, ps, DSA_HEAD_DIM), lambda b, i: (b, 0, i, 0)),
                   pl.BlockSpec((1, DSA_KV_HEADS, ps, DSA_HEAD_DIM), lambda b, i: (b, 0, i, 0)),
                   pl.BlockSpec((1, ps, LANES), lambda b, i: (b, i, 0))),
        compiler_params=pltpu.CompilerParams(dimension_semantics=("arbitrary", "arbitrary"),
                                             vmem_limit_bytes=VMEM_LIMIT_BYTES),
        name="dsa_prep",
    )(y, y, y, cos_q, sin_q, cos_i, sin_i)

    t_blk = DSA_BLOCK
    qtab = pl.BlockSpec((1, t_blk, LANES), lambda b, i: (b, i, 0))
    kv_spec = pl.BlockSpec((1, DSA_KV_HEADS, l, DSA_HEAD_DIM), lambda b, i: (b, 0, 0, 0))
    return pl.pallas_call(
        functools.partial(_dsa_kernel, topk=topk, seq=l),
        out_shape=jax.ShapeDtypeStruct((bsz, l, DSA_Q_DIM), BF16),
        grid=(bsz, l // t_blk),
        in_specs=[pl.BlockSpec((1, t_blk, DSA_Q_DIM), lambda b, i: (b, i, 0)),
                  pl.BlockSpec((1, t_blk, IDX_HEADS * IDX_HEAD_DIM),
                               lambda b, i: (b, i, (DSA_Q_DIM + 2 * DSA_KV_DIM) // (IDX_HEADS * IDX_HEAD_DIM))),
                  pl.BlockSpec((1, t_blk, LANES), lambda b, i: (b, i, ki_blk)),
                  qtab, qtab, qtab, qtab,
                  kv_spec, kv_spec,
                  pl.BlockSpec((1, l, LANES), lambda b, i: (b, 0, 0))],
        out_specs=pl.BlockSpec((1, t_blk, DSA_Q_DIM), lambda b, i: (b, i, 0)),
        scratch_shapes=[pltpu.VMEM((t_blk, l), jnp.int32)],
        compiler_params=pltpu.CompilerParams(dimension_semantics=("arbitrary", "arbitrary"),
                                             vmem_limit_bytes=VMEM_LIMIT_BYTES),
        name="dsa_attn",
    )(y, y, y, cos_q, sin_q, cos_i, sin_i, kb, vb, ki2)


def _dsa_mixer(u, positions, w_in, w_out):
    bsz, l, _ = u.shape
    y = _proj(u.reshape(bsz * l, -1), w_in, keep_pad=True)
    o = _dsa_core(y.reshape(bsz, l, -1), positions)
    return _proj(o.reshape(bsz * l, DSA_Q_DIM), w_out).reshape(bsz, l, D_MODEL)


def kernel(x, p, positions, norm_mix, norm_mlp, norm_ple, norm_final, gdn_w_in, gdn_conv_w, gdn_a_log,
           gdn_dt_bias, gdn_norm, gdn_w_out, ret_w_in, ret_w_out, dsa_w_in, dsa_w_out, mlp_w_up, mlp_w_down,
           ple_w_gate, ple_w_proj):
    bsz, l, d = x.shape
    h = x
    for i in range(DEPTH):
        kind, j = i % N_MIXERS, i // N_MIXERS
        u = _rms_norm(h, norm_mix[i]).astype(h.dtype)
        if kind == 0:
            y = _gdn_mixer(u, gdn_w_in[j], gdn_conv_w[j], gdn_a_log[j], gdn_dt_bias[j], gdn_norm[j], gdn_w_out[j])
        elif kind == 1:
            y = _retention_mixer(u, positions, ret_w_in[j], ret_w_out[j])
        else:
            y = _dsa_mixer(u, positions, dsa_w_in[j], dsa_w_out[j])
        h = h + y
        u = _rms_norm(h, norm_mlp[i]).astype(h.dtype)
        a = jnp.square(jax.nn.relu(_proj(u.reshape(bsz * l, d), mlp_w_up[i])))
        h = h + _proj(a, mlp_w_down[i]).reshape(bsz, l, d)
        u = _rms_norm(h, norm_ple[i]).astype(h.dtype)
        gate = jax.nn.sigmoid(_proj(u.reshape(bsz * l, d), ple_w_gate[i]))
        proj = _proj(p[i].reshape(bsz * l, -1), ple_w_proj[i])
        h = h + (gate * proj).reshape(bsz, l, d)
    return _rms_norm(h, norm_final).astype(x.dtype)
```

```python
import functools

import jax
import jax.numpy as jnp
from jax import lax
from jax.experimental import pallas as pl
from jax.experimental.pallas import tpu as pltpu

F32 = jnp.float32
BF16 = jnp.bfloat16

D_MODEL = 2048
DEPTH = 4
N_MIXERS = 3
NORM_EPS = 1e-6
ROPE_THETA = 10000.0

GDN_HEAD_DIM = 128
GDN_QK_HEADS = D_MODEL // 128
GDN_V_HEADS = 2 * GDN_QK_HEADS
GDN_CONV = 4
GDN_CHUNK = 64
GDN_KEY_DIM = GDN_QK_HEADS * GDN_HEAD_DIM
GDN_VAL_DIM = GDN_V_HEADS * GDN_HEAD_DIM
GDN_CONV_DIM = 2 * GDN_KEY_DIM + GDN_VAL_DIM
GDN_SUPER = 256

RET_HEADS = 8
RET_QK_DIM = D_MODEL
RET_V_DIM = 2 * D_MODEL
RET_QK_HEAD = RET_QK_DIM // RET_HEADS
RET_V_HEAD = RET_V_DIM // RET_HEADS
RET_CHUNK = 128
RET_STEP = 256

DSA_HEAD_DIM = 128
DSA_HEADS = D_MODEL // DSA_HEAD_DIM
DSA_KV_HEADS = 4
IDX_HEADS = 16
IDX_HEAD_DIM = 64
DSA_TOPK_MAX = 256
DSA_BLOCK = 128
DSA_KV_STEP = 512
DSA_Q_DIM = DSA_HEADS * DSA_HEAD_DIM
DSA_KV_DIM = DSA_KV_HEADS * DSA_HEAD_DIM

LANES = 128
SUBLANES = 8
VMEM_LIMIT_BYTES = 56 * 1024 * 1024

_ARB2 = pltpu.CompilerParams(dimension_semantics=("arbitrary", "arbitrary"), vmem_limit_bytes=VMEM_LIMIT_BYTES)


def _bdot(a, b):
    return jnp.dot(a.astype(BF16), b.astype(BF16), preferred_element_type=F32)


def _bdot_nt(a, b):
    return lax.dot_general(a.astype(BF16), b.astype(BF16), (((1,), (1,)), ((), ())), preferred_element_type=F32)


def _bdot_tn(a, b):
    return lax.dot_general(a.astype(BF16), b.astype(BF16), (((0,), (0,)), ((), ())), preferred_element_type=F32)


def _lane_iota(shape):
    return lax.broadcasted_iota(jnp.int32, shape, 1)


def _silu(x):
    return x * jax.nn.sigmoid(x)


def _mm_kernel(*refs, nk, mode):
    x_ref, w_ref = refs[0], refs[1]
    o_ref = refs[-2] if nk > 1 else refs[-1]
    extra = refs[2:-2] if nk > 1 else refs[2:-1]

    def epilogue(acc):
        if mode == "relu2":
            acc = jnp.square(jnp.maximum(acc, 0.0))
        elif mode == "residual":
            acc = extra[0][...] + acc
        elif mode == "ple":
            res_ref, p_ref, wp_ref = extra
            acc = res_ref[...] + jax.nn.sigmoid(acc) * _bdot(p_ref[...], wp_ref[...])
        o_ref[...] = acc.astype(o_ref.dtype)

    prod = _bdot(x_ref[...], w_ref[...])
    if nk == 1:
        epilogue(prod)
    else:
        acc_ref = refs[-1]
        k = pl.program_id(2)

        @pl.when(k == 0)
        def _():
            acc_ref[...] = prod

        @pl.when(k > 0)
        def _():
            acc_ref[...] += prod

        @pl.when(k == nk - 1)
        def _():
            epilogue(acc_ref[...])


def _matmul(x, w, *, bm, bn, bk, n_outer, mode="plain", out_dtype=F32, res=None, p=None, wp=None):
    m, kdim = x.shape
    n = w.shape[1]
    bm = min(bm, m)
    assert m % bm == 0 and n % bn == 0 and kdim % bk == 0, (x.shape, w.shape, bm, bn, bk)
    nk = kdim // bk
    if n_outer:
        grid = (n // bn, m // bm, nk)
        mi, ni = (lambda j, i, k: i), (lambda j, i, k: j)
    else:
        grid = (m // bm, n // bn, nk)
        mi, ni = (lambda i, j, k: i), (lambda i, j, k: j)
    in_specs = [pl.BlockSpec((bm, bk), lambda a, b, k: (mi(a, b, k), k)),
                pl.BlockSpec((bk, bn), lambda a, b, k: (k, ni(a, b, k)))]
    args = [x, w]
    o_map = lambda a, b, k: (mi(a, b, k), ni(a, b, k))
    if mode in ("residual", "ple"):
        in_specs.append(pl.BlockSpec((bm, bn), o_map))
        args.append(res)
    if mode == "ple":
        kp = p.shape[1]
        in_specs.append(pl.BlockSpec((bm, kp), lambda a, b, k: (mi(a, b, k), 0)))
        in_specs.append(pl.BlockSpec((kp, bn), lambda a, b, k: (0, ni(a, b, k))))
        args += [p, wp]
    return pl.pallas_call(
        functools.partial(_mm_kernel, nk=nk, mode=mode),
        out_shape=jax.ShapeDtypeStruct((m, n), out_dtype),
        grid=grid,
        in_specs=in_specs,
        out_specs=pl.BlockSpec((bm, bn), o_map),
        scratch_shapes=[pltpu.VMEM((bm, bn), F32)] if nk > 1 else [],
        compiler_params=pltpu.CompilerParams(
            dimension_semantics=("arbitrary", "arbitrary", "arbitrary"), vmem_limit_bytes=VMEM_LIMIT_BYTES),
        name="matmul_" + mode,
    )(*args)


def _pad_cols(w, mult):
    pad = (-w.shape[-1]) % mult
    return jnp.pad(w, ((0, 0), (0, pad))) if pad else w


def _proj(x, w, **kw):
    kdim, n = w.shape
    if kdim <= 2048:
        wide = n % 1024 == 0 and kw.get("mode", "plain") in ("plain", "relu2")
        return _matmul(x, w, bm=1024, bn=1024 if wide else 512, bk=kdim, n_outer=True, **kw)
    return _matmul(x, w, bm=2048, bn=1024, bk=512, n_outer=False, **kw)


def _rmsnorm_kernel(x_ref, g_ref, o_ref):
    x = x_ref[...]
    y = x * lax.rsqrt(jnp.mean(x * x, axis=1, keepdims=True) + NORM_EPS) * g_ref[...]
    o_ref[...] = y.astype(o_ref.dtype)


def _rmsnorm(x, gain, out_dtype):
    m, d = x.shape
    bm = min(m, 512)
    return pl.pallas_call(
        _rmsnorm_kernel,
        out_shape=jax.ShapeDtypeStruct((m, d), out_dtype),
        grid=(m // bm,),
        in_specs=[pl.BlockSpec((bm, d), lambda i: (i, 0)), pl.BlockSpec((1, d), lambda i: (0, 0))],
        out_specs=pl.BlockSpec((bm, d), lambda i: (i, 0)),
        compiler_params=pltpu.CompilerParams(dimension_semantics=("arbitrary",), vmem_limit_bytes=VMEM_LIMIT_BYTES),
        name="rmsnorm",
    )(x, gain.reshape(1, d))


def _gdn_gates_kernel(gb_ref, alog_ref, dtb_ref, o_ref):
    x = gb_ref[0]
    lane = _lane_iota(x.shape)
    row = lax.broadcasted_iota(jnp.int32, x.shape, 0) % GDN_CHUNK
    g = -jnp.exp(alog_ref[...]) * jax.nn.softplus(x + dtb_ref[...])
    sh = 1
    while sh < GDN_CHUNK:
        g = g + jnp.where(row >= sh, pltpu.roll(g, sh, 0), 0.0)
        sh *= 2
    o_ref[0] = jnp.where(lane < GDN_V_HEADS, jax.nn.sigmoid(x), g)


def _conv_silu_tile(x, tail, w):
    rows8 = lax.broadcasted_iota(jnp.int32, tail.shape, 0)
    acc = x * w[GDN_CONV - 1:GDN_CONV]
    for sh in range(1, GDN_CONV):
        xs = pltpu.roll(x, sh, 0)
        head = jnp.where(rows8 < sh, pltpu.roll(tail, sh, 0), xs[:SUBLANES])
        xs = jnp.concatenate([head, xs[SUBLANES:]], axis=0)
        acc = acc + xs * w[GDN_CONV - 1 - sh:GDN_CONV - sh]
    return _silu(acc)


def _gdn_kernel(q_ref, k_ref, v_ref, z_ref, gt_ref, wq_ref, wk_ref, wv_ref, nw_ref, o_ref, s_ref, *, seq):
    hq = pl.program_id(1)
    r_blk, c_blk, d = GDN_SUPER, GDN_CHUNK, GDN_HEAD_DIM
    ii = lax.broadcasted_iota(jnp.int32, (r_blk, r_blk), 0)
    jj = lax.broadcasted_iota(jnp.int32, (r_blk, r_blk), 1)
    blk_masks = [(ii >> sh) == (jj >> sh) for sh in (3, 4, 5, 6)]
    blk8 = blk_masks[0]
    tri_incl = blk_masks[3] & (ii >= jj)
    diag = ii == jj
    eye = jnp.where(diag, 1.0, 0.0)
    lane = _lane_iota((r_blk, LANES))
    s_ref[...] = jnp.zeros_like(s_ref)
    wq, wk, wv, nw = wq_ref[...], wk_ref[...], wv_ref[...], nw_ref[...]

    def step(s, tails):
        tq, tk, tv = tails
        r0 = pl.multiple_of(s * r_blk, r_blk)
        xq, xk, xv = q_ref[0, pl.ds(r0, r_blk), :], k_ref[0, pl.ds(r0, r_blk), :], v_ref[0, pl.ds(r0, r_blk), :]
        qc, kc, vc = _conv_silu_tile(xq, tq, wq), _conv_silu_tile(xk, tk, wk), _conv_silu_tile(xv, tv, wv)
        qn = qc * lax.rsqrt(jnp.sum(qc * qc, axis=1, keepdims=True) + NORM_EPS) * d ** -0.5
        kn = kc * lax.rsqrt(jnp.sum(kc * kc, axis=1, keepdims=True) + NORM_EPS)
        gt = gt_ref[0, pl.ds(r0, r_blk), :]
        qk = _bdot_nt(qn, kn)
        for j in range(GDN_V_HEADS // GDN_QK_HEADS):
            hv = (GDN_V_HEADS // GDN_QK_HEADS) * hq + j
            beta = jnp.sum(jnp.where(lane == hv, gt, 0.0), axis=1, keepdims=True)
            g = jnp.sum(jnp.where(lane == GDN_V_HEADS + hv, gt, 0.0), axis=1, keepdims=True)
            gi = jnp.broadcast_to(g, (r_blk, r_blk))
            grow = jnp.sum(jnp.where(diag, gi, 0.0), axis=0, keepdims=True)
            dec = jnp.exp(jnp.where(tri_incl, gi - grow, -jnp.inf))
            eg = jnp.exp(g)
            kb = kn * beta
            a = jnp.where(diag, 0.0, _bdot_nt(kb, kn) * dec)
            intra = qk * dec
            a8 = jnp.where(blk8, a, 0.0)
            a8_2 = _bdot(a8, a8)
            inv = eye - a8
            inv = inv + _bdot(inv, a8_2)
            inv = inv + _bdot(inv, _bdot(a8_2, a8_2))
            for lvl in range(3):
                off = jnp.where(blk_masks[lvl + 1] & ~blk_masks[lvl], a, 0.0)
                inv = inv - _bdot(inv, _bdot(off, inv))
            x = _bdot(inv, jnp.concatenate([vc[:, j * d:(j + 1) * d] * beta, kb * eg], axis=1))
            w_val, k_cum = x[:, :d], x[:, d:]
            qg = qn * eg
            gl = gi[:, :d]
            state = s_ref[j]
            for c in range(r_blk // c_blk):
                rs = slice(c * c_blk, (c + 1) * c_blk)
                g_last = gl[(c + 1) * c_blk - 1:(c + 1) * c_blk, :]
                both = _bdot(jnp.concatenate([k_cum[rs], qg[rs]], axis=0), state)
                v_new = w_val[rs] - both[:c_blk]
                parts = []
                if c > 0:
                    parts.append(jnp.zeros((c * c_blk, d), F32))
                parts.append(v_new)
                if (c + 1) * c_blk < r_blk:
                    parts.append(jnp.zeros((r_blk - (c + 1) * c_blk, d), F32))
                o_c = both[c_blk:] + _bdot(intra[rs], jnp.concatenate(parts, axis=0))
                kd = kn[rs] * jnp.exp(g_last - gl[rs])
                state = state * jnp.exp(g_last) + _bdot_tn(kd, v_new)
                on = o_c * lax.rsqrt(jnp.mean(o_c * o_c, axis=1, keepdims=True) + NORM_EPS) * nw
                rows = pl.ds(r0 + c * c_blk, c_blk)
                zc = z_ref[0, rows, j * d:(j + 1) * d]
                o_ref[0, rows, j * d:(j + 1) * d] = (on * _silu(zc)).astype(o_ref.dtype)
            s_ref[j] = state
        return xq[r_blk - SUBLANES:], xk[r_blk - SUBLANES:], xv[r_blk - SUBLANES:]

    tails = (jnp.zeros((SUBLANES, d), F32), jnp.zeros((SUBLANES, d), F32), jnp.zeros((SUBLANES, 2 * d), F32))
    lax.fori_loop(0, seq // r_blk, step, tails)


def _gdn_core(y, conv_w, a_log, dt_bias, norm_w):
    bsz, l, _ = y.shape
    nh, d = GDN_QK_HEADS, GDN_HEAD_DIM
    rep = GDN_V_HEADS // GDN_QK_HEADS
    gate_blk = (GDN_CONV_DIM + GDN_VAL_DIM) // LANES
    lane_vec = lambda t: jnp.zeros((1, LANES), F32).at[0, GDN_V_HEADS:2 * GDN_V_HEADS].set(t.astype(F32))
    gates = pl.pallas_call(
        _gdn_gates_kernel,
        out_shape=jax.ShapeDtypeStruct((bsz, l, LANES), F32),
        grid=(bsz,),
        in_specs=[pl.BlockSpec((1, l, LANES), lambda b: (b, 0, gate_blk)),
                  pl.BlockSpec((1, LANES), lambda b: (0, 0)), pl.BlockSpec((1, LANES), lambda b: (0, 0))],
        out_specs=pl.BlockSpec((1, l, LANES), lambda b: (b, 0, 0)),
        compiler_params=pltpu.CompilerParams(dimension_semantics=("arbitrary",), vmem_limit_bytes=VMEM_LIMIT_BYTES),
        name="gdn_gates",
    )(y, lane_vec(a_log), lane_vec(dt_bias))
    return pl.pallas_call(
        functools.partial(_gdn_kernel, seq=l),
        out_shape=jax.ShapeDtypeStruct((bsz, l, GDN_VAL_DIM), BF16),
        grid=(bsz, nh),
        in_specs=[pl.BlockSpec((1, l, d), lambda b, h: (b, 0, h)),
                  pl.BlockSpec((1, l, d), lambda b, h: (b, 0, nh + h)),
                  pl.BlockSpec((1, l, rep * d), lambda b, h: (b, 0, 2 * nh // rep + h)),
                  pl.BlockSpec((1, l, rep * d), lambda b, h: (b, 0, (GDN_CONV_DIM // d) // rep + h)),
                  pl.BlockSpec((1, l, LANES), lambda b, h: (b, 0, 0)),
                  pl.BlockSpec((GDN_CONV, d), lambda b, h: (0, h)),
                  pl.BlockSpec((GDN_CONV, d), lambda b, h: (0, nh + h)),
                  pl.BlockSpec((GDN_CONV, rep * d), lambda b, h: (0, 2 * nh // rep + h)),
                  pl.BlockSpec((1, d), lambda b, h: (0, 0))],
        out_specs=pl.BlockSpec((1, l, rep * d), lambda b, h: (b, 0, h)),
        scratch_shapes=[pltpu.VMEM((rep, d, d), F32)],
        compiler_params=_ARB2,
        name="gdn_core",
    )(y, y, y, y, gates, conv_w, conv_w, conv_w, norm_w.reshape(1, d))


def _gdn_mixer(u, w_in, conv_w, a_log, dt_bias, norm_w, w_out, res):
    bsz, l, dm = u.shape
    y = _proj(u.reshape(bsz * l, dm), _pad_cols(w_in, 512))
    o = _gdn_core(y.reshape(bsz, l, -1), conv_w, a_log, dt_bias, norm_w)
    return _proj(o.reshape(bsz * l, GDN_VAL_DIM), w_out, mode="residual", res=res)


def _ret_kernel(q_ref, k_ref, v_ref, gate_ref, cos_ref, sin_ref, dec_ref, qd_ref, kd_ref, cd_ref, o_ref, s_ref):
    @pl.when(pl.program_id(1) == 0)
    def _():
        s_ref[...] = jnp.zeros_like(s_ref)

    half = RET_QK_HEAD // 2

    def rope(ref, rs, h, c, s):
        x1 = ref[0, rs, h * RET_QK_HEAD:h * RET_QK_HEAD + half]
        x2 = ref[0, rs, h * RET_QK_HEAD + half:(h + 1) * RET_QK_HEAD]
        return jnp.concatenate([x1 * c - x2 * s, x2 * c + x1 * s], axis=1)

    for c in range(RET_STEP // RET_CHUNK):
        rs = slice(c * RET_CHUNK, (c + 1) * RET_CHUNK)
        cos, sin = cos_ref[0, rs, :], sin_ref[0, rs, :]
        for h in range(RET_HEADS):
            vs = slice(h * RET_V_HEAD, (h + 1) * RET_V_HEAD)
            qr = rope(q_ref, rs, h, cos, sin)
            kr = rope(k_ref, rs, h, cos, sin) * RET_QK_HEAD ** -0.5
            v = v_ref[0, rs, vs]
            qdec = jnp.concatenate([qd_ref[h], qd_ref[h]], axis=1)
            kdec = jnp.concatenate([kd_ref[h], kd_ref[h]], axis=1)
            o = _bdot(_bdot_nt(qr, kr) * dec_ref[h], v) + _bdot(qr * qdec, s_ref[h])
            s_ref[h] = s_ref[h] * cd_ref[h] + _bdot_tn(kr * kdec, v)
            on = o * lax.rsqrt(jnp.mean(o * o, axis=1, keepdims=True) + NORM_EPS)
            o_ref[0, rs, vs] = (_silu(gate_ref[0, rs, vs]) * on).astype(o_ref.dtype)


def _ret_core(y, positions):
    bsz, l, _ = y.shape
    c = RET_CHUNK
    inv_freq = ROPE_THETA ** (-jnp.arange(0, RET_QK_HEAD, 2, dtype=F32) / RET_QK_HEAD)
    ang = positions.astype(F32)[..., None] * inv_freq
    cos, sin = jnp.cos(ang), jnp.sin(ang)
    log_gamma = jnp.log1p(-jnp.exp2(-5.0 - jnp.arange(RET_HEADS, dtype=F32)))
    idx = jnp.arange(c, dtype=F32)
    rel = idx[:, None] - idx[None, :]
    inner_decay = jnp.where(rel >= 0, jnp.exp(log_gamma[:, None, None] * jnp.maximum(rel, 0.0)), 0.0)
    lanes = lambda t: jnp.broadcast_to(t[:, :, None], (RET_HEADS, c, LANES))
    q_decay = lanes(jnp.exp(log_gamma[:, None] * (idx[None, :] + 1.0)))
    k_decay = lanes(jnp.exp(log_gamma[:, None] * (c - 1.0 - idx[None, :])))
    chunk_decay = jnp.exp(log_gamma * c)
    t = RET_STEP
    full = lambda shape: pl.BlockSpec(shape, lambda b, i: (0,) * len(shape))
    return pl.pallas_call(
        _ret_kernel,
        out_shape=jax.ShapeDtypeStruct((bsz, l, RET_V_DIM), BF16),
        grid=(bsz, l // t),
        in_specs=[pl.BlockSpec((1, t, RET_QK_DIM), lambda b, i: (b, i, 0)),
                  pl.BlockSpec((1, t, RET_QK_DIM), lambda b, i: (b, i, 1)),
                  pl.BlockSpec((1, t, RET_V_DIM), lambda b, i: (b, i, 1)),
                  pl.BlockSpec((1, t, RET_V_DIM), lambda b, i: (b, i, 2)),
                  pl.BlockSpec((1, t, LANES), lambda b, i: (b, i, 0)),
                  pl.BlockSpec((1, t, LANES), lambda b, i: (b, i, 0)),
                  full((RET_HEADS, c, c)), full((RET_HEADS, c, LANES)), full((RET_HEADS, c, LANES)),
                  pl.BlockSpec(memory_space=pltpu.SMEM)],
        out_specs=pl.BlockSpec((1, t, RET_V_DIM), lambda b, i: (b, i, 0)),
        scratch_shapes=[pltpu.VMEM((RET_HEADS, RET_QK_HEAD, RET_V_HEAD), F32)],
        compiler_params=_ARB2,
        name="ret_core",
    )(y, y, y, y, cos, sin, inner_decay, q_decay, k_decay, chunk_decay)


def _retention_mixer(u, positions, w_in, w_out, res):
    bsz, l, dm = u.shape
    y = _proj(u.reshape(bsz * l, dm), w_in)
    o = _ret_core(y.reshape(bsz, l, -1), positions)
    return _proj(o.reshape(bsz * l, RET_V_DIM), w_out, mode="residual", res=res)


def _rope_tables(positions, dim):
    inv_freq = ROPE_THETA ** (-jnp.arange(0, dim, 2, dtype=F32) / dim)
    ang = positions.astype(F32)[..., None] * inv_freq
    c, s = jnp.cos(ang), jnp.sin(ang)
    reps = LANES // dim
    cos_full = jnp.tile(jnp.concatenate([c, c], axis=-1), (1, 1, reps))
    sin_signed = jnp.tile(jnp.concatenate([-s, s], axis=-1), (1, 1, reps))
    return cos_full, sin_signed


def _rope128(x, cos, sin_signed):
    return x * cos + pltpu.roll(x, 64, 1) * sin_signed


def _rope64(x, cos, sin_signed, low_half):
    partner = jnp.where(low_half, pltpu.roll(x, 96, 1), pltpu.roll(x, 32, 1))
    return x * cos + partner * sin_signed


def _dsa_prep_kernel(k_ref, v_ref, ki_ref, ck_ref, sk_ref, cik_ref, sik_ref, kb_ref, vb_ref, ki2_ref):
    ck, sk = ck_ref[0], sk_ref[0]
    for g in range(DSA_KV_HEADS):
        sl = slice(g * DSA_HEAD_DIM, (g + 1) * DSA_HEAD_DIM)
        kb_ref[0, g] = _rope128(k_ref[0, :, sl], ck, sk).astype(BF16)
        vb_ref[0, g] = v_ref[0, :, sl].astype(BF16)
    x = ki_ref[0]
    lane = _lane_iota(x.shape)
    xr = _rope64(x, cik_ref[0], sik_ref[0], (lane % IDX_HEAD_DIM) < IDX_HEAD_DIM // 2)
    ki2_ref[0] = jnp.where(lane < IDX_HEAD_DIM, xr, pltpu.roll(xr, IDX_HEAD_DIM, 1)).astype(BF16)


def _dsa_block(s_eff, q_ref, qi_ref, wi_ref, cq_ref, sq_ref, ciq_ref, siq_ref, kb_ref, vb_ref, ki2_ref, o_ref,
               key_ref, *, topk):
    t_blk = DSA_BLOCK
    qb = pl.program_id(1)
    lane = _lane_iota((t_blk, LANES))
    low_half = (lane % IDX_HEAD_DIM) < IDX_HEAD_DIM // 2
    lo64 = lane < IDX_HEAD_DIM

    wi = wi_ref[0] * (IDX_HEADS ** -0.5 * IDX_HEAD_DIM ** -0.5)
    ciq, siq = ciq_ref[0], siq_ref[0]
    ki2 = ki2_ref[0, :s_eff, :]
    score = jnp.zeros((t_blk, s_eff), F32)
    for quad in range(IDX_HEADS // 4):
        rows = []
        for pr in (2 * quad, 2 * quad + 1):
            xr = _rope64(qi_ref[0, :, pr * LANES:(pr + 1) * LANES], ciq, siq, low_half)
            rows.append(jnp.where(lo64, xr, 0.0))
            rows.append(jnp.where(lo64, 0.0, xr))
        s4 = _bdot_nt(jnp.concatenate(rows, axis=0), ki2)
        for j in range(4):
            hh = 4 * quad + j
            w_col = wi[:, IDX_HEAD_DIM + hh:IDX_HEAD_DIM + hh + 1]
            score = score + w_col * jnp.maximum(s4[j * t_blk:(j + 1) * t_blk], 0.0)

    t_idx = qb * t_blk + lax.broadcasted_iota(jnp.int32, (t_blk, 1), 0)
    s_idx = _lane_iota((t_blk, s_eff))
    score = jnp.where(s_idx <= t_idx, score, -jnp.inf)
    bits = lax.bitcast_convert_type(score, jnp.int32)
    key_ref[:, :s_eff] = bits ^ ((bits >> 31) & jnp.int32(0x7FFFFFFF))

    kk = jnp.minimum(t_idx + 1, topk).astype(F32)
    int_min = jnp.int32(-2 ** 31)

    def bisect(i, t_u):
        cand_u = t_u | lax.shift_left(jnp.int32(1), 31 - i)
        hit = key_ref[:, :s_eff] >= (cand_u ^ int_min)
        cnt = jnp.sum(jnp.where(hit, 1.0, 0.0), axis=1, keepdims=True)
        return jnp.where(cnt >= kk, cand_u, t_u)

    t_u = lax.fori_loop(0, 32, bisect, jnp.zeros((t_blk, 1), jnp.int32))
    sel = key_ref[:, :s_eff] >= (t_u ^ int_min)

    cq, sq = cq_ref[0], sq_ref[0]
    group = DSA_HEADS // DSA_KV_HEADS
    for g in range(DSA_KV_HEADS):
        qs = []
        for j in range(group):
            hh = group * g + j
            qs.append(_rope128(q_ref[0, :, hh * DSA_HEAD_DIM:(hh + 1) * DSA_HEAD_DIM], cq, sq))
        logits = _bdot_nt(jnp.concatenate(qs, axis=0), kb_ref[0, g, :s_eff, :]) * DSA_HEAD_DIM ** -0.5
        ps, dens = [], []
        for j in range(group):
            lj = jnp.where(sel, logits[j * t_blk:(j + 1) * t_blk], -jnp.inf)
            pj = jnp.exp(lj - jnp.max(lj, axis=1, keepdims=True))
            dens.append(jnp.sum(pj, axis=1, keepdims=True))
            ps.append(pj)
        pv = _bdot(jnp.concatenate(ps, axis=0), vb_ref[0, g, :s_eff, :])
        for j in range(group):
            hh = group * g + j
            o_ref[0, :, hh * DSA_HEAD_DIM:(hh + 1) * DSA_HEAD_DIM] = (
                pv[j * t_blk:(j + 1) * t_blk] / dens[j]).astype(o_ref.dtype)


def _dsa_kernel(q_ref, qi_ref, wi_ref, cq_ref, sq_ref, ciq_ref, siq_ref, kb_ref, vb_ref, ki2_ref, o_ref, key_ref,
                *, topk, seq):
    qb = pl.program_id(1)
    blocks_per_step = DSA_KV_STEP // DSA_BLOCK
    for br in range(seq // DSA_KV_STEP):
        @pl.when(qb // blocks_per_step == br)
        def _():
            _dsa_block((br + 1) * DSA_KV_STEP, q_ref, qi_ref, wi_ref, cq_ref, sq_ref, ciq_ref, siq_ref, kb_ref,
                       vb_ref, ki2_ref, o_ref, key_ref, topk=topk)


def _dsa_core(y, positions):
    bsz, l, _ = y.shape
    topk = min(DSA_TOPK_MAX, l // 4)
    cos_q, sin_q = _rope_tables(positions, DSA_HEAD_DIM)
    cos_i, sin_i = _rope_tables(positions, IDX_HEAD_DIM)
    k_blk = DSA_Q_DIM // DSA_KV_DIM
    ki_blk = (DSA_Q_DIM + 2 * DSA_KV_DIM + IDX_HEADS * IDX_HEAD_DIM) // LANES
    ps = DSA_KV_STEP
    tab = pl.BlockSpec((1, ps, LANES), lambda b, i: (b, i, 0))
    kb, vb, ki2 = pl.pallas_call(
        _dsa_prep_kernel,
        out_shape=(jax.ShapeDtypeStruct((bsz, DSA_KV_HEADS, l, DSA_HEAD_DIM), BF16),
                   jax.ShapeDtypeStruct((bsz, DSA_KV_HEADS, l, DSA_HEAD_DIM), BF16),
                   jax.ShapeDtypeStruct((bsz, l, LANES), BF16)),
        grid=(bsz, l // ps),
        in_specs=[pl.BlockSpec((1, ps, DSA_KV_DIM), lambda b, i: (b, i, k_blk)),
                  pl.BlockSpec((1, ps, DSA_KV_DIM), lambda b, i: (b, i, k_blk + 1)),
                  pl.BlockSpec((1, ps, LANES), lambda b, i: (b, i, ki_blk)),
                  tab, tab, tab, tab],
        out_specs=(pl.BlockSpec((1, DSA_KV_HEADS, ps, DSA_HEAD_DIM), lambda b, i: (b, 0, i, 0)),
                   pl.BlockSpec((1, DSA_KV_HEADS, ps, DSA_HEAD_DIM), lambda b, i: (b, 0, i, 0)),
                   pl.BlockSpec((1, ps, LANES), lambda b, i: (b, i, 0))),
        compiler_params=_ARB2,
        name="dsa_prep",
    )(y, y, y, cos_q, sin_q, cos_i, sin_i)

    t_blk = DSA_BLOCK
    qtab = pl.BlockSpec((1, t_blk, LANES), lambda b, i: (b, i, 0))
    kv_spec = pl.BlockSpec((1, DSA_KV_HEADS, l, DSA_HEAD_DIM), lambda b, i: (b, 0, 0, 0))
    return pl.pallas_call(
        functools.partial(_dsa_kernel, topk=topk, seq=l),
        out_shape=jax.ShapeDtypeStruct((bsz, l, DSA_Q_DIM), BF16),
        grid=(bsz, l // t_blk),
        in_specs=[pl.BlockSpec((1, t_blk, DSA_Q_DIM), lambda b, i: (b, i, 0)),
                  pl.BlockSpec((1, t_blk, IDX_HEADS * IDX_HEAD_DIM),
                               lambda b, i: (b, i, (DSA_Q_DIM + 2 * DSA_KV_DIM) // (IDX_HEADS * IDX_HEAD_DIM))),
                  pl.BlockSpec((1, t_blk, LANES), lambda b, i: (b, i, ki_blk)),
                  qtab, qtab, qtab, qtab,
                  kv_spec, kv_spec,
                  pl.BlockSpec((1, l, LANES), lambda b, i: (b, 0, 0))],
        out_specs=pl.BlockSpec((1, t_blk, DSA_Q_DIM), lambda b, i: (b, i, 0)),
        scratch_shapes=[pltpu.VMEM((t_blk, l), jnp.int32)],
        compiler_params=_ARB2,
        name="dsa_attn",
    )(y, y, y, cos_q, sin_q, cos_i, sin_i, kb, vb, ki2)


def _dsa_mixer(u, positions, w_in, w_out, res):
    bsz, l, dm = u.shape
    y = _proj(u.reshape(bsz * l, dm), _pad_cols(w_in, 512))
    o = _dsa_core(y.reshape(bsz, l, -1), positions)
    return _proj(o.reshape(bsz * l, DSA_Q_DIM), w_out, mode="residual", res=res)


def kernel(x, p, positions, norm_mix, norm_mlp, norm_ple, norm_final, gdn_w_in, gdn_conv_w, gdn_a_log,
           gdn_dt_bias, gdn_norm, gdn_w_out, ret_w_in, ret_w_out, dsa_w_in, dsa_w_out, mlp_w_up, mlp_w_down,
           ple_w_gate, ple_w_proj):
    bsz, l, d = x.shape
    m = bsz * l
    h = x.reshape(m, d)
    for i in range(DEPTH):
        kind, j = i % N_MIXERS, i // N_MIXERS
        u = _rmsnorm(h, norm_mix[i], BF16).reshape(bsz, l, d)
        if kind == 0:
            h = _gdn_mixer(u, gdn_w_in[j], gdn_conv_w[j], gdn_a_log[j], gdn_dt_bias[j], gdn_norm[j], gdn_w_out[j], h)
        elif kind == 1:
            h = _retention_mixer(u, positions, ret_w_in[j], ret_w_out[j], h)
        else:
            h = _dsa_mixer(u, positions, dsa_w_in[j], dsa_w_out[j], h)
        u = _rmsnorm(h, norm_mlp[i], BF16)
        a = _proj(u, mlp_w_up[i], mode="relu2", out_dtype=BF16)
        h = _proj(a, mlp_w_down[i], mode="residual", res=h)
        u = _rmsnorm(h, norm_ple[i], BF16)
        h = _proj(u, ple_w_gate[i], mode="ple", res=h, p=p[i].reshape(m, -1), wp=ple_w_proj[i])
    return _rmsnorm(h, norm_final, x.dtype).reshape(bsz, l, d)
```

```python
import functools

import jax
import jax.numpy as jnp
from jax import lax
from jax.experimental import pallas as pl
from jax.experimental.pallas import tpu as pltpu

F32 = jnp.float32
BF16 = jnp.bfloat16

D_MODEL = 2048
DEPTH = 4
N_MIXERS = 3
NORM_EPS = 1e-6
ROPE_THETA = 10000.0

GDN_HEAD_DIM = 128
GDN_QK_HEADS = D_MODEL // 128
GDN_V_HEADS = 2 * GDN_QK_HEADS
GDN_CONV = 4
GDN_CHUNK = 64
GDN_KEY_DIM = GDN_QK_HEADS * GDN_HEAD_DIM
GDN_VAL_DIM = GDN_V_HEADS * GDN_HEAD_DIM
GDN_CONV_DIM = 2 * GDN_KEY_DIM + GDN_VAL_DIM
GDN_SUPER = 256

RET_HEADS = 8
RET_QK_DIM = D_MODEL
RET_V_DIM = 2 * D_MODEL
RET_QK_HEAD = RET_QK_DIM // RET_HEADS
RET_V_HEAD = RET_V_DIM // RET_HEADS
RET_CHUNK = 128
RET_STEP = 256

DSA_HEAD_DIM = 128
DSA_HEADS = D_MODEL // DSA_HEAD_DIM
DSA_KV_HEADS = 4
IDX_HEADS = 16
IDX_HEAD_DIM = 64
DSA_TOPK_MAX = 256
DSA_BLOCK = 128
DSA_KV_STEP = 512
DSA_Q_DIM = DSA_HEADS * DSA_HEAD_DIM
DSA_KV_DIM = DSA_KV_HEADS * DSA_HEAD_DIM

LANES = 128
SUBLANES = 8
VMEM_LIMIT_BYTES = 56 * 1024 * 1024

_ARB2 = pltpu.CompilerParams(dimension_semantics=("arbitrary", "arbitrary"), vmem_limit_bytes=VMEM_LIMIT_BYTES)


def _bdot(a, b):
    return jnp.dot(a.astype(BF16), b.astype(BF16), preferred_element_type=F32)


def _bdot_nt(a, b):
    return lax.dot_general(a.astype(BF16), b.astype(BF16), (((1,), (1,)), ((), ())), preferred_element_type=F32)


def _bdot_tn(a, b):
    return lax.dot_general(a.astype(BF16), b.astype(BF16), (((0,), (0,)), ((), ())), preferred_element_type=F32)


def _lane_iota(shape):
    return lax.broadcasted_iota(jnp.int32, shape, 1)


def _silu(x):
    return x * jax.nn.sigmoid(x)


def _mm_kernel(*refs, nk, mode):
    x_ref, w_ref, o_ref = refs[0], refs[1], refs[-1]
    extra = refs[2:-1]
    if nk > 1:
        half = o_ref.shape[1] // 2
        for cs in (slice(0, half), slice(half, 2 * half)):
            prod = _bdot(x_ref[...], w_ref[:, cs])

            @pl.when(pl.program_id(2) == 0)
            def _():
                o_ref[:, cs] = extra[0][:, cs] + prod

            @pl.when(pl.program_id(2) > 0)
            def _():
                o_ref[:, cs] += prod
        return
    prod = _bdot(x_ref[...], w_ref[...])
    if mode == "plain":
        o_ref[...] = prod.astype(o_ref.dtype)
    elif mode == "relu2":
        o_ref[...] = jnp.square(jnp.maximum(prod, 0.0)).astype(o_ref.dtype)
    elif mode == "ple":
        res_ref, p_ref, wp_ref = extra
        o_ref[...] = res_ref[...] + jax.nn.sigmoid(prod) * _bdot(p_ref[...], wp_ref[...])
    else:
        o_ref[...] = extra[0][...] + prod


def _matmul(x, w, layer, *, n_out, bm, bn, bk, n_outer, mode="plain", out_dtype=F32, res=None, p=None, wp=None):
    m, kdim = x.shape
    bm = min(bm, m)
    assert m % bm == 0 and n_out % bn == 0 and kdim % bk == 0 and n_out <= w.shape[2], (x.shape, w.shape, bm, bn, bk)
    nk = kdim // bk
    assert nk == 1 or mode == "residual"
    if n_outer:
        grid = (n_out // bn, m // bm, nk)
        mi, ni = (lambda j, i, k: i), (lambda j, i, k: j)
    else:
        grid = (m // bm, n_out // bn, nk)
        mi, ni = (lambda i, j, k: i), (lambda i, j, k: j)
    in_specs = [pl.BlockSpec((bm, bk), lambda a, b, k: (mi(a, b, k), k)),
                pl.BlockSpec((None, bk, bn), lambda a, b, k: (layer, k, ni(a, b, k)))]
    args = [x, w]
    o_map = lambda a, b, k: (mi(a, b, k), ni(a, b, k))
    if mode in ("residual", "ple"):
        in_specs.append(pl.BlockSpec((bm, bn), o_map))
        args.append(res)
    if mode == "ple":
        kp = p.shape[2]
        in_specs.append(pl.BlockSpec((None, bm, kp), lambda a, b, k: (layer, mi(a, b, k), 0)))
        in_specs.append(pl.BlockSpec((None, kp, bn), lambda a, b, k: (layer, 0, ni(a, b, k))))
        args += [p, wp]
    return pl.pallas_call(
        functools.partial(_mm_kernel, nk=nk, mode=mode),
        out_shape=jax.ShapeDtypeStruct((m, n_out), out_dtype),
        grid=grid,
        in_specs=in_specs,
        out_specs=pl.BlockSpec((bm, bn), o_map),
        compiler_params=pltpu.CompilerParams(
            dimension_semantics=("arbitrary", "arbitrary", "arbitrary"), vmem_limit_bytes=VMEM_LIMIT_BYTES),
        name="matmul_" + mode,
    )(*args)


def _proj(x, w, layer, n_out=None, **kw):
    kdim = w.shape[1]
    n_out = w.shape[2] if n_out is None else n_out
    if kdim <= 2048:
        bn = 512 if kw.get("mode") == "ple" else min(n_out, 1024)
        return _matmul(x, w, layer, n_out=n_out, bm=1024, bn=bn, bk=kdim, n_outer=True, **kw)
    return _matmul(x, w, layer, n_out=n_out, bm=2048, bn=1024, bk=1024, n_outer=False, **kw)


def _tail_cols(w, layer, start):
    t = w[layer, :, start:]
    return jnp.pad(t, ((0, 0), (0, LANES - t.shape[1])))[None]


def _rmsnorm_kernel(x_ref, g_ref, o_ref):
    x = x_ref[...]
    y = x * lax.rsqrt(jnp.mean(x * x, axis=1, keepdims=True) + NORM_EPS) * g_ref[...]
    o_ref[...] = y.astype(o_ref.dtype)


def _rmsnorm(x, gain, out_dtype):
    m, d = x.shape
    bm = min(m, 512)
    return pl.pallas_call(
        _rmsnorm_kernel,
        out_shape=jax.ShapeDtypeStruct((m, d), out_dtype),
        grid=(m // bm,),
        in_specs=[pl.BlockSpec((bm, d), lambda i: (i, 0)), pl.BlockSpec((1, d), lambda i: (0, 0))],
        out_specs=pl.BlockSpec((bm, d), lambda i: (i, 0)),
        compiler_params=pltpu.CompilerParams(dimension_semantics=("arbitrary",), vmem_limit_bytes=VMEM_LIMIT_BYTES),
        name="rmsnorm",
    )(x, gain.reshape(1, d))


def _gdn_gates_kernel(gb_ref, alog_ref, dtb_ref, o_ref):
    x = gb_ref[0]
    lane = _lane_iota(x.shape)
    row = lax.broadcasted_iota(jnp.int32, x.shape, 0) % GDN_CHUNK
    g = -jnp.exp(alog_ref[...]) * jax.nn.softplus(x + dtb_ref[...])
    sh = 1
    while sh < GDN_CHUNK:
        g = g + jnp.where(row >= sh, pltpu.roll(g, sh, 0), 0.0)
        sh *= 2
    o_ref[0] = jnp.where(lane < GDN_V_HEADS, jax.nn.sigmoid(x), g)


def _conv_silu_tile(x, tail, w):
    rows8 = lax.broadcasted_iota(jnp.int32, tail.shape, 0)
    acc = x * w[GDN_CONV - 1:GDN_CONV]
    for sh in range(1, GDN_CONV):
        xs = pltpu.roll(x, sh, 0)
        head = jnp.where(rows8 < sh, pltpu.roll(tail, sh, 0), xs[:SUBLANES])
        xs = jnp.concatenate([head, xs[SUBLANES:]], axis=0)
        acc = acc + xs * w[GDN_CONV - 1 - sh:GDN_CONV - sh]
    return _silu(acc)


def _gdn_kernel(q_ref, k_ref, v_ref, z_ref, gt_ref, wq_ref, wk_ref, wv_ref, nw_ref, o_ref,
                s_ref, kn_st, wk_st, qg_st, gl_st, in_st, *, seq):
    hq = pl.program_id(1)
    r_blk, c_blk, d = GDN_SUPER, GDN_CHUNK, GDN_HEAD_DIM
    ii = lax.broadcasted_iota(jnp.int32, (r_blk, r_blk), 0)
    jj = lax.broadcasted_iota(jnp.int32, (r_blk, r_blk), 1)
    blk_masks = [(ii >> sh) == (jj >> sh) for sh in (3, 4, 5, 6)]
    blk8 = blk_masks[0]
    tri_incl = blk_masks[3] & (ii >= jj)
    diag = ii == jj
    eye = jnp.where(diag, 1.0, 0.0)
    lane = _lane_iota((r_blk, LANES))
    s_ref[...] = jnp.zeros_like(s_ref)
    wq, wk, wv, nw = wq_ref[...], wk_ref[...], wv_ref[...], nw_ref[...]
    n_steps = seq // r_blk
    rep = GDN_V_HEADS // GDN_QK_HEADS

    def fetch(s):
        r0 = pl.multiple_of(s * r_blk, r_blk)
        rt = pl.multiple_of(jnp.maximum(r0 - SUBLANES, 0), SUBLANES)
        tail = lambda ref: jnp.where(s > 0, ref[0, pl.ds(rt, SUBLANES), :], 0.0)
        rows = pl.ds(r0, r_blk)
        return (q_ref[0, rows, :], tail(q_ref), k_ref[0, rows, :], tail(k_ref), v_ref[0, rows, :], tail(v_ref),
                gt_ref[0, rows, :])

    heads = range(rep)

    def solve(raw, slot):
        xq, tq, xk, tk, xv, tv, gt = raw
        qc, kc, vc = _conv_silu_tile(xq, tq, wq), _conv_silu_tile(xk, tk, wk), _conv_silu_tile(xv, tv, wv)
        qn = qc * lax.rsqrt(jnp.sum(qc * qc, axis=1, keepdims=True) + NORM_EPS) * d ** -0.5
        kn = kc * lax.rsqrt(jnp.sum(kc * kc, axis=1, keepdims=True) + NORM_EPS)
        kn_st[slot] = kn
        beta = [jnp.sum(jnp.where(lane == rep * hq + j, gt, 0.0), axis=1, keepdims=True) for j in heads]
        g = [jnp.sum(jnp.where(lane == GDN_V_HEADS + rep * hq + j, gt, 0.0), axis=1, keepdims=True) for j in heads]
        gi = [jnp.broadcast_to(g[j], (r_blk, r_blk)) for j in heads]
        dec = [jnp.exp(jnp.where(tri_incl, gi[j] - jnp.sum(jnp.where(diag, gi[j], 0.0), axis=0, keepdims=True),
                                 -jnp.inf)) for j in heads]
        kb = [kn * beta[j] for j in heads]
        for j in heads:
            qg_st[slot, j] = qn * jnp.exp(g[j])
            gl_st[slot, j] = gi[j][:, :d]
        qk = _bdot_nt(qn, kn)
        kk = [_bdot_nt(kb[j], kn) for j in heads]
        yield
        for j in heads:
            in_st[slot, j] = qk * dec[j]
        a = [jnp.where(diag, 0.0, kk[j] * dec[j]) for j in heads]
        a8 = [jnp.where(blk8, a[j], 0.0) for j in heads]
        a8_2 = [_bdot(a8[j], a8[j]) for j in heads]
        yield
        inv = [eye - a8[j] for j in heads]
        t = [_bdot(inv[j], a8_2[j]) for j in heads]
        a8_4 = [_bdot(a8_2[j], a8_2[j]) for j in heads]
        yield
        inv = [inv[j] + t[j] for j in heads]
        t = [_bdot(inv[j], a8_4[j]) for j in heads]
        yield
        inv = [inv[j] + t[j] for j in heads]
        for lvl in range(3):
            off = [jnp.where(blk_masks[lvl + 1] & ~blk_masks[lvl], a[j], 0.0) for j in heads]
            t = [_bdot(off[j], inv[j]) for j in heads]
            yield
            t = [_bdot(inv[j], t[j]) for j in heads]
            yield
            inv = [inv[j] - t[j] for j in heads]
        rhs = [jnp.concatenate([vc[:, j * d:(j + 1) * d] * beta[j], kb[j] * jnp.exp(g[j])], axis=1) for j in heads]
        x = [_bdot(inv[j], rhs[j]) for j in heads]
        yield
        for j in heads:
            wk_st[slot, j] = x[j]

    def recur(s, slot):
        r0 = pl.multiple_of(s * r_blk, r_blk)
        state = [s_ref[j] for j in heads]
        for c in range(r_blk // c_blk):
            rs = slice(c * c_blk, (c + 1) * c_blk)
            gl = [gl_st[slot, j, rs, :] for j in heads]
            g_last = [gl[j][c_blk - 1:c_blk, :] for j in heads]
            both = [_bdot(jnp.concatenate([wk_st[slot, j, rs, d:], qg_st[slot, j, rs, :]], axis=0), state[j])
                    for j in heads]
            yield
            v_new = [wk_st[slot, j, rs, :d] - both[j][:c_blk] for j in heads]
            kd = [kn_st[slot, rs, :] * jnp.exp(g_last[j] - gl[j]) for j in heads]
            upd = [_bdot_tn(kd[j], v_new[j]) for j in heads]
            pad = []
            for j in heads:
                parts = []
                if c > 0:
                    parts.append(jnp.zeros((c * c_blk, d), F32))
                parts.append(v_new[j])
                if (c + 1) * c_blk < r_blk:
                    parts.append(jnp.zeros((r_blk - (c + 1) * c_blk, d), F32))
                pad.append(jnp.concatenate(parts, axis=0))
            o_in = [_bdot(in_st[slot, j, rs, :], pad[j]) for j in heads]
            yield
            state = [state[j] * jnp.exp(g_last[j]) + upd[j] for j in heads]
            rows = pl.ds(r0 + c * c_blk, c_blk)
            for j in heads:
                o_c = both[j][c_blk:] + o_in[j]
                on = o_c * lax.rsqrt(jnp.mean(o_c * o_c, axis=1, keepdims=True) + NORM_EPS) * nw
                zc = z_ref[0, rows, j * d:(j + 1) * d]
                o_ref[0, rows, j * d:(j + 1) * d] = (on * _silu(zc)).astype(o_ref.dtype)
        for j in heads:
            s_ref[j] = state[j]

    def run(*gens):
        live = list(gens)
        while live:
            live = [gen for gen in live if next(gen, StopIteration) is not StopIteration]

    run(solve(fetch(0), 0))

    def body(i, carry):
        s = 2 * i
        run(solve(fetch(s + 1), 1), recur(s, 0))
        run(solve(fetch(jnp.minimum(s + 2, n_steps - 1)), 0), recur(s + 1, 1))
        return carry

    lax.fori_loop(0, n_steps // 2, body, 0)


def _gdn_core(y, yg, conv_w, layer, a_log, dt_bias, norm_w):
    bsz, l, _ = y.shape
    assert l % (2 * GDN_SUPER) == 0
    nh, d = GDN_QK_HEADS, GDN_HEAD_DIM
    rep = GDN_V_HEADS // GDN_QK_HEADS
    lane_vec = lambda t: jnp.zeros((1, LANES), F32).at[0, GDN_V_HEADS:2 * GDN_V_HEADS].set(t.astype(F32))
    gates = pl.pallas_call(
        _gdn_gates_kernel,
        out_shape=jax.ShapeDtypeStruct((bsz, l, LANES), F32),
        grid=(bsz,),
        in_specs=[pl.BlockSpec((1, l, LANES), lambda b: (b, 0, 0)),
                  pl.BlockSpec((1, LANES), lambda b: (0, 0)), pl.BlockSpec((1, LANES), lambda b: (0, 0))],
        out_specs=pl.BlockSpec((1, l, LANES), lambda b: (b, 0, 0)),
        compiler_params=pltpu.CompilerParams(dimension_semantics=("arbitrary",), vmem_limit_bytes=VMEM_LIMIT_BYTES),
        name="gdn_gates",
    )(yg, lane_vec(a_log), lane_vec(dt_bias))
    r_blk = GDN_SUPER
    return pl.pallas_call(
        functools.partial(_gdn_kernel, seq=l),
        out_shape=jax.ShapeDtypeStruct((bsz, l, GDN_VAL_DIM), BF16),
        grid=(bsz, nh),
        in_specs=[pl.BlockSpec((1, l, d), lambda b, h: (b, 0, h)),
                  pl.BlockSpec((1, l, d), lambda b, h: (b, 0, nh + h)),
                  pl.BlockSpec((1, l, rep * d), lambda b, h: (b, 0, 2 * nh // rep + h)),
                  pl.BlockSpec((1, l, rep * d), lambda b, h: (b, 0, (GDN_CONV_DIM // d) // rep + h)),
                  pl.BlockSpec((1, l, LANES), lambda b, h: (b, 0, 0)),
                  pl.BlockSpec((None, GDN_CONV, d), lambda b, h: (layer, 0, h)),
                  pl.BlockSpec((None, GDN_CONV, d), lambda b, h: (layer, 0, nh + h)),
                  pl.BlockSpec((None, GDN_CONV, rep * d), lambda b, h: (layer, 0, 2 * nh // rep + h)),
                  pl.BlockSpec((1, d), lambda b, h: (0, 0))],
        out_specs=pl.BlockSpec((1, l, rep * d), lambda b, h: (b, 0, h)),
        scratch_shapes=[pltpu.VMEM((rep, d, d), F32),
                        pltpu.VMEM((2, r_blk, d), F32),
                        pltpu.VMEM((2, rep, r_blk, 2 * d), F32),
                        pltpu.VMEM((2, rep, r_blk, d), F32),
                        pltpu.VMEM((2, rep, r_blk, d), F32),
                        pltpu.VMEM((2, rep, r_blk, r_blk), F32)],
        compiler_params=_ARB2,
        name="gdn_core",
    )(y, y, y, y, gates, conv_w, conv_w, conv_w, norm_w.reshape(1, d))


def _gdn_mixer(u, w_in, conv_w, layer, a_log, dt_bias, norm_w, w_out, res):
    bsz, l, dm = u.shape
    u2 = u.reshape(bsz * l, dm)
    n_main = GDN_CONV_DIM + GDN_VAL_DIM
    y = _proj(u2, w_in, layer, n_out=n_main)
    yg = _proj(u2, _tail_cols(w_in, layer, n_main), 0)
    o = _gdn_core(y.reshape(bsz, l, -1), yg.reshape(bsz, l, -1), conv_w, layer, a_log, dt_bias, norm_w)
    return _proj(o.reshape(bsz * l, GDN_VAL_DIM), w_out, layer, mode="residual", res=res)


def _ret_kernel(q_ref, k_ref, v_ref, gate_ref, cos_ref, sin_ref, dec_ref, qd_ref, kd_ref, cd_ref, o_ref, s_ref):
    @pl.when(pl.program_id(1) == 0)
    def _():
        s_ref[...] = jnp.zeros_like(s_ref)

    half = RET_QK_HEAD // 2

    def rope(ref, rs, h, c, s):
        x1 = ref[0, rs, h * RET_QK_HEAD:h * RET_QK_HEAD + half]
        x2 = ref[0, rs, h * RET_QK_HEAD + half:(h + 1) * RET_QK_HEAD]
        return jnp.concatenate([x1 * c - x2 * s, x2 * c + x1 * s], axis=1)

    for c in range(RET_STEP // RET_CHUNK):
        rs = slice(c * RET_CHUNK, (c + 1) * RET_CHUNK)
        cos, sin = cos_ref[0, rs, :], sin_ref[0, rs, :]
        for h in range(RET_HEADS):
            vs = slice(h * RET_V_HEAD, (h + 1) * RET_V_HEAD)
            qr = rope(q_ref, rs, h, cos, sin)
            kr = rope(k_ref, rs, h, cos, sin) * RET_QK_HEAD ** -0.5
            v = v_ref[0, rs, vs]
            qdec = jnp.concatenate([qd_ref[h], qd_ref[h]], axis=1)
            kdec = jnp.concatenate([kd_ref[h], kd_ref[h]], axis=1)
            o = _bdot(_bdot_nt(qr, kr) * dec_ref[h], v) + _bdot(qr * qdec, s_ref[h])
            s_ref[h] = s_ref[h] * cd_ref[h] + _bdot_tn(kr * kdec, v)
            on = o * lax.rsqrt(jnp.mean(o * o, axis=1, keepdims=True) + NORM_EPS)
            o_ref[0, rs, vs] = (_silu(gate_ref[0, rs, vs]) * on).astype(o_ref.dtype)


def _ret_core(y, positions):
    bsz, l, _ = y.shape
    c = RET_CHUNK
    inv_freq = ROPE_THETA ** (-jnp.arange(0, RET_QK_HEAD, 2, dtype=F32) / RET_QK_HEAD)
    ang = positions.astype(F32)[..., None] * inv_freq
    cos, sin = jnp.cos(ang), jnp.sin(ang)
    log_gamma = jnp.log1p(-jnp.exp2(-5.0 - jnp.arange(RET_HEADS, dtype=F32)))
    idx = jnp.arange(c, dtype=F32)
    rel = idx[:, None] - idx[None, :]
    inner_decay = jnp.where(rel >= 0, jnp.exp(log_gamma[:, None, None] * jnp.maximum(rel, 0.0)), 0.0)
    lanes = lambda t: jnp.broadcast_to(t[:, :, None], (RET_HEADS, c, LANES))
    q_decay = lanes(jnp.exp(log_gamma[:, None] * (idx[None, :] + 1.0)))
    k_decay = lanes(jnp.exp(log_gamma[:, None] * (c - 1.0 - idx[None, :])))
    chunk_decay = jnp.exp(log_gamma * c)
    t = RET_STEP
    full = lambda shape: pl.BlockSpec(shape, lambda b, i: (0,) * len(shape))
    return pl.pallas_call(
        _ret_kernel,
        out_shape=jax.ShapeDtypeStruct((bsz, l, RET_V_DIM), BF16),
        grid=(bsz, l // t),
        in_specs=[pl.BlockSpec((1, t, RET_QK_DIM), lambda b, i: (b, i, 0)),
                  pl.BlockSpec((1, t, RET_QK_DIM), lambda b, i: (b, i, 1)),
                  pl.BlockSpec((1, t, RET_V_DIM), lambda b, i: (b, i, 1)),
                  pl.BlockSpec((1, t, RET_V_DIM), lambda b, i: (b, i, 2)),
                  pl.BlockSpec((1, t, LANES), lambda b, i: (b, i, 0)),
                  pl.BlockSpec((1, t, LANES), lambda b, i: (b, i, 0)),
                  full((RET_HEADS, c, c)), full((RET_HEADS, c, LANES)), full((RET_HEADS, c, LANES)),
                  pl.BlockSpec(memory_space=pltpu.SMEM)],
        out_specs=pl.BlockSpec((1, t, RET_V_DIM), lambda b, i: (b, i, 0)),
        scratch_shapes=[pltpu.VMEM((RET_HEADS, RET_QK_HEAD, RET_V_HEAD), F32)],
        compiler_params=_ARB2,
        name="ret_core",
    )(y, y, y, y, cos, sin, inner_decay, q_decay, k_decay, chunk_decay)


def _retention_mixer(u, positions, w_in, w_out, layer, res):
    bsz, l, dm = u.shape
    y = _proj(u.reshape(bsz * l, dm), w_in, layer)
    o = _ret_core(y.reshape(bsz, l, -1), positions)
    return _proj(o.reshape(bsz * l, RET_V_DIM), w_out, layer, mode="residual", res=res)


def _rope_tables(positions, dim):
    inv_freq = ROPE_THETA ** (-jnp.arange(0, dim, 2, dtype=F32) / dim)
    ang = positions.astype(F32)[..., None] * inv_freq
    c, s = jnp.cos(ang), jnp.sin(ang)
    reps = LANES // dim
    cos_full = jnp.tile(jnp.concatenate([c, c], axis=-1), (1, 1, reps))
    sin_signed = jnp.tile(jnp.concatenate([-s, s], axis=-1), (1, 1, reps))
    return cos_full, sin_signed


def _rope128(x, cos, sin_signed):
    return x * cos + pltpu.roll(x, 64, 1) * sin_signed


def _rope64(x, cos, sin_signed, low_half):
    partner = jnp.where(low_half, pltpu.roll(x, 96, 1), pltpu.roll(x, 32, 1))
    return x * cos + partner * sin_signed


def _dsa_prep_kernel(k_ref, v_ref, ki_ref, ck_ref, sk_ref, cik_ref, sik_ref, kb_ref, vb_ref, ki2_ref):
    ck, sk = ck_ref[0], sk_ref[0]
    for g in range(DSA_KV_HEADS):
        sl = slice(g * DSA_HEAD_DIM, (g + 1) * DSA_HEAD_DIM)
        kb_ref[0, g] = _rope128(k_ref[0, :, sl], ck, sk).astype(BF16)
        vb_ref[0, g] = v_ref[0, :, sl].astype(BF16)
    x = ki_ref[0]
    lane = _lane_iota(x.shape)
    xr = _rope64(x, cik_ref[0], sik_ref[0], (lane % IDX_HEAD_DIM) < IDX_HEAD_DIM // 2)
    ki2_ref[0] = jnp.where(lane < IDX_HEAD_DIM, xr, pltpu.roll(xr, IDX_HEAD_DIM, 1)).astype(BF16)


def _dsa_block(s_eff, q_ref, qi_ref, wi_ref, cq_ref, sq_ref, ciq_ref, siq_ref, kb_ref, vb_ref, ki2_ref, o_ref,
               key_ref, *, topk):
    t_blk = DSA_BLOCK
    qb = pl.program_id(1)
    lane = _lane_iota((t_blk, LANES))
    low_half = (lane % IDX_HEAD_DIM) < IDX_HEAD_DIM // 2
    lo64 = lane < IDX_HEAD_DIM

    wi = wi_ref[0] * (IDX_HEADS ** -0.5 * IDX_HEAD_DIM ** -0.5)
    ciq, siq = ciq_ref[0], siq_ref[0]
    ki2 = ki2_ref[0, :s_eff, :]
    score = jnp.zeros((t_blk, s_eff), F32)
    for quad in range(IDX_HEADS // 4):
        rows = []
        for pr in (2 * quad, 2 * quad + 1):
            xr = _rope64(qi_ref[0, :, pr * LANES:(pr + 1) * LANES], ciq, siq, low_half)
            rows.append(jnp.where(lo64, xr, 0.0))
            rows.append(jnp.where(lo64, 0.0, xr))
        s4 = _bdot_nt(jnp.concatenate(rows, axis=0), ki2)
        for j in range(4):
            hh = 4 * quad + j
            w_col = wi[:, IDX_HEAD_DIM + hh:IDX_HEAD_DIM + hh + 1]
            score = score + w_col * jnp.maximum(s4[j * t_blk:(j + 1) * t_blk], 0.0)

    t_idx = qb * t_blk + lax.broadcasted_iota(jnp.int32, (t_blk, 1), 0)
    s_idx = _lane_iota((t_blk, s_eff))
    score = jnp.where(s_idx <= t_idx, score, -jnp.inf)
    bits = lax.bitcast_convert_type(score, jnp.int32)
    key_ref[:, :s_eff] = bits ^ ((bits >> 31) & jnp.int32(0x7FFFFFFF))

    kk = jnp.minimum(t_idx + 1, topk).astype(F32)
    int_min = jnp.int32(-2 ** 31)

    def bisect(i, t_u):
        cand_u = t_u | lax.shift_left(jnp.int32(1), 31 - i)
        hit = key_ref[:, :s_eff] >= (cand_u ^ int_min)
        cnt = jnp.sum(jnp.where(hit, 1.0, 0.0), axis=1, keepdims=True)
        return jnp.where(cnt >= kk, cand_u, t_u)

    t_u = lax.fori_loop(0, 32, bisect, jnp.zeros((t_blk, 1), jnp.int32))
    sel = key_ref[:, :s_eff] >= (t_u ^ int_min)

    cq, sq = cq_ref[0], sq_ref[0]
    group = DSA_HEADS // DSA_KV_HEADS
    def group_logits(g):
        qs = []
        for j in range(group):
            hh = group * g + j
            qs.append(_rope128(q_ref[0, :, hh * DSA_HEAD_DIM:(hh + 1) * DSA_HEAD_DIM], cq, sq))
        return _bdot_nt(jnp.concatenate(qs, axis=0), kb_ref[0, g, :s_eff, :]) * DSA_HEAD_DIM ** -0.5

    next_logits = group_logits(0)
    for g in range(DSA_KV_HEADS):
        logits = next_logits
        if g + 1 < DSA_KV_HEADS:
            next_logits = group_logits(g + 1)
        ps, dens = [], []
        for j in range(group):
            lj = jnp.where(sel, logits[j * t_blk:(j + 1) * t_blk], -jnp.inf)
            pj = jnp.exp(lj - jnp.max(lj, axis=1, keepdims=True))
            dens.append(jnp.sum(pj, axis=1, keepdims=True))
            ps.append(pj)
        pv = _bdot(jnp.concatenate(ps, axis=0), vb_ref[0, g, :s_eff, :])
        for j in range(group):
            hh = group * g + j
            o_ref[0, :, hh * DSA_HEAD_DIM:(hh + 1) * DSA_HEAD_DIM] = (
                pv[j * t_blk:(j + 1) * t_blk] / dens[j]).astype(o_ref.dtype)


def _dsa_kernel(q_ref, qi_ref, wi_ref, cq_ref, sq_ref, ciq_ref, siq_ref, kb_ref, vb_ref, ki2_ref, o_ref, key_ref,
                *, topk, seq):
    qb = pl.program_id(1)
    blocks_per_step = DSA_KV_STEP // DSA_BLOCK
    for br in range(seq // DSA_KV_STEP):
        @pl.when(qb // blocks_per_step == br)
        def _():
            _dsa_block((br + 1) * DSA_KV_STEP, q_ref, qi_ref, wi_ref, cq_ref, sq_ref, ciq_ref, siq_ref, kb_ref,
                       vb_ref, ki2_ref, o_ref, key_ref, topk=topk)


def _dsa_core(y, yx, positions):
    bsz, l, _ = y.shape
    topk = min(DSA_TOPK_MAX, l // 4)
    cos_q, sin_q = _rope_tables(positions, DSA_HEAD_DIM)
    cos_i, sin_i = _rope_tables(positions, IDX_HEAD_DIM)
    k_blk = DSA_Q_DIM // DSA_KV_DIM
    ki_blk = 0
    ps = DSA_KV_STEP
    tab = pl.BlockSpec((1, ps, LANES), lambda b, i: (b, i, 0))
    kb, vb, ki2 = pl.pallas_call(
        _dsa_prep_kernel,
        out_shape=(jax.ShapeDtypeStruct((bsz, DSA_KV_HEADS, l, DSA_HEAD_DIM), BF16),
                   jax.ShapeDtypeStruct((bsz, DSA_KV_HEADS, l, DSA_HEAD_DIM), BF16),
                   jax.ShapeDtypeStruct((bsz, l, LANES), BF16)),
        grid=(bsz, l // ps),
        in_specs=[pl.BlockSpec((1, ps, DSA_KV_DIM), lambda b, i: (b, i, k_blk)),
                  pl.BlockSpec((1, ps, DSA_KV_DIM), lambda b, i: (b, i, k_blk + 1)),
                  pl.BlockSpec((1, ps, LANES), lambda b, i: (b, i, ki_blk)),
                  tab, tab, tab, tab],
        out_specs=(pl.BlockSpec((1, DSA_KV_HEADS, ps, DSA_HEAD_DIM), lambda b, i: (b, 0, i, 0)),
                   pl.BlockSpec((1, DSA_KV_HEADS, ps, DSA_HEAD_DIM), lambda b, i: (b, 0, i, 0)),
                   pl.BlockSpec((1, ps, LANES), lambda b, i: (b, i, 0))),
        compiler_params=_ARB2,
        name="dsa_prep",
    )(y, y, yx, cos_q, sin_q, cos_i, sin_i)

    t_blk = DSA_BLOCK
    qtab = pl.BlockSpec((1, t_blk, LANES), lambda b, i: (b, i, 0))
    kv_spec = pl.BlockSpec((1, DSA_KV_HEADS, l, DSA_HEAD_DIM), lambda b, i: (b, 0, 0, 0))
    return pl.pallas_call(
        functools.partial(_dsa_kernel, topk=topk, seq=l),
        out_shape=jax.ShapeDtypeStruct((bsz, l, DSA_Q_DIM), BF16),
        grid=(bsz, l // t_blk),
        in_specs=[pl.BlockSpec((1, t_blk, DSA_Q_DIM), lambda b, i: (b, i, 0)),
                  pl.BlockSpec((1, t_blk, IDX_HEADS * IDX_HEAD_DIM),
                               lambda b, i: (b, i, (DSA_Q_DIM + 2 * DSA_KV_DIM) // (IDX_HEADS * IDX_HEAD_DIM))),
                  pl.BlockSpec((1, t_blk, LANES), lambda b, i: (b, i, ki_blk)),
                  qtab, qtab, qtab, qtab,
                  kv_spec, kv_spec,
                  pl.BlockSpec((1, l, LANES), lambda b, i: (b, 0, 0))],
        out_specs=pl.BlockSpec((1, t_blk, DSA_Q_DIM), lambda b, i: (b, i, 0)),
        scratch_shapes=[pltpu.VMEM((t_blk, l), jnp.int32)],
        compiler_params=_ARB2,
        name="dsa_attn",
    )(y, y, yx, cos_q, sin_q, cos_i, sin_i, kb, vb, ki2)


def _dsa_mixer(u, positions, w_in, w_out, layer, res):
    bsz, l, dm = u.shape
    u2 = u.reshape(bsz * l, dm)
    n_main = DSA_Q_DIM + 2 * DSA_KV_DIM + IDX_HEADS * IDX_HEAD_DIM
    y = _proj(u2, w_in, layer, n_out=n_main)
    yx = _proj(u2, _tail_cols(w_in, layer, n_main), 0)
    o = _dsa_core(y.reshape(bsz, l, -1), yx.reshape(bsz, l, -1), positions)
    return _proj(o.reshape(bsz * l, DSA_Q_DIM), w_out, layer, mode="residual", res=res)


def kernel(x, p, positions, norm_mix, norm_mlp, norm_ple, norm_final, gdn_w_in, gdn_conv_w, gdn_a_log,
           gdn_dt_bias, gdn_norm, gdn_w_out, ret_w_in, ret_w_out, dsa_w_in, dsa_w_out, mlp_w_up, mlp_w_down,
           ple_w_gate, ple_w_proj):
    bsz, l, d = x.shape
    m = bsz * l
    h = x.reshape(m, d)
    for i in range(DEPTH):
        kind, j = i % N_MIXERS, i // N_MIXERS
        u = _rmsnorm(h, norm_mix[i], BF16).reshape(bsz, l, d)
        if kind == 0:
            h = _gdn_mixer(u, gdn_w_in, gdn_conv_w, j, gdn_a_log[j], gdn_dt_bias[j], gdn_norm[j], gdn_w_out, h)
        elif kind == 1:
            h = _retention_mixer(u, positions, ret_w_in, ret_w_out, j, h)
        else:
            h = _dsa_mixer(u, positions, dsa_w_in, dsa_w_out, j, h)
        u = _rmsnorm(h, norm_mlp[i], BF16)
        a = _proj(u, mlp_w_up, i, mode="relu2", out_dtype=BF16)
        h = _proj(a, mlp_w_down, i, mode="residual", res=h)
        u = _rmsnorm(h, norm_ple[i], BF16)
        h = _proj(u, ple_w_gate, i, mode="ple", res=h, p=p.reshape(DEPTH, m, -1), wp=ple_w_proj)
    return _rmsnorm(h, norm_final, x.dtype).reshape(bsz, l, d)
```

```python
import functools

import jax
import jax.numpy as jnp
from jax import lax
from jax.experimental import pallas as pl
from jax.experimental.pallas import tpu as pltpu

F32 = jnp.float32
BF16 = jnp.bfloat16

D_MODEL = 2048
DEPTH = 4
N_MIXERS = 3
NORM_EPS = 1e-6
ROPE_THETA = 10000.0

GDN_HEAD_DIM = 128
GDN_QK_HEADS = D_MODEL // 128
GDN_V_HEADS = 2 * GDN_QK_HEADS
GDN_CONV = 4
GDN_CHUNK = 64
GDN_KEY_DIM = GDN_QK_HEADS * GDN_HEAD_DIM
GDN_VAL_DIM = GDN_V_HEADS * GDN_HEAD_DIM
GDN_CONV_DIM = 2 * GDN_KEY_DIM + GDN_VAL_DIM
GDN_SUPER = 256
GDN_STEP_QK = 2

RET_HEADS = 8
RET_QK_DIM = D_MODEL
RET_V_DIM = 2 * D_MODEL
RET_QK_HEAD = RET_QK_DIM // RET_HEADS
RET_V_HEAD = RET_V_DIM // RET_HEADS
RET_CHUNK = 128
RET_STEP = 256

DSA_HEAD_DIM = 128
DSA_HEADS = D_MODEL // DSA_HEAD_DIM
DSA_KV_HEADS = 4
IDX_HEADS = 16
IDX_HEAD_DIM = 64
DSA_TOPK_MAX = 256
DSA_BLOCK = 128
DSA_KV_STEP = 512
DSA_Q_DIM = DSA_HEADS * DSA_HEAD_DIM
DSA_KV_DIM = DSA_KV_HEADS * DSA_HEAD_DIM

LANES = 128
SUBLANES = 8
VMEM_LIMIT_BYTES = 56 * 1024 * 1024

_ARB2 = pltpu.CompilerParams(dimension_semantics=("arbitrary", "arbitrary"), vmem_limit_bytes=VMEM_LIMIT_BYTES)


def _bdot(a, b):
    return jnp.dot(a.astype(BF16), b.astype(BF16), preferred_element_type=F32)


def _bdot_nt(a, b):
    return lax.dot_general(a.astype(BF16), b.astype(BF16), (((1,), (1,)), ((), ())), preferred_element_type=F32)


def _bdot_tn(a, b):
    return lax.dot_general(a.astype(BF16), b.astype(BF16), (((0,), (0,)), ((), ())), preferred_element_type=F32)


def _lane_iota(shape):
    return lax.broadcasted_iota(jnp.int32, shape, 1)


def _silu(x):
    return x * jax.nn.sigmoid(x)


def _mm_kernel(*refs, nk, mode):
    x_ref, w_ref, o_ref = refs[0], refs[1], refs[-1]
    extra = refs[2:-1]
    if nk > 1:
        half = o_ref.shape[1] // 2
        for cs in (slice(0, half), slice(half, 2 * half)):
            prod = _bdot(x_ref[...], w_ref[:, cs])

            @pl.when(pl.program_id(2) == 0)
            def _():
                o_ref[:, cs] = extra[0][:, cs] + prod

            @pl.when(pl.program_id(2) > 0)
            def _():
                o_ref[:, cs] += prod
        return
    prod = _bdot(x_ref[...], w_ref[...])
    if mode == "plain":
        o_ref[...] = prod.astype(o_ref.dtype)
    elif mode == "relu2":
        o_ref[...] = jnp.square(jnp.maximum(prod, 0.0)).astype(o_ref.dtype)
    elif mode == "ple":
        res_ref, p_ref, wp_ref = extra
        o_ref[...] = res_ref[...] + jax.nn.sigmoid(prod) * _bdot(p_ref[...], wp_ref[...])
    else:
        o_ref[...] = extra[0][...] + prod


def _matmul(x, w, layer, *, n_out, bm, bn, bk, n_outer, mode="plain", out_dtype=F32, res=None, p=None, wp=None):
    m, kdim = x.shape
    bm = min(bm, m)
    assert m % bm == 0 and n_out % bn == 0 and kdim % bk == 0 and n_out <= w.shape[2], (x.shape, w.shape, bm, bn, bk)
    nk = kdim // bk
    assert nk == 1 or mode == "residual"
    if n_outer:
        grid = (n_out // bn, m // bm, nk)
        mi, ni = (lambda j, i, k: i), (lambda j, i, k: j)
    else:
        grid = (m // bm, n_out // bn, nk)
        mi, ni = (lambda i, j, k: i), (lambda i, j, k: j)
    in_specs = [pl.BlockSpec((bm, bk), lambda a, b, k: (mi(a, b, k), k)),
                pl.BlockSpec((None, bk, bn), lambda a, b, k: (layer, k, ni(a, b, k)))]
    args = [x, w]
    o_map = lambda a, b, k: (mi(a, b, k), ni(a, b, k))
    if mode in ("residual", "ple"):
        in_specs.append(pl.BlockSpec((bm, bn), o_map))
        args.append(res)
    if mode == "ple":
        kp = p.shape[2]
        in_specs.append(pl.BlockSpec((None, bm, kp), lambda a, b, k: (layer, mi(a, b, k), 0)))
        in_specs.append(pl.BlockSpec((None, kp, bn), lambda a, b, k: (layer, 0, ni(a, b, k))))
        args += [p, wp]
    return pl.pallas_call(
        functools.partial(_mm_kernel, nk=nk, mode=mode),
        out_shape=jax.ShapeDtypeStruct((m, n_out), out_dtype),
        grid=grid,
        in_specs=in_specs,
        out_specs=pl.BlockSpec((bm, bn), o_map),
        compiler_params=pltpu.CompilerParams(
            dimension_semantics=("arbitrary", "arbitrary", "arbitrary"), vmem_limit_bytes=VMEM_LIMIT_BYTES),
        name="matmul_" + mode,
    )(*args)


def _proj(x, w, layer, n_out=None, **kw):
    kdim = w.shape[1]
    n_out = w.shape[2] if n_out is None else n_out
    if kdim <= 2048:
        bn = 512 if kw.get("mode") == "ple" else min(n_out, 1024)
        return _matmul(x, w, layer, n_out=n_out, bm=1024, bn=bn, bk=kdim, n_outer=True, **kw)
    return _matmul(x, w, layer, n_out=n_out, bm=2048, bn=1024, bk=1024, n_outer=False, **kw)


def _tail_cols(w, layer, start):
    t = w[layer, :, start:]
    return jnp.pad(t, ((0, 0), (0, LANES - t.shape[1])))[None]


def _rmsnorm_kernel(x_ref, g_ref, o_ref):
    x = x_ref[...]
    y = x * lax.rsqrt(jnp.mean(x * x, axis=1, keepdims=True) + NORM_EPS) * g_ref[...]
    o_ref[...] = y.astype(o_ref.dtype)


def _rmsnorm(x, gain, out_dtype):
    m, d = x.shape
    bm = min(m, 512)
    return pl.pallas_call(
        _rmsnorm_kernel,
        out_shape=jax.ShapeDtypeStruct((m, d), out_dtype),
        grid=(m // bm,),
        in_specs=[pl.BlockSpec((bm, d), lambda i: (i, 0)), pl.BlockSpec((1, d), lambda i: (0, 0))],
        out_specs=pl.BlockSpec((bm, d), lambda i: (i, 0)),
        compiler_params=pltpu.CompilerParams(dimension_semantics=("arbitrary",), vmem_limit_bytes=VMEM_LIMIT_BYTES),
        name="rmsnorm",
    )(x, gain.reshape(1, d))


def _gdn_gates_kernel(gb_ref, alog_ref, dtb_ref, o_ref):
    x = gb_ref[0]
    lane = _lane_iota(x.shape)
    row = lax.broadcasted_iota(jnp.int32, x.shape, 0) % GDN_CHUNK
    g = -jnp.exp(alog_ref[...]) * jax.nn.softplus(x + dtb_ref[...])
    sh = 1
    while sh < GDN_CHUNK:
        g = g + jnp.where(row >= sh, pltpu.roll(g, sh, 0), 0.0)
        sh *= 2
    o_ref[0] = jnp.where(lane < GDN_V_HEADS, jax.nn.sigmoid(x), g)


def _conv_silu_tile(x, tail, w):
    rows8 = lax.broadcasted_iota(jnp.int32, tail.shape, 0)
    acc = x * w[GDN_CONV - 1:GDN_CONV]
    for sh in range(1, GDN_CONV):
        xs = pltpu.roll(x, sh, 0)
        head = jnp.where(rows8 < sh, pltpu.roll(tail, sh, 0), xs[:SUBLANES])
        xs = jnp.concatenate([head, xs[SUBLANES:]], axis=0)
        acc = acc + xs * w[GDN_CONV - 1 - sh:GDN_CONV - sh]
    return _silu(acc)


def _gdn_kernel(q_ref, k_ref, v_ref, z_ref, gt_ref, wq_ref, wk_ref, wv_ref, nw_ref, o_ref,
                s_ref, kn_st, wk_st, qg_st, gl_st, in_st, *, seq):
    hq = pl.program_id(1)
    r_blk, c_blk, d = GDN_SUPER, GDN_CHUNK, GDN_HEAD_DIM
    ii = lax.broadcasted_iota(jnp.int32, (r_blk, r_blk), 0)
    jj = lax.broadcasted_iota(jnp.int32, (r_blk, r_blk), 1)
    blk_masks = [(ii >> sh) == (jj >> sh) for sh in (3, 4, 5, 6)]
    blk8 = blk_masks[0]
    tri_incl = blk_masks[3] & (ii >= jj)
    diag = ii == jj
    eye = jnp.where(diag, 1.0, 0.0)
    lane = _lane_iota((r_blk, LANES))
    s_ref[...] = jnp.zeros_like(s_ref)
    wq, wk, wv, nw = wq_ref[...], wk_ref[...], wv_ref[...], nw_ref[...]
    n_steps = seq // r_blk
    rep = GDN_V_HEADS // GDN_QK_HEADS

    def fetch(s):
        r0 = pl.multiple_of(s * r_blk, r_blk)
        rt = pl.multiple_of(jnp.maximum(r0 - SUBLANES, 0), SUBLANES)
        tail = lambda ref: jnp.where(s > 0, ref[0, pl.ds(rt, SUBLANES), :], 0.0)
        rows = pl.ds(r0, r_blk)
        return (q_ref[0, rows, :], tail(q_ref), k_ref[0, rows, :], tail(k_ref), v_ref[0, rows, :], tail(v_ref),
                gt_ref[0, rows, :])

    heads = range(GDN_STEP_QK * rep)
    qk_of = lambda j: j // rep

    def solve(raw, slot):
        xq, tq, xk, tk, xv, tv, gt = raw
        qc, kc, vc = _conv_silu_tile(xq, tq, wq), _conv_silu_tile(xk, tk, wk), _conv_silu_tile(xv, tv, wv)
        qn, kn = [], []
        for i in range(GDN_STEP_QK):
            qi, ki = qc[:, i * d:(i + 1) * d], kc[:, i * d:(i + 1) * d]
            qn.append(qi * lax.rsqrt(jnp.sum(qi * qi, axis=1, keepdims=True) + NORM_EPS) * d ** -0.5)
            kn.append(ki * lax.rsqrt(jnp.sum(ki * ki, axis=1, keepdims=True) + NORM_EPS))
            kn_st[slot, i] = kn[i]
        hv0 = len(heads) * hq
        beta = [jnp.sum(jnp.where(lane == hv0 + j, gt, 0.0), axis=1, keepdims=True) for j in heads]
        g = [jnp.sum(jnp.where(lane == GDN_V_HEADS + hv0 + j, gt, 0.0), axis=1, keepdims=True) for j in heads]
        gi = [jnp.broadcast_to(g[j], (r_blk, r_blk)) for j in heads]
        dec = [jnp.exp(jnp.where(tri_incl, gi[j] - jnp.sum(jnp.where(diag, gi[j], 0.0), axis=0, keepdims=True),
                                 -jnp.inf)) for j in heads]
        kb = [kn[qk_of(j)] * beta[j] for j in heads]
        for j in heads:
            qg_st[slot, j] = qn[qk_of(j)] * jnp.exp(g[j])
            gl_st[slot, j] = gi[j][:, :d]
        qk = [_bdot_nt(qn[i], kn[i]) for i in range(GDN_STEP_QK)]
        kk = [_bdot_nt(kb[j], kn[qk_of(j)]) for j in heads]
        yield
        for j in heads:
            in_st[slot, j] = qk[qk_of(j)] * dec[j]
        a = [jnp.where(diag, 0.0, kk[j] * dec[j]) for j in heads]
        a8 = [jnp.where(blk8, a[j], 0.0) for j in heads]
        a8_2 = [_bdot(a8[j], a8[j]) for j in heads]
        yield
        inv = [eye - a8[j] for j in heads]
        t = [_bdot(inv[j], a8_2[j]) for j in heads]
        a8_4 = [_bdot(a8_2[j], a8_2[j]) for j in heads]
        yield
        inv = [inv[j] + t[j] for j in heads]
        t = [_bdot(inv[j], a8_4[j]) for j in heads]
        yield
        inv = [inv[j] + t[j] for j in heads]
        for lvl in range(3):
            off = [jnp.where(blk_masks[lvl + 1] & ~blk_masks[lvl], a[j], 0.0) for j in heads]
            t = [_bdot(off[j], inv[j]) for j in heads]
            yield
            t = [_bdot(inv[j], t[j]) for j in heads]
            yield
            inv = [inv[j] - t[j] for j in heads]
        rhs = [jnp.concatenate([vc[:, j * d:(j + 1) * d] * beta[j], kb[j] * jnp.exp(g[j])], axis=1) for j in heads]
        x = [_bdot(inv[j], rhs[j]) for j in heads]
        yield
        for j in heads:
            wk_st[slot, j] = x[j]

    def recur(s, slot):
        r0 = pl.multiple_of(s * r_blk, r_blk)
        state = [s_ref[j] for j in heads]
        for c in range(r_blk // c_blk):
            rs = slice(c * c_blk, (c + 1) * c_blk)
            gl = [gl_st[slot, j, rs, :] for j in heads]
            g_last = [gl[j][c_blk - 1:c_blk, :] for j in heads]
            both = [_bdot(jnp.concatenate([wk_st[slot, j, rs, d:], qg_st[slot, j, rs, :]], axis=0), state[j])
                    for j in heads]
            yield
            v_new = [wk_st[slot, j, rs, :d] - both[j][:c_blk] for j in heads]
            kd = [kn_st[slot, qk_of(j), rs, :] * jnp.exp(g_last[j] - gl[j]) for j in heads]
            upd = [_bdot_tn(kd[j], v_new[j]) for j in heads]
            pad = []
            for j in heads:
                parts = []
                if c > 0:
                    parts.append(jnp.zeros((c * c_blk, d), F32))
                parts.append(v_new[j])
                if (c + 1) * c_blk < r_blk:
                    parts.append(jnp.zeros((r_blk - (c + 1) * c_blk, d), F32))
                pad.append(jnp.concatenate(parts, axis=0))
            o_in = [_bdot(in_st[slot, j, rs, :], pad[j]) for j in heads]
            yield
            state = [state[j] * jnp.exp(g_last[j]) + upd[j] for j in heads]
            rows = pl.ds(r0 + c * c_blk, c_blk)
            for j in heads:
                o_c = both[j][c_blk:] + o_in[j]
                on = o_c * lax.rsqrt(jnp.mean(o_c * o_c, axis=1, keepdims=True) + NORM_EPS) * nw
                zc = z_ref[0, rows, j * d:(j + 1) * d]
                o_ref[0, rows, j * d:(j + 1) * d] = (on * _silu(zc)).astype(o_ref.dtype)
        for j in heads:
            s_ref[j] = state[j]

    def run(*gens):
        live = list(gens)
        while live:
            live = [gen for gen in live if next(gen, StopIteration) is not StopIteration]

    run(solve(fetch(0), 0))

    def body(i, carry):
        s = 2 * i
        run(solve(fetch(s + 1), 1), recur(s, 0))
        run(solve(fetch(jnp.minimum(s + 2, n_steps - 1)), 0), recur(s + 1, 1))
        return carry

    lax.fori_loop(0, n_steps // 2, body, 0)


def _gdn_core(y, yg, conv_w, layer, a_log, dt_bias, norm_w):
    bsz, l, _ = y.shape
    assert l % (2 * GDN_SUPER) == 0
    nh, d = GDN_QK_HEADS, GDN_HEAD_DIM
    rep = GDN_V_HEADS // GDN_QK_HEADS
    lane_vec = lambda t: jnp.zeros((1, LANES), F32).at[0, GDN_V_HEADS:2 * GDN_V_HEADS].set(t.astype(F32))
    gates = pl.pallas_call(
        _gdn_gates_kernel,
        out_shape=jax.ShapeDtypeStruct((bsz, l, LANES), F32),
        grid=(bsz,),
        in_specs=[pl.BlockSpec((1, l, LANES), lambda b: (b, 0, 0)),
                  pl.BlockSpec((1, LANES), lambda b: (0, 0)), pl.BlockSpec((1, LANES), lambda b: (0, 0))],
        out_specs=pl.BlockSpec((1, l, LANES), lambda b: (b, 0, 0)),
        compiler_params=pltpu.CompilerParams(dimension_semantics=("arbitrary",), vmem_limit_bytes=VMEM_LIMIT_BYTES),
        name="gdn_gates",
    )(yg, lane_vec(a_log), lane_vec(dt_bias))
    r_blk = GDN_SUPER
    nq = GDN_STEP_QK
    qw, vw = nq * d, nq * rep * d
    nv = nq * rep
    k0, v0, z0 = GDN_KEY_DIM // qw, 2 * GDN_KEY_DIM // vw, GDN_CONV_DIM // vw
    return pl.pallas_call(
        functools.partial(_gdn_kernel, seq=l),
        out_shape=jax.ShapeDtypeStruct((bsz, l, GDN_VAL_DIM), BF16),
        grid=(bsz, nh // nq),
        in_specs=[pl.BlockSpec((1, l, qw), lambda b, h: (b, 0, h)),
                  pl.BlockSpec((1, l, qw), lambda b, h: (b, 0, k0 + h)),
                  pl.BlockSpec((1, l, vw), lambda b, h: (b, 0, v0 + h)),
                  pl.BlockSpec((1, l, vw), lambda b, h: (b, 0, z0 + h)),
                  pl.BlockSpec((1, l, LANES), lambda b, h: (b, 0, 0)),
                  pl.BlockSpec((None, GDN_CONV, qw), lambda b, h: (layer, 0, h)),
                  pl.BlockSpec((None, GDN_CONV, qw), lambda b, h: (layer, 0, k0 + h)),
                  pl.BlockSpec((None, GDN_CONV, vw), lambda b, h: (layer, 0, v0 + h)),
                  pl.BlockSpec((1, d), lambda b, h: (0, 0))],
        out_specs=pl.BlockSpec((1, l, vw), lambda b, h: (b, 0, h)),
        scratch_shapes=[pltpu.VMEM((nv, d, d), F32),
                        pltpu.VMEM((2, nq, r_blk, d), F32),
                        pltpu.VMEM((2, nv, r_blk, 2 * d), F32),
                        pltpu.VMEM((2, nv, r_blk, d), F32),
                        pltpu.VMEM((2, nv, r_blk, d), F32),
                        pltpu.VMEM((2, nv, r_blk, r_blk), F32)],
        compiler_params=_ARB2,
        name="gdn_core",
    )(y, y, y, y, gates, conv_w, conv_w, conv_w, norm_w.reshape(1, d))


def _gdn_mixer(u, w_in, conv_w, layer, a_log, dt_bias, norm_w, w_out, res):
    bsz, l, dm = u.shape
    u2 = u.reshape(bsz * l, dm)
    n_main = GDN_CONV_DIM + GDN_VAL_DIM
    y = _proj(u2, w_in, layer, n_out=n_main)
    yg = _proj(u2, _tail_cols(w_in, layer, n_main), 0)
    o = _gdn_core(y.reshape(bsz, l, -1), yg.reshape(bsz, l, -1), conv_w, layer, a_log, dt_bias, norm_w)
    return _proj(o.reshape(bsz * l, GDN_VAL_DIM), w_out, layer, mode="residual", res=res)


def _ret_kernel(q_ref, k_ref, v_ref, gate_ref, cos_ref, sin_ref, dec_ref, qd_ref, kd_ref, cd_ref, o_ref, s_ref):
    @pl.when(pl.program_id(1) == 0)
    def _():
        s_ref[...] = jnp.zeros_like(s_ref)

    half = RET_QK_HEAD // 2

    def rope(ref, rs, h, c, s):
        x1 = ref[0, rs, h * RET_QK_HEAD:h * RET_QK_HEAD + half]
        x2 = ref[0, rs, h * RET_QK_HEAD + half:(h + 1) * RET_QK_HEAD]
        return jnp.concatenate([x1 * c - x2 * s, x2 * c + x1 * s], axis=1)

    for c in range(RET_STEP // RET_CHUNK):
        rs = slice(c * RET_CHUNK, (c + 1) * RET_CHUNK)
        cos, sin = cos_ref[0, rs, :], sin_ref[0, rs, :]
        for h in range(RET_HEADS):
            vs = slice(h * RET_V_HEAD, (h + 1) * RET_V_HEAD)
            qr = rope(q_ref, rs, h, cos, sin)
            kr = rope(k_ref, rs, h, cos, sin) * RET_QK_HEAD ** -0.5
            v = v_ref[0, rs, vs]
            qdec = jnp.concatenate([qd_ref[h], qd_ref[h]], axis=1)
            kdec = jnp.concatenate([kd_ref[h], kd_ref[h]], axis=1)
            o = _bdot(_bdot_nt(qr, kr) * dec_ref[h], v) + _bdot(qr * qdec, s_ref[h])
            s_ref[h] = s_ref[h] * cd_ref[h] + _bdot_tn(kr * kdec, v)
            on = o * lax.rsqrt(jnp.mean(o * o, axis=1, keepdims=True) + NORM_EPS)
            o_ref[0, rs, vs] = (_silu(gate_ref[0, rs, vs]) * on).astype(o_ref.dtype)


def _ret_core(y, positions):
    bsz, l, _ = y.shape
    c = RET_CHUNK
    inv_freq = ROPE_THETA ** (-jnp.arange(0, RET_QK_HEAD, 2, dtype=F32) / RET_QK_HEAD)
    ang = positions.astype(F32)[..., None] * inv_freq
    cos, sin = jnp.cos(ang), jnp.sin(ang)
    log_gamma = jnp.log1p(-jnp.exp2(-5.0 - jnp.arange(RET_HEADS, dtype=F32)))
    idx = jnp.arange(c, dtype=F32)
    rel = idx[:, None] - idx[None, :]
    inner_decay = jnp.where(rel >= 0, jnp.exp(log_gamma[:, None, None] * jnp.maximum(rel, 0.0)), 0.0)
    lanes = lambda t: jnp.broadcast_to(t[:, :, None], (RET_HEADS, c, LANES))
    q_decay = lanes(jnp.exp(log_gamma[:, None] * (idx[None, :] + 1.0)))
    k_decay = lanes(jnp.exp(log_gamma[:, None] * (c - 1.0 - idx[None, :])))
    chunk_decay = jnp.exp(log_gamma * c)
    t = RET_STEP
    full = lambda shape: pl.BlockSpec(shape, lambda b, i: (0,) * len(shape))
    return pl.pallas_call(
        _ret_kernel,
        out_shape=jax.ShapeDtypeStruct((bsz, l, RET_V_DIM), BF16),
        grid=(bsz, l // t),
        in_specs=[pl.BlockSpec((1, t, RET_QK_DIM), lambda b, i: (b, i, 0)),
                  pl.BlockSpec((1, t, RET_QK_DIM), lambda b, i: (b, i, 1)),
                  pl.BlockSpec((1, t, RET_V_DIM), lambda b, i: (b, i, 1)),
                  pl.BlockSpec((1, t, RET_V_DIM), lambda b, i: (b, i, 2)),
                  pl.BlockSpec((1, t, LANES), lambda b, i: (b, i, 0)),
                  pl.BlockSpec((1, t, LANES), lambda b, i: (b, i, 0)),
                  full((RET_HEADS, c, c)), full((RET_HEADS, c, LANES)), full((RET_HEADS, c, LANES)),
                  pl.BlockSpec(memory_space=pltpu.SMEM)],
        out_specs=pl.BlockSpec((1, t, RET_V_DIM), lambda b, i: (b, i, 0)),
        scratch_shapes=[pltpu.VMEM((RET_HEADS, RET_QK_HEAD, RET_V_HEAD), F32)],
        compiler_params=_ARB2,
        name="ret_core",
    )(y, y, y, y, cos, sin, inner_decay, q_decay, k_decay, chunk_decay)


def _retention_mixer(u, positions, w_in, w_out, layer, res):
    bsz, l, dm = u.shape
    y = _proj(u.reshape(bsz * l, dm), w_in, layer)
    o = _ret_core(y.reshape(bsz, l, -1), positions)
    return _proj(o.reshape(bsz * l, RET_V_DIM), w_out, layer, mode="residual", res=res)


def _rope_tables(positions, dim):
    inv_freq = ROPE_THETA ** (-jnp.arange(0, dim, 2, dtype=F32) / dim)
    ang = positions.astype(F32)[..., None] * inv_freq
    c, s = jnp.cos(ang), jnp.sin(ang)
    reps = LANES // dim
    cos_full = jnp.tile(jnp.concatenate([c, c], axis=-1), (1, 1, reps))
    sin_signed = jnp.tile(jnp.concatenate([-s, s], axis=-1), (1, 1, reps))
    return cos_full, sin_signed


def _rope128(x, cos, sin_signed):
    return x * cos + pltpu.roll(x, 64, 1) * sin_signed


def _rope64(x, cos, sin_signed, low_half):
    partner = jnp.where(low_half, pltpu.roll(x, 96, 1), pltpu.roll(x, 32, 1))
    return x * cos + partner * sin_signed


def _dsa_prep_kernel(k_ref, v_ref, ki_ref, ck_ref, sk_ref, cik_ref, sik_ref, kb_ref, vb_ref, ki2_ref):
    ck, sk = ck_ref[0], sk_ref[0]
    for g in range(DSA_KV_HEADS):
        sl = slice(g * DSA_HEAD_DIM, (g + 1) * DSA_HEAD_DIM)
        kb_ref[0, g] = _rope128(k_ref[0, :, sl], ck, sk).astype(BF16)
        vb_ref[0, g] = v_ref[0, :, sl].astype(BF16)
    x = ki_ref[0]
    lane = _lane_iota(x.shape)
    xr = _rope64(x, cik_ref[0], sik_ref[0], (lane % IDX_HEAD_DIM) < IDX_HEAD_DIM // 2)
    ki2_ref[0] = jnp.where(lane < IDX_HEAD_DIM, xr, pltpu.roll(xr, IDX_HEAD_DIM, 1)).astype(BF16)


def _dsa_block(s_eff, q_ref, qi_ref, wi_ref, cq_ref, sq_ref, ciq_ref, siq_ref, kb_ref, vb_ref, ki2_ref, o_ref,
               key_ref, *, topk):
    t_blk = DSA_BLOCK
    qb = pl.program_id(1)
    lane = _lane_iota((t_blk, LANES))
    low_half = (lane % IDX_HEAD_DIM) < IDX_HEAD_DIM // 2
    lo64 = lane < IDX_HEAD_DIM

    wi = wi_ref[0] * (IDX_HEADS ** -0.5 * IDX_HEAD_DIM ** -0.5)
    ciq, siq = ciq_ref[0], siq_ref[0]
    ki2 = ki2_ref[0, :s_eff, :]
    score = jnp.zeros((t_blk, s_eff), F32)
    for quad in range(IDX_HEADS // 4):
        rows = []
        for pr in (2 * quad, 2 * quad + 1):
            xr = _rope64(qi_ref[0, :, pr * LANES:(pr + 1) * LANES], ciq, siq, low_half)
            rows.append(jnp.where(lo64, xr, 0.0))
            rows.append(jnp.where(lo64, 0.0, xr))
        s4 = _bdot_nt(jnp.concatenate(rows, axis=0), ki2)
        for j in range(4):
            hh = 4 * quad + j
            w_col = wi[:, IDX_HEAD_DIM + hh:IDX_HEAD_DIM + hh + 1]
            score = score + w_col * jnp.maximum(s4[j * t_blk:(j + 1) * t_blk], 0.0)

    t_idx = qb * t_blk + lax.broadcasted_iota(jnp.int32, (t_blk, 1), 0)
    s_idx = _lane_iota((t_blk, s_eff))
    score = jnp.where(s_idx <= t_idx, score, -jnp.inf)
    bits = lax.bitcast_convert_type(score, jnp.int32)
    key_ref[:, :s_eff] = bits ^ ((bits >> 31) & jnp.int32(0x7FFFFFFF))

    kk = jnp.minimum(t_idx + 1, topk).astype(F32)
    int_min = jnp.int32(-2 ** 31)

    def bisect(i, t_u):
        cand_u = t_u | lax.shift_left(jnp.int32(1), 31 - i)
        hit = key_ref[:, :s_eff] >= (cand_u ^ int_min)
        cnt = jnp.sum(jnp.where(hit, 1.0, 0.0), axis=1, keepdims=True)
        return jnp.where(cnt >= kk, cand_u, t_u)

    t_u = lax.fori_loop(0, 32, bisect, jnp.zeros((t_blk, 1), jnp.int32))
    sel = key_ref[:, :s_eff] >= (t_u ^ int_min)

    cq, sq = cq_ref[0], sq_ref[0]
    group = DSA_HEADS // DSA_KV_HEADS
    def group_logits(g):
        qs = []
        for j in range(group):
            hh = group * g + j
            qs.append(_rope128(q_ref[0, :, hh * DSA_HEAD_DIM:(hh + 1) * DSA_HEAD_DIM], cq, sq))
        return _bdot_nt(jnp.concatenate(qs, axis=0), kb_ref[0, g, :s_eff, :]) * DSA_HEAD_DIM ** -0.5

    next_logits = group_logits(0)
    for g in range(DSA_KV_HEADS):
        logits = next_logits
        if g + 1 < DSA_KV_HEADS:
            next_logits = group_logits(g + 1)
        ps, dens = [], []
        for j in range(group):
            lj = jnp.where(sel, logits[j * t_blk:(j + 1) * t_blk], -jnp.inf)
            pj = jnp.exp(lj - jnp.max(lj, axis=1, keepdims=True))
            dens.append(jnp.sum(pj, axis=1, keepdims=True))
            ps.append(pj)
        pv = _bdot(jnp.concatenate(ps, axis=0), vb_ref[0, g, :s_eff, :])
        for j in range(group):
            hh = group * g + j
            o_ref[0, :, hh * DSA_HEAD_DIM:(hh + 1) * DSA_HEAD_DIM] = (
                pv[j * t_blk:(j + 1) * t_blk] / dens[j]).astype(o_ref.dtype)


def _dsa_kernel(q_ref, qi_ref, wi_ref, cq_ref, sq_ref, ciq_ref, siq_ref, kb_ref, vb_ref, ki2_ref, o_ref, key_ref,
                *, topk, seq):
    qb = pl.program_id(1)
    blocks_per_step = DSA_KV_STEP // DSA_BLOCK
    for br in range(seq // DSA_KV_STEP):
        @pl.when(qb // blocks_per_step == br)
        def _():
            _dsa_block((br + 1) * DSA_KV_STEP, q_ref, qi_ref, wi_ref, cq_ref, sq_ref, ciq_ref, siq_ref, kb_ref,
                       vb_ref, ki2_ref, o_ref, key_ref, topk=topk)


def _dsa_core(y, yx, positions):
    bsz, l, _ = y.shape
    topk = min(DSA_TOPK_MAX, l // 4)
    cos_q, sin_q = _rope_tables(positions, DSA_HEAD_DIM)
    cos_i, sin_i = _rope_tables(positions, IDX_HEAD_DIM)
    k_blk = DSA_Q_DIM // DSA_KV_DIM
    ki_blk = 0
    ps = DSA_KV_STEP
    tab = pl.BlockSpec((1, ps, LANES), lambda b, i: (b, i, 0))
    kb, vb, ki2 = pl.pallas_call(
        _dsa_prep_kernel,
        out_shape=(jax.ShapeDtypeStruct((bsz, DSA_KV_HEADS, l, DSA_HEAD_DIM), BF16),
                   jax.ShapeDtypeStruct((bsz, DSA_KV_HEADS, l, DSA_HEAD_DIM), BF16),
                   jax.ShapeDtypeStruct((bsz, l, LANES), BF16)),
        grid=(bsz, l // ps),
        in_specs=[pl.BlockSpec((1, ps, DSA_KV_DIM), lambda b, i: (b, i, k_blk)),
                  pl.BlockSpec((1, ps, DSA_KV_DIM), lambda b, i: (b, i, k_blk + 1)),
                  pl.BlockSpec((1, ps, LANES), lambda b, i: (b, i, ki_blk)),
                  tab, tab, tab, tab],
        out_specs=(pl.BlockSpec((1, DSA_KV_HEADS, ps, DSA_HEAD_DIM), lambda b, i: (b, 0, i, 0)),
                   pl.BlockSpec((1, DSA_KV_HEADS, ps, DSA_HEAD_DIM), lambda b, i: (b, 0, i, 0)),
                   pl.BlockSpec((1, ps, LANES), lambda b, i: (b, i, 0))),
        compiler_params=_ARB2,
        name="dsa_prep",
    )(y, y, yx, cos_q, sin_q, cos_i, sin_i)

    t_blk = DSA_BLOCK
    qtab = pl.BlockSpec((1, t_blk, LANES), lambda b, i: (b, i, 0))
    kv_spec = pl.BlockSpec((1, DSA_KV_HEADS, l, DSA_HEAD_DIM), lambda b, i: (b, 0, 0, 0))
    return pl.pallas_call(
        functools.partial(_dsa_kernel, topk=topk, seq=l),
        out_shape=jax.ShapeDtypeStruct((bsz, l, DSA_Q_DIM), BF16),
        grid=(bsz, l // t_blk),
        in_specs=[pl.BlockSpec((1, t_blk, DSA_Q_DIM), lambda b, i: (b, i, 0)),
                  pl.BlockSpec((1, t_blk, IDX_HEADS * IDX_HEAD_DIM),
                               lambda b, i: (b, i, (DSA_Q_DIM + 2 * DSA_KV_DIM) // (IDX_HEADS * IDX_HEAD_DIM))),
                  pl.BlockSpec((1, t_blk, LANES), lambda b, i: (b, i, ki_blk)),
                  qtab, qtab, qtab, qtab,
                  kv_spec, kv_spec,
                  pl.BlockSpec((1, l, LANES), lambda b, i: (b, 0, 0))],
        out_specs=pl.BlockSpec((1, t_blk, DSA_Q_DIM), lambda b, i: (b, i, 0)),
        scratch_shapes=[pltpu.VMEM((t_blk, l), jnp.int32)],
        compiler_params=_ARB2,
        name="dsa_attn",
    )(y, y, yx, cos_q, sin_q, cos_i, sin_i, kb, vb, ki2)


def _dsa_mixer(u, positions, w_in, w_out, layer, res):
    bsz, l, dm = u.shape
    u2 = u.reshape(bsz * l, dm)
    n_main = DSA_Q_DIM + 2 * DSA_KV_DIM + IDX_HEADS * IDX_HEAD_DIM
    y = _proj(u2, w_in, layer, n_out=n_main)
    yx = _proj(u2, _tail_cols(w_in, layer, n_main), 0)
    o = _dsa_core(y.reshape(bsz, l, -1), yx.reshape(bsz, l, -1), positions)
    return _proj(o.reshape(bsz * l, DSA_Q_DIM), w_out, layer, mode="residual", res=res)


def kernel(x, p, positions, norm_mix, norm_mlp, norm_ple, norm_final, gdn_w_in, gdn_conv_w, gdn_a_log,
           gdn_dt_bias, gdn_norm, gdn_w_out, ret_w_in, ret_w_out, dsa_w_in, dsa_w_out, mlp_w_up, mlp_w_down,
           ple_w_gate, ple_w_proj):
    bsz, l, d = x.shape
    m = bsz * l
    h = x.reshape(m, d)
    for i in range(DEPTH):
        kind, j = i % N_MIXERS, i // N_MIXERS
        u = _rmsnorm(h, norm_mix[i], BF16).reshape(bsz, l, d)
        if kind == 0:
            h = _gdn_mixer(u, gdn_w_in, gdn_conv_w, j, gdn_a_log[j], gdn_dt_bias[j], gdn_norm[j], gdn_w_out, h)
        elif kind == 1:
            h = _retention_mixer(u, positions, ret_w_in, ret_w_out, j, h)
        else:
            h = _dsa_mixer(u, positions, dsa_w_in, dsa_w_out, j, h)
        u = _rmsnorm(h, norm_mlp[i], BF16)
        a = _proj(u, mlp_w_up, i, mode="relu2", out_dtype=BF16)
        h = _proj(a, mlp_w_down, i, mode="residual", res=h)
        u = _rmsnorm(h, norm_ple[i], BF16)
        h = _proj(u, ple_w_gate, i, mode="ple", res=h, p=p.reshape(DEPTH, m, -1), wp=ple_w_proj)
    return _rmsnorm(h, norm_final, x.dtype).reshape(bsz, l, d)
```

```python
import functools

import jax
import jax.numpy as jnp
from jax import lax
from jax.experimental import pallas as pl
from jax.experimental.pallas import tpu as pltpu

F32 = jnp.float32
BF16 = jnp.bfloat16

D_MODEL = 2048
DEPTH = 4
N_MIXERS = 3
NORM_EPS = 1e-6
ROPE_THETA = 10000.0

GDN_HEAD_DIM = 128
GDN_QK_HEADS = D_MODEL // 128
GDN_V_HEADS = 2 * GDN_QK_HEADS
GDN_CONV = 4
GDN_CHUNK = 64
GDN_KEY_DIM = GDN_QK_HEADS * GDN_HEAD_DIM
GDN_VAL_DIM = GDN_V_HEADS * GDN_HEAD_DIM
GDN_CONV_DIM = 2 * GDN_KEY_DIM + GDN_VAL_DIM
GDN_SUPER = 256
GDN_STEP_QK = 2

RET_HEADS = 8
RET_QK_DIM = D_MODEL
RET_V_DIM = 2 * D_MODEL
RET_QK_HEAD = RET_QK_DIM // RET_HEADS
RET_V_HEAD = RET_V_DIM // RET_HEADS
RET_CHUNK = 128
RET_STEP = 256

DSA_HEAD_DIM = 128
DSA_HEADS = D_MODEL // DSA_HEAD_DIM
DSA_KV_HEADS = 4
IDX_HEADS = 16
IDX_HEAD_DIM = 64
DSA_TOPK_MAX = 256
DSA_BLOCK = 128
DSA_KV_STEP = 512
DSA_Q_DIM = DSA_HEADS * DSA_HEAD_DIM
DSA_KV_DIM = DSA_KV_HEADS * DSA_HEAD_DIM

LANES = 128
SUBLANES = 8
VMEM_LIMIT_BYTES = 56 * 1024 * 1024

_ARB2 = pltpu.CompilerParams(dimension_semantics=("arbitrary", "arbitrary"), vmem_limit_bytes=VMEM_LIMIT_BYTES)


def _bdot(a, b):
    return jnp.dot(a.astype(BF16), b.astype(BF16), preferred_element_type=F32)


def _bdot_nt(a, b):
    return lax.dot_general(a.astype(BF16), b.astype(BF16), (((1,), (1,)), ((), ())), preferred_element_type=F32)


def _bdot_tn(a, b):
    return lax.dot_general(a.astype(BF16), b.astype(BF16), (((0,), (0,)), ((), ())), preferred_element_type=F32)


def _lane_iota(shape):
    return lax.broadcasted_iota(jnp.int32, shape, 1)


def _silu(x):
    return x * jax.nn.sigmoid(x)


def _mm_kernel(*refs, nk, mode, valid):
    x_ref, w_ref, o_ref = refs[0], refs[1], refs[-1]
    extra = refs[2:-1]
    if nk > 1:
        @pl.when(pl.program_id(2) == 0)
        def _():
            o_ref[...] = extra[0][...]

        half = o_ref.shape[1] // 2
        for cs in (slice(0, half), slice(half, 2 * half)):
            o_ref[:, cs] += _bdot(x_ref[...], w_ref[:, cs])
        return
    if mode == "tail":
        w = jnp.where(_lane_iota(w_ref.shape) < valid, w_ref[...], 0.0)
        o_ref[...] = _bdot(x_ref[...], w)
        return
    prod = _bdot(x_ref[...], w_ref[...])
    if mode == "plain":
        o_ref[...] = prod.astype(o_ref.dtype)
    elif mode == "relu2":
        o_ref[...] = jnp.square(jnp.maximum(prod, 0.0)).astype(o_ref.dtype)
    elif mode == "ple":
        res_ref, p_ref, wp_ref = extra
        o_ref[...] = res_ref[...] + jax.nn.sigmoid(prod) * _bdot(p_ref[...], wp_ref[...])
    else:
        o_ref[...] = extra[0][...] + prod


def _matmul(x, w, layer, *, n_out, bm, bn, bk, n_outer, col0=0, mode="plain", out_dtype=F32, res=None, p=None,
            wp=None):
    m, kdim = x.shape
    bm = min(bm, m)
    assert m % bm == 0 and n_out % bn == 0 and col0 % bn == 0 and kdim % bk == 0, (x.shape, w.shape, bm, bn, bk)
    assert col0 + n_out <= w.shape[2] or (mode == "tail" and n_out == bn)
    nk = kdim // bk
    assert nk == 1 or mode == "residual"
    cb0 = col0 // bn
    if n_outer:
        grid = (n_out // bn, m // bm, nk)
        mi, ni = (lambda j, i, k: i), (lambda j, i, k: j)
    else:
        grid = (m // bm, n_out // bn, nk)
        mi, ni = (lambda i, j, k: i), (lambda i, j, k: j)
    in_specs = [pl.BlockSpec((bm, bk), lambda a, b, k: (mi(a, b, k), k)),
                pl.BlockSpec((None, bk, bn), lambda a, b, k: (layer, k, cb0 + ni(a, b, k)))]
    args = [x, w]
    o_map = lambda a, b, k: (mi(a, b, k), ni(a, b, k))
    if mode in ("residual", "ple"):
        in_specs.append(pl.BlockSpec((bm, bn), o_map))
        args.append(res)
    if mode == "ple":
        kp = p.shape[2]
        in_specs.append(pl.BlockSpec((None, bm, kp), lambda a, b, k: (layer, mi(a, b, k), 0)))
        in_specs.append(pl.BlockSpec((None, kp, bn), lambda a, b, k: (layer, 0, ni(a, b, k))))
        args += [p, wp]
    return pl.pallas_call(
        functools.partial(_mm_kernel, nk=nk, mode=mode, valid=w.shape[2] - col0),
        out_shape=jax.ShapeDtypeStruct((m, n_out), out_dtype),
        grid=grid,
        in_specs=in_specs,
        out_specs=pl.BlockSpec((bm, bn), o_map),
        compiler_params=pltpu.CompilerParams(
            dimension_semantics=("arbitrary", "arbitrary", "arbitrary"), vmem_limit_bytes=VMEM_LIMIT_BYTES),
        name="matmul_" + mode,
    )(*args)


def _proj(x, w, layer, n_out=None, **kw):
    kdim = w.shape[1]
    n_out = w.shape[2] if n_out is None else n_out
    if kdim <= 2048:
        bn = 512 if kw.get("mode") == "ple" else min(n_out, 1024)
        return _matmul(x, w, layer, n_out=n_out, bm=1024, bn=bn, bk=kdim, n_outer=True, **kw)
    return _matmul(x, w, layer, n_out=n_out, bm=2048, bn=1024, bk=1024, n_outer=False, **kw)


def _proj_tail(x, w, layer, start):
    return _matmul(x, w, layer, n_out=LANES, col0=start, bm=1024, bn=LANES, bk=w.shape[1], n_outer=True, mode="tail")


def _rmsnorm_kernel(x_ref, g_ref, o_ref):
    x = x_ref[...]
    y = x * lax.rsqrt(jnp.mean(x * x, axis=1, keepdims=True) + NORM_EPS) * g_ref[...]
    o_ref[...] = y.astype(o_ref.dtype)


def _rmsnorm(x, gain, out_dtype):
    m, d = x.shape
    bm = min(m, 512)
    return pl.pallas_call(
        _rmsnorm_kernel,
        out_shape=jax.ShapeDtypeStruct((m, d), out_dtype),
        grid=(m // bm,),
        in_specs=[pl.BlockSpec((bm, d), lambda i: (i, 0)), pl.BlockSpec((1, d), lambda i: (0, 0))],
        out_specs=pl.BlockSpec((bm, d), lambda i: (i, 0)),
        compiler_params=pltpu.CompilerParams(dimension_semantics=("arbitrary",), vmem_limit_bytes=VMEM_LIMIT_BYTES),
        name="rmsnorm",
    )(x, gain.reshape(1, d))


def _gdn_gates_kernel(gb_ref, alog_ref, dtb_ref, o_ref):
    x = gb_ref[0]
    lane = _lane_iota(x.shape)
    row = lax.broadcasted_iota(jnp.int32, x.shape, 0) % GDN_CHUNK
    g = -jnp.exp(alog_ref[...]) * jax.nn.softplus(x + dtb_ref[...])
    sh = 1
    while sh < GDN_CHUNK:
        g = g + jnp.where(row >= sh, pltpu.roll(g, sh, 0), 0.0)
        sh *= 2
    o_ref[0] = jnp.where(lane < GDN_V_HEADS, jax.nn.sigmoid(x), g)


def _conv_silu_tile(x, tail, w):
    rows8 = lax.broadcasted_iota(jnp.int32, tail.shape, 0)
    acc = x * w[GDN_CONV - 1:GDN_CONV]
    for sh in range(1, GDN_CONV):
        xs = pltpu.roll(x, sh, 0)
        head = jnp.where(rows8 < sh, pltpu.roll(tail, sh, 0), xs[:SUBLANES])
        xs = jnp.concatenate([head, xs[SUBLANES:]], axis=0)
        acc = acc + xs * w[GDN_CONV - 1 - sh:GDN_CONV - sh]
    return _silu(acc)


def _gdn_kernel(q_ref, k_ref, v_ref, z_ref, gt_ref, wq_ref, wk_ref, wv_ref, nw_ref, o_ref,
                s_ref, kn_st, wk_st, qg_st, gl_st, in_st, *, seq):
    hq = pl.program_id(1)
    r_blk, c_blk, d = GDN_SUPER, GDN_CHUNK, GDN_HEAD_DIM
    ii = lax.broadcasted_iota(jnp.int32, (r_blk, r_blk), 0)
    jj = lax.broadcasted_iota(jnp.int32, (r_blk, r_blk), 1)
    blk_masks = [(ii >> sh) == (jj >> sh) for sh in (3, 4, 5, 6)]
    blk8 = blk_masks[0]
    tri_incl = blk_masks[3] & (ii >= jj)
    diag = ii == jj
    eye = jnp.where(diag, 1.0, 0.0)
    lane = _lane_iota((r_blk, LANES))
    s_ref[...] = jnp.zeros_like(s_ref)
    wq, wk, wv, nw = wq_ref[...], wk_ref[...], wv_ref[...], nw_ref[...]
    n_steps = seq // r_blk
    rep = GDN_V_HEADS // GDN_QK_HEADS

    def fetch(s):
        r0 = pl.multiple_of(s * r_blk, r_blk)
        rt = pl.multiple_of(jnp.maximum(r0 - SUBLANES, 0), SUBLANES)
        tail = lambda ref: jnp.where(s > 0, ref[0, pl.ds(rt, SUBLANES), :], 0.0)
        rows = pl.ds(r0, r_blk)
        return (q_ref[0, rows, :], tail(q_ref), k_ref[0, rows, :], tail(k_ref), v_ref[0, rows, :], tail(v_ref),
                gt_ref[0, rows, :])

    heads = range(GDN_STEP_QK * rep)
    qk_of = lambda j: j // rep

    def solve(raw, slot):
        xq, tq, xk, tk, xv, tv, gt = raw
        qc, kc, vc = _conv_silu_tile(xq, tq, wq), _conv_silu_tile(xk, tk, wk), _conv_silu_tile(xv, tv, wv)
        qn, kn = [], []
        for i in range(GDN_STEP_QK):
            qi, ki = qc[:, i * d:(i + 1) * d], kc[:, i * d:(i + 1) * d]
            qn.append(qi * lax.rsqrt(jnp.sum(qi * qi, axis=1, keepdims=True) + NORM_EPS) * d ** -0.5)
            kn.append(ki * lax.rsqrt(jnp.sum(ki * ki, axis=1, keepdims=True) + NORM_EPS))
            kn_st[slot, i] = kn[i]
        hv0 = len(heads) * hq
        beta = [jnp.sum(jnp.where(lane == hv0 + j, gt, 0.0), axis=1, keepdims=True) for j in heads]
        g = [jnp.sum(jnp.where(lane == GDN_V_HEADS + hv0 + j, gt, 0.0), axis=1, keepdims=True) for j in heads]
        gi = [jnp.broadcast_to(g[j], (r_blk, r_blk)) for j in heads]
        dec = [jnp.exp(jnp.where(tri_incl, gi[j] - jnp.sum(jnp.where(diag, gi[j], 0.0), axis=0, keepdims=True),
                                 -jnp.inf)) for j in heads]
        kb = [kn[qk_of(j)] * beta[j] for j in heads]
        for j in heads:
            qg_st[slot, j] = qn[qk_of(j)] * jnp.exp(g[j])
            gl_st[slot, j] = gi[j][:, :d]
        qk = [_bdot_nt(qn[i], kn[i]) for i in range(GDN_STEP_QK)]
        kk = [_bdot_nt(kb[j], kn[qk_of(j)]) for j in heads]
        yield
        for j in heads:
            in_st[slot, j] = qk[qk_of(j)] * dec[j]
        a = [jnp.where(diag, 0.0, kk[j] * dec[j]) for j in heads]
        a8 = [jnp.where(blk8, a[j], 0.0) for j in heads]
        a8_2 = [_bdot(a8[j], a8[j]) for j in heads]
        yield
        inv = [eye - a8[j] for j in heads]
        t = [_bdot(inv[j], a8_2[j]) for j in heads]
        a8_4 = [_bdot(a8_2[j], a8_2[j]) for j in heads]
        yield
        inv = [inv[j] + t[j] for j in heads]
        t = [_bdot(inv[j], a8_4[j]) for j in heads]
        yield
        inv = [inv[j] + t[j] for j in heads]
        for lvl in range(3):
            off = [jnp.where(blk_masks[lvl + 1] & ~blk_masks[lvl], a[j], 0.0) for j in heads]
            t = [_bdot(off[j], inv[j]) for j in heads]
            yield
            t = [_bdot(inv[j], t[j]) for j in heads]
            yield
            inv = [inv[j] - t[j] for j in heads]
        rhs = [jnp.concatenate([vc[:, j * d:(j + 1) * d] * beta[j], kb[j] * jnp.exp(g[j])], axis=1) for j in heads]
        x = [_bdot(inv[j], rhs[j]) for j in heads]
        yield
        for j in heads:
            wk_st[slot, j] = x[j]

    def recur(s, slot):
        r0 = pl.multiple_of(s * r_blk, r_blk)
        state = [s_ref[j] for j in heads]
        for c in range(r_blk // c_blk):
            rs = slice(c * c_blk, (c + 1) * c_blk)
            gl = [gl_st[slot, j, rs, :] for j in heads]
            g_last = [gl[j][c_blk - 1:c_blk, :] for j in heads]
            both = [_bdot(jnp.concatenate([wk_st[slot, j, rs, d:], qg_st[slot, j, rs, :]], axis=0), state[j])
                    for j in heads]
            yield
            v_new = [wk_st[slot, j, rs, :d] - both[j][:c_blk] for j in heads]
            kd = [kn_st[slot, qk_of(j), rs, :] * jnp.exp(g_last[j] - gl[j]) for j in heads]
            upd = [_bdot_tn(kd[j], v_new[j]) for j in heads]
            pad = []
            for j in heads:
                parts = []
                if c > 0:
                    parts.append(jnp.zeros((c * c_blk, d), F32))
                parts.append(v_new[j])
                if (c + 1) * c_blk < r_blk:
                    parts.append(jnp.zeros((r_blk - (c + 1) * c_blk, d), F32))
                pad.append(jnp.concatenate(parts, axis=0))
            o_in = [_bdot(in_st[slot, j, rs, :], pad[j]) for j in heads]
            yield
            state = [state[j] * jnp.exp(g_last[j]) + upd[j] for j in heads]
            rows = pl.ds(r0 + c * c_blk, c_blk)
            for j in heads:
                o_c = both[j][c_blk:] + o_in[j]
                on = o_c * lax.rsqrt(jnp.mean(o_c * o_c, axis=1, keepdims=True) + NORM_EPS) * nw
                zc = z_ref[0, rows, j * d:(j + 1) * d]
                o_ref[0, rows, j * d:(j + 1) * d] = (on * _silu(zc)).astype(o_ref.dtype)
        for j in heads:
            s_ref[j] = state[j]

    def run(*gens):
        live = list(gens)
        while live:
            live = [gen for gen in live if next(gen, StopIteration) is not StopIteration]

    run(solve(fetch(0), 0))

    def body(i, carry):
        s = 2 * i
        run(solve(fetch(s + 1), 1), recur(s, 0))
        run(solve(fetch(s + 2), 0), recur(s + 1, 1))
        return carry

    lax.fori_loop(0, n_steps // 2 - 1, body, 0)
    run(solve(fetch(n_steps - 1), 1), recur(n_steps - 2, 0))
    run(recur(n_steps - 1, 1))


def _gdn_core(y, yg, conv_w, layer, a_log, dt_bias, norm_w):
    bsz, l, _ = y.shape
    assert l % (2 * GDN_SUPER) == 0
    nh, d = GDN_QK_HEADS, GDN_HEAD_DIM
    rep = GDN_V_HEADS // GDN_QK_HEADS
    lane_vec = lambda t: jnp.zeros((1, LANES), F32).at[0, GDN_V_HEADS:2 * GDN_V_HEADS].set(t.astype(F32))
    gates = pl.pallas_call(
        _gdn_gates_kernel,
        out_shape=jax.ShapeDtypeStruct((bsz, l, LANES), F32),
        grid=(bsz,),
        in_specs=[pl.BlockSpec((1, l, LANES), lambda b: (b, 0, 0)),
                  pl.BlockSpec((1, LANES), lambda b: (0, 0)), pl.BlockSpec((1, LANES), lambda b: (0, 0))],
        out_specs=pl.BlockSpec((1, l, LANES), lambda b: (b, 0, 0)),
        compiler_params=pltpu.CompilerParams(dimension_semantics=("arbitrary",), vmem_limit_bytes=VMEM_LIMIT_BYTES),
        name="gdn_gates",
    )(yg, lane_vec(a_log), lane_vec(dt_bias))
    r_blk = GDN_SUPER
    nq = GDN_STEP_QK
    qw, vw = nq * d, nq * rep * d
    nv = nq * rep
    k0, v0, z0 = GDN_KEY_DIM // qw, 2 * GDN_KEY_DIM // vw, GDN_CONV_DIM // vw
    return pl.pallas_call(
        functools.partial(_gdn_kernel, seq=l),
        out_shape=jax.ShapeDtypeStruct((bsz, l, GDN_VAL_DIM), BF16),
        grid=(bsz, nh // nq),
        in_specs=[pl.BlockSpec((1, l, qw), lambda b, h: (b, 0, h)),
                  pl.BlockSpec((1, l, qw), lambda b, h: (b, 0, k0 + h)),
                  pl.BlockSpec((1, l, vw), lambda b, h: (b, 0, v0 + h)),
                  pl.BlockSpec((1, l, vw), lambda b, h: (b, 0, z0 + h)),
                  pl.BlockSpec((1, l, LANES), lambda b, h: (b, 0, 0)),
                  pl.BlockSpec((None, GDN_CONV, qw), lambda b, h: (layer, 0, h)),
                  pl.BlockSpec((None, GDN_CONV, qw), lambda b, h: (layer, 0, k0 + h)),
                  pl.BlockSpec((None, GDN_CONV, vw), lambda b, h: (layer, 0, v0 + h)),
                  pl.BlockSpec((1, d), lambda b, h: (0, 0))],
        out_specs=pl.BlockSpec((1, l, vw), lambda b, h: (b, 0, h)),
        scratch_shapes=[pltpu.VMEM((nv, d, d), F32),
                        pltpu.VMEM((2, nq, r_blk, d), F32),
                        pltpu.VMEM((2, nv, r_blk, 2 * d), F32),
                        pltpu.VMEM((2, nv, r_blk, d), F32),
                        pltpu.VMEM((2, nv, r_blk, d), F32),
                        pltpu.VMEM((2, nv, r_blk, r_blk), F32)],
        compiler_params=_ARB2,
        name="gdn_core",
    )(y, y, y, y, gates, conv_w, conv_w, conv_w, norm_w.reshape(1, d))


def _gdn_mixer(u, w_in, conv_w, layer, a_log, dt_bias, norm_w, w_out, res):
    bsz, l, dm = u.shape
    u2 = u.reshape(bsz * l, dm)
    n_main = GDN_CONV_DIM + GDN_VAL_DIM
    y = _proj(u2, w_in, layer, n_out=n_main)
    yg = _proj_tail(u2, w_in, layer, n_main)
    o = _gdn_core(y.reshape(bsz, l, -1), yg.reshape(bsz, l, -1), conv_w, layer, a_log, dt_bias, norm_w)
    return _proj(o.reshape(bsz * l, GDN_VAL_DIM), w_out, layer, mode="residual", res=res)


def _ret_kernel(q_ref, k_ref, v_ref, gate_ref, cos_ref, sin_ref, dec_ref, qd_ref, kd_ref, cd_ref, o_ref, s_ref):
    @pl.when(pl.program_id(1) == 0)
    def _():
        s_ref[...] = jnp.zeros_like(s_ref)

    half = RET_QK_HEAD // 2

    def rope(ref, rs, h, c, s):
        x1 = ref[0, rs, h * RET_QK_HEAD:h * RET_QK_HEAD + half]
        x2 = ref[0, rs, h * RET_QK_HEAD + half:(h + 1) * RET_QK_HEAD]
        return jnp.concatenate([x1 * c - x2 * s, x2 * c + x1 * s], axis=1)

    for c in range(RET_STEP // RET_CHUNK):
        rs = slice(c * RET_CHUNK, (c + 1) * RET_CHUNK)
        cos, sin = cos_ref[0, rs, :], sin_ref[0, rs, :]
        for h in range(RET_HEADS):
            vs = slice(h * RET_V_HEAD, (h + 1) * RET_V_HEAD)
            qr = rope(q_ref, rs, h, cos, sin)
            kr = rope(k_ref, rs, h, cos, sin) * RET_QK_HEAD ** -0.5
            v = v_ref[0, rs, vs]
            qdec = jnp.concatenate([qd_ref[h], qd_ref[h]], axis=1)
            kdec = jnp.concatenate([kd_ref[h], kd_ref[h]], axis=1)
            o = _bdot(_bdot_nt(qr, kr) * dec_ref[h], v) + _bdot(qr * qdec, s_ref[h])
            s_ref[h] = s_ref[h] * cd_ref[h] + _bdot_tn(kr * kdec, v)
            on = o * lax.rsqrt(jnp.mean(o * o, axis=1, keepdims=True) + NORM_EPS)
            o_ref[0, rs, vs] = (_silu(gate_ref[0, rs, vs]) * on).astype(o_ref.dtype)


def _ret_core(y, positions):
    bsz, l, _ = y.shape
    c = RET_CHUNK
    inv_freq = ROPE_THETA ** (-jnp.arange(0, RET_QK_HEAD, 2, dtype=F32) / RET_QK_HEAD)
    ang = positions.astype(F32)[..., None] * inv_freq
    cos, sin = jnp.cos(ang), jnp.sin(ang)
    log_gamma = jnp.log1p(-jnp.exp2(-5.0 - jnp.arange(RET_HEADS, dtype=F32)))
    idx = jnp.arange(c, dtype=F32)
    rel = idx[:, None] - idx[None, :]
    inner_decay = jnp.where(rel >= 0, jnp.exp(log_gamma[:, None, None] * jnp.maximum(rel, 0.0)), 0.0)
    lanes = lambda t: jnp.broadcast_to(t[:, :, None], (RET_HEADS, c, LANES))
    q_decay = lanes(jnp.exp(log_gamma[:, None] * (idx[None, :] + 1.0)))
    k_decay = lanes(jnp.exp(log_gamma[:, None] * (c - 1.0 - idx[None, :])))
    chunk_decay = jnp.exp(log_gamma * c)
    t = RET_STEP
    full = lambda shape: pl.BlockSpec(shape, lambda b, i: (0,) * len(shape))
    return pl.pallas_call(
        _ret_kernel,
        out_shape=jax.ShapeDtypeStruct((bsz, l, RET_V_DIM), BF16),
        grid=(bsz, l // t),
        in_specs=[pl.BlockSpec((1, t, RET_QK_DIM), lambda b, i: (b, i, 0)),
                  pl.BlockSpec((1, t, RET_QK_DIM), lambda b, i: (b, i, 1)),
                  pl.BlockSpec((1, t, RET_V_DIM), lambda b, i: (b, i, 1)),
                  pl.BlockSpec((1, t, RET_V_DIM), lambda b, i: (b, i, 2)),
                  pl.BlockSpec((1, t, LANES), lambda b, i: (b, i, 0)),
                  pl.BlockSpec((1, t, LANES), lambda b, i: (b, i, 0)),
                  full((RET_HEADS, c, c)), full((RET_HEADS, c, LANES)), full((RET_HEADS, c, LANES)),
                  pl.BlockSpec(memory_space=pltpu.SMEM)],
        out_specs=pl.BlockSpec((1, t, RET_V_DIM), lambda b, i: (b, i, 0)),
        scratch_shapes=[pltpu.VMEM((RET_HEADS, RET_QK_HEAD, RET_V_HEAD), F32)],
        compiler_params=_ARB2,
        name="ret_core",
    )(y, y, y, y, cos, sin, inner_decay, q_decay, k_decay, chunk_decay)


def _retention_mixer(u, positions, w_in, w_out, layer, res):
    bsz, l, dm = u.shape
    y = _proj(u.reshape(bsz * l, dm), w_in, layer)
    o = _ret_core(y.reshape(bsz, l, -1), positions)
    return _proj(o.reshape(bsz * l, RET_V_DIM), w_out, layer, mode="residual", res=res)


def _rope_tables(positions, dim):
    inv_freq = ROPE_THETA ** (-jnp.arange(0, dim, 2, dtype=F32) / dim)
    ang = positions.astype(F32)[..., None] * inv_freq
    c, s = jnp.cos(ang), jnp.sin(ang)
    reps = LANES // dim
    cos_full = jnp.tile(jnp.concatenate([c, c], axis=-1), (1, 1, reps))
    sin_signed = jnp.tile(jnp.concatenate([-s, s], axis=-1), (1, 1, reps))
    return cos_full, sin_signed


def _rope128(x, cos, sin_signed):
    return x * cos + pltpu.roll(x, 64, 1) * sin_signed


def _rope64(x, cos, sin_signed, low_half):
    partner = jnp.where(low_half, pltpu.roll(x, 96, 1), pltpu.roll(x, 32, 1))
    return x * cos + partner * sin_signed


def _dsa_prep_kernel(k_ref, v_ref, ki_ref, ck_ref, sk_ref, cik_ref, sik_ref, kb_ref, vb_ref, ki2_ref):
    ck, sk = ck_ref[0], sk_ref[0]
    for g in range(DSA_KV_HEADS):
        sl = slice(g * DSA_HEAD_DIM, (g + 1) * DSA_HEAD_DIM)
        kb_ref[0, g] = _rope128(k_ref[0, :, sl], ck, sk).astype(BF16)
        vb_ref[0, g] = v_ref[0, :, sl].astype(BF16)
    x = ki_ref[0]
    lane = _lane_iota(x.shape)
    xr = _rope64(x, cik_ref[0], sik_ref[0], (lane % IDX_HEAD_DIM) < IDX_HEAD_DIM // 2)
    ki2_ref[0] = jnp.where(lane < IDX_HEAD_DIM, xr, pltpu.roll(xr, IDX_HEAD_DIM, 1)).astype(BF16)


def _dsa_block(s_eff, q_ref, qi_ref, wi_ref, cq_ref, sq_ref, ciq_ref, siq_ref, kb_ref, vb_ref, ki2_ref, o_ref,
               key_ref, *, topk):
    t_blk = DSA_BLOCK
    qb = pl.program_id(1)
    lane = _lane_iota((t_blk, LANES))
    low_half = (lane % IDX_HEAD_DIM) < IDX_HEAD_DIM // 2
    lo64 = lane < IDX_HEAD_DIM

    wi = wi_ref[0] * (IDX_HEADS ** -0.5 * IDX_HEAD_DIM ** -0.5)
    ciq, siq = ciq_ref[0], siq_ref[0]
    ki2 = ki2_ref[0, :s_eff, :]
    score = jnp.zeros((t_blk, s_eff), F32)
    for quad in range(IDX_HEADS // 4):
        rows = []
        for pr in (2 * quad, 2 * quad + 1):
            xr = _rope64(qi_ref[0, :, pr * LANES:(pr + 1) * LANES], ciq, siq, low_half)
            rows.append(jnp.where(lo64, xr, 0.0))
            rows.append(jnp.where(lo64, 0.0, xr))
        s4 = _bdot_nt(jnp.concatenate(rows, axis=0), ki2)
        for j in range(4):
            hh = 4 * quad + j
            w_col = wi[:, IDX_HEAD_DIM + hh:IDX_HEAD_DIM + hh + 1]
            score = score + w_col * jnp.maximum(s4[j * t_blk:(j + 1) * t_blk], 0.0)

    t_idx = qb * t_blk + lax.broadcasted_iota(jnp.int32, (t_blk, 1), 0)
    s_idx = _lane_iota((t_blk, s_eff))
    score = jnp.where(s_idx <= t_idx, score, -jnp.inf)
    bits = lax.bitcast_convert_type(score, jnp.int32)
    key_ref[:, :s_eff] = bits ^ ((bits >> 31) & jnp.int32(0x7FFFFFFF))

    kk = jnp.minimum(t_idx + 1, topk).astype(F32)
    int_min = jnp.int32(-2 ** 31)

    def bisect(i, t_u):
        cand_u = t_u | lax.shift_left(jnp.int32(1), 31 - i)
        hit = key_ref[:, :s_eff] >= (cand_u ^ int_min)
        cnt = jnp.sum(jnp.where(hit, 1.0, 0.0), axis=1, keepdims=True)
        return jnp.where(cnt >= kk, cand_u, t_u)

    t_u = lax.fori_loop(0, 32, bisect, jnp.zeros((t_blk, 1), jnp.int32))
    sel = key_ref[:, :s_eff] >= (t_u ^ int_min)

    cq, sq = cq_ref[0], sq_ref[0]
    group = DSA_HEADS // DSA_KV_HEADS
    def group_logits(g):
        qs = []
        for j in range(group):
            hh = group * g + j
            qs.append(_rope128(q_ref[0, :, hh * DSA_HEAD_DIM:(hh + 1) * DSA_HEAD_DIM], cq, sq))
        return _bdot_nt(jnp.concatenate(qs, axis=0), kb_ref[0, g, :s_eff, :]) * DSA_HEAD_DIM ** -0.5

    next_logits = group_logits(0)
    for g in range(DSA_KV_HEADS):
        logits = next_logits
        if g + 1 < DSA_KV_HEADS:
            next_logits = group_logits(g + 1)
        ps, dens = [], []
        for j in range(group):
            lj = jnp.where(sel, logits[j * t_blk:(j + 1) * t_blk], -jnp.inf)
            pj = jnp.exp(lj - jnp.max(lj, axis=1, keepdims=True))
            dens.append(jnp.sum(pj, axis=1, keepdims=True))
            ps.append(pj)
        pv = _bdot(jnp.concatenate(ps, axis=0), vb_ref[0, g, :s_eff, :])
        for j in range(group):
            hh = group * g + j
            o_ref[0, :, hh * DSA_HEAD_DIM:(hh + 1) * DSA_HEAD_DIM] = (
                pv[j * t_blk:(j + 1) * t_blk] / dens[j]).astype(o_ref.dtype)


def _dsa_kernel(q_ref, qi_ref, wi_ref, cq_ref, sq_ref, ciq_ref, siq_ref, kb_ref, vb_ref, ki2_ref, o_ref, key_ref,
                *, topk, seq):
    qb = pl.program_id(1)
    blocks_per_step = DSA_KV_STEP // DSA_BLOCK
    for br in range(seq // DSA_KV_STEP):
        @pl.when(qb // blocks_per_step == br)
        def _():
            _dsa_block((br + 1) * DSA_KV_STEP, q_ref, qi_ref, wi_ref, cq_ref, sq_ref, ciq_ref, siq_ref, kb_ref,
                       vb_ref, ki2_ref, o_ref, key_ref, topk=topk)


def _dsa_core(y, yx, positions):
    bsz, l, _ = y.shape
    topk = min(DSA_TOPK_MAX, l // 4)
    cos_q, sin_q = _rope_tables(positions, DSA_HEAD_DIM)
    cos_i, sin_i = _rope_tables(positions, IDX_HEAD_DIM)
    k_blk = DSA_Q_DIM // DSA_KV_DIM
    ki_blk = 0
    ps = DSA_KV_STEP
    tab = pl.BlockSpec((1, ps, LANES), lambda b, i: (b, i, 0))
    kb, vb, ki2 = pl.pallas_call(
        _dsa_prep_kernel,
        out_shape=(jax.ShapeDtypeStruct((bsz, DSA_KV_HEADS, l, DSA_HEAD_DIM), BF16),
                   jax.ShapeDtypeStruct((bsz, DSA_KV_HEADS, l, DSA_HEAD_DIM), BF16),
                   jax.ShapeDtypeStruct((bsz, l, LANES), BF16)),
        grid=(bsz, l // ps),
        in_specs=[pl.BlockSpec((1, ps, DSA_KV_DIM), lambda b, i: (b, i, k_blk)),
                  pl.BlockSpec((1, ps, DSA_KV_DIM), lambda b, i: (b, i, k_blk + 1)),
                  pl.BlockSpec((1, ps, LANES), lambda b, i: (b, i, ki_blk)),
                  tab, tab, tab, tab],
        out_specs=(pl.BlockSpec((1, DSA_KV_HEADS, ps, DSA_HEAD_DIM), lambda b, i: (b, 0, i, 0)),
                   pl.BlockSpec((1, DSA_KV_HEADS, ps, DSA_HEAD_DIM), lambda b, i: (b, 0, i, 0)),
                   pl.BlockSpec((1, ps, LANES), lambda b, i: (b, i, 0))),
        compiler_params=_ARB2,
        name="dsa_prep",
    )(y, y, yx, cos_q, sin_q, cos_i, sin_i)

    t_blk = DSA_BLOCK
    qtab = pl.BlockSpec((1, t_blk, LANES), lambda b, i: (b, i, 0))
    kv_spec = pl.BlockSpec((1, DSA_KV_HEADS, l, DSA_HEAD_DIM), lambda b, i: (b, 0, 0, 0))
    return pl.pallas_call(
        functools.partial(_dsa_kernel, topk=topk, seq=l),
        out_shape=jax.ShapeDtypeStruct((bsz, l, DSA_Q_DIM), BF16),
        grid=(bsz, l // t_blk),
        in_specs=[pl.BlockSpec((1, t_blk, DSA_Q_DIM), lambda b, i: (b, i, 0)),
                  pl.BlockSpec((1, t_blk, IDX_HEADS * IDX_HEAD_DIM),
                               lambda b, i: (b, i, (DSA_Q_DIM + 2 * DSA_KV_DIM) // (IDX_HEADS * IDX_HEAD_DIM))),
                  pl.BlockSpec((1, t_blk, LANES), lambda b, i: (b, i, ki_blk)),
                  qtab, qtab, qtab, qtab,
                  kv_spec, kv_spec,
                  pl.BlockSpec((1, l, LANES), lambda b, i: (b, 0, 0))],
        out_specs=pl.BlockSpec((1, t_blk, DSA_Q_DIM), lambda b, i: (b, i, 0)),
        scratch_shapes=[pltpu.VMEM((t_blk, l), jnp.int32)],
        compiler_params=_ARB2,
        name="dsa_attn",
    )(y, y, yx, cos_q, sin_q, cos_i, sin_i, kb, vb, ki2)


def _dsa_mixer(u, positions, w_in, w_out, layer, res):
    bsz, l, dm = u.shape
    u2 = u.reshape(bsz * l, dm)
    n_main = DSA_Q_DIM + 2 * DSA_KV_DIM + IDX_HEADS * IDX_HEAD_DIM
    y = _proj(u2, w_in, layer, n_out=n_main)
    yx = _proj_tail(u2, w_in, layer, n_main)
    o = _dsa_core(y.reshape(bsz, l, -1), yx.reshape(bsz, l, -1), positions)
    return _proj(o.reshape(bsz * l, DSA_Q_DIM), w_out, layer, mode="residual", res=res)


def kernel(x, p, positions, norm_mix, norm_mlp, norm_ple, norm_final, gdn_w_in, gdn_conv_w, gdn_a_log,
           gdn_dt_bias, gdn_norm, gdn_w_out, ret_w_in, ret_w_out, dsa_w_in, dsa_w_out, mlp_w_up, mlp_w_down,
           ple_w_gate, ple_w_proj):
    bsz, l, d = x.shape
    m = bsz * l
    h = x.reshape(m, d)
    for i in range(DEPTH):
        kind, j = i % N_MIXERS, i // N_MIXERS
        u = _rmsnorm(h, norm_mix[i], BF16).reshape(bsz, l, d)
        if kind == 0:
            h = _gdn_mixer(u, gdn_w_in, gdn_conv_w, j, gdn_a_log[j], gdn_dt_bias[j], gdn_norm[j], gdn_w_out, h)
        elif kind == 1:
            h = _retention_mixer(u, positions, ret_w_in, ret_w_out, j, h)
        else:
            h = _dsa_mixer(u, positions, dsa_w_in, dsa_w_out, j, h)
        u = _rmsnorm(h, norm_mlp[i], BF16)
        a = _proj(u, mlp_w_up, i, mode="relu2", out_dtype=BF16)
        h = _proj(a, mlp_w_down, i, mode="residual", res=h)
        u = _rmsnorm(h, norm_ple[i], BF16)
        h = _proj(u, ple_w_gate, i, mode="ple", res=h, p=p.reshape(DEPTH, m, -1), wp=ple_w_proj)
    return _rmsnorm(h, norm_final, x.dtype).reshape(bsz, l, d)
```

```python
import functools

import jax
import jax.numpy as jnp
from jax import lax
from jax.experimental import pallas as pl
from jax.experimental.pallas import tpu as pltpu

F32 = jnp.float32
BF16 = jnp.bfloat16

D_MODEL = 2048
DEPTH = 4
N_MIXERS = 3
NORM_EPS = 1e-6
ROPE_THETA = 10000.0

GDN_HEAD_DIM = 128
GDN_QK_HEADS = D_MODEL // 128
GDN_V_HEADS = 2 * GDN_QK_HEADS
GDN_CONV = 4
GDN_CHUNK = 64
GDN_KEY_DIM = GDN_QK_HEADS * GDN_HEAD_DIM
GDN_VAL_DIM = GDN_V_HEADS * GDN_HEAD_DIM
GDN_CONV_DIM = 2 * GDN_KEY_DIM + GDN_VAL_DIM
GDN_SUPER = 256
GDN_STEP_QK = 2

RET_HEADS = 8
RET_QK_DIM = D_MODEL
RET_V_DIM = 2 * D_MODEL
RET_QK_HEAD = RET_QK_DIM // RET_HEADS
RET_V_HEAD = RET_V_DIM // RET_HEADS
RET_CHUNK = 128
RET_STEP = 256

DSA_HEAD_DIM = 128
DSA_HEADS = D_MODEL // DSA_HEAD_DIM
DSA_KV_HEADS = 4
IDX_HEADS = 16
IDX_HEAD_DIM = 64
DSA_TOPK_MAX = 256
DSA_BLOCK = 128
DSA_KV_STEP = 512
DSA_Q_DIM = DSA_HEADS * DSA_HEAD_DIM
DSA_KV_DIM = DSA_KV_HEADS * DSA_HEAD_DIM

LANES = 128
SUBLANES = 8
VMEM_LIMIT_BYTES = 56 * 1024 * 1024

_ARB2 = pltpu.CompilerParams(dimension_semantics=("arbitrary", "arbitrary"), vmem_limit_bytes=VMEM_LIMIT_BYTES)


def _bdot(a, b):
    return jnp.dot(a.astype(BF16), b.astype(BF16), preferred_element_type=F32)


def _bdot_nt(a, b):
    return lax.dot_general(a.astype(BF16), b.astype(BF16), (((1,), (1,)), ((), ())), preferred_element_type=F32)


def _bdot_tn(a, b):
    return lax.dot_general(a.astype(BF16), b.astype(BF16), (((0,), (0,)), ((), ())), preferred_element_type=F32)


def _lane_iota(shape):
    return lax.broadcasted_iota(jnp.int32, shape, 1)


def _silu(x):
    return x * jax.nn.sigmoid(x)


def _mm_kernel(*refs, nk, mode, valid, w_t):
    x_ref, w_ref, o_ref = refs[0], refs[1], refs[-1]
    extra = refs[2:-1]
    if w_t:
        w = w_ref[...]
        if mode == "tail":
            w = jnp.where(lax.broadcasted_iota(jnp.int32, w.shape, 0) < valid, w, 0.0)
        o_ref[...] = _bdot_nt(x_ref[...], w)
        return
    if nk > 1:
        @pl.when(pl.program_id(2) == 0)
        def _():
            o_ref[...] = extra[0][...]

        half = o_ref.shape[1] // 2
        for cs in (slice(0, half), slice(half, 2 * half)):
            o_ref[:, cs] += _bdot(x_ref[...], w_ref[:, cs])
        return
    if mode == "tail":
        w = jnp.where(_lane_iota(w_ref.shape) < valid, w_ref[...], 0.0)
        o_ref[...] = _bdot(x_ref[...], w)
        return
    prod = _bdot(x_ref[...], w_ref[...])
    if mode == "plain":
        o_ref[...] = prod.astype(o_ref.dtype)
    elif mode == "relu2":
        o_ref[...] = jnp.square(jnp.maximum(prod, 0.0)).astype(o_ref.dtype)
    elif mode == "ple":
        res_ref, p_ref, wp_ref = extra
        o_ref[...] = res_ref[...] + jax.nn.sigmoid(prod) * _bdot(p_ref[...], wp_ref[...])
    else:
        o_ref[...] = extra[0][...] + prod


def _matmul(x, w, layer, *, n_out, bm, bn, bk, n_outer, col0=0, mode="plain", out_dtype=F32, res=None, p=None,
            wp=None, w_t=False):
    m, kdim = x.shape
    bm = min(bm, m)
    n_total = w.shape[1] if w_t else w.shape[2]
    assert m % bm == 0 and n_out % bn == 0 and col0 % bn == 0 and kdim % bk == 0, (x.shape, w.shape, bm, bn, bk)
    assert col0 + n_out <= n_total or (mode == "tail" and n_out == bn)
    nk = kdim // bk
    assert nk == 1 or mode == "residual"
    assert not w_t or mode in ("plain", "tail")
    cb0 = col0 // bn
    if n_outer:
        grid = (n_out // bn, m // bm, nk)
        mi, ni = (lambda j, i, k: i), (lambda j, i, k: j)
    else:
        grid = (m // bm, n_out // bn, nk)
        mi, ni = (lambda i, j, k: i), (lambda i, j, k: j)
    in_specs = [pl.BlockSpec((bm, bk), lambda a, b, k: (mi(a, b, k), k)),
                pl.BlockSpec((None, bn, bk), lambda a, b, k: (layer, cb0 + ni(a, b, k), k)) if w_t else
                pl.BlockSpec((None, bk, bn), lambda a, b, k: (layer, k, cb0 + ni(a, b, k)))]
    args = [x, w]
    o_map = lambda a, b, k: (mi(a, b, k), ni(a, b, k))
    if mode in ("residual", "ple"):
        in_specs.append(pl.BlockSpec((bm, bn), o_map))
        args.append(res)
    if mode == "ple":
        kp = p.shape[2]
        in_specs.append(pl.BlockSpec((None, bm, kp), lambda a, b, k: (layer, mi(a, b, k), 0)))
        in_specs.append(pl.BlockSpec((None, kp, bn), lambda a, b, k: (layer, 0, ni(a, b, k))))
        args += [p, wp]
    return pl.pallas_call(
        functools.partial(_mm_kernel, nk=nk, mode=mode, valid=n_total - col0, w_t=w_t),
        out_shape=jax.ShapeDtypeStruct((m, n_out), out_dtype),
        grid=grid,
        in_specs=in_specs,
        out_specs=pl.BlockSpec((bm, bn), o_map),
        compiler_params=pltpu.CompilerParams(
            dimension_semantics=("arbitrary", "arbitrary", "arbitrary"), vmem_limit_bytes=VMEM_LIMIT_BYTES),
        name="matmul_" + mode,
    )(*args)


def _proj(x, w, layer, n_out=None, **kw):
    w_t = kw.get("w_t", False)
    kdim = w.shape[2] if w_t else w.shape[1]
    n_out = (w.shape[1] if w_t else w.shape[2]) if n_out is None else n_out
    if kdim <= 2048:
        bm = 512 if kw.get("mode") == "ple" else 1024
        return _matmul(x, w, layer, n_out=n_out, bm=bm, bn=min(n_out, 1024), bk=kdim, n_outer=True, **kw)
    return _matmul(x, w, layer, n_out=n_out, bm=2048, bn=1024, bk=1024, n_outer=False, **kw)


def _proj_tail(x, w_t, layer, start):
    return _matmul(x, w_t, layer, n_out=LANES, col0=start, bm=1024, bn=LANES, bk=w_t.shape[2], n_outer=True,
                   mode="tail", w_t=True)


def _transposed(w):
    return jnp.swapaxes(w, 1, 2)


def _rmsnorm_kernel(x_ref, g_ref, o_ref):
    x = x_ref[...]
    y = x * lax.rsqrt(jnp.mean(x * x, axis=1, keepdims=True) + NORM_EPS) * g_ref[...]
    o_ref[...] = y.astype(o_ref.dtype)


def _rmsnorm(x, gain, out_dtype):
    m, d = x.shape
    bm = min(m, 512)
    return pl.pallas_call(
        _rmsnorm_kernel,
        out_shape=jax.ShapeDtypeStruct((m, d), out_dtype),
        grid=(m // bm,),
        in_specs=[pl.BlockSpec((bm, d), lambda i: (i, 0)), pl.BlockSpec((1, d), lambda i: (0, 0))],
        out_specs=pl.BlockSpec((bm, d), lambda i: (i, 0)),
        compiler_params=pltpu.CompilerParams(dimension_semantics=("arbitrary",), vmem_limit_bytes=VMEM_LIMIT_BYTES),
        name="rmsnorm",
    )(x, gain.reshape(1, d))


def _gdn_gates_kernel(gb_ref, alog_ref, dtb_ref, o_ref):
    x = gb_ref[0]
    lane = _lane_iota(x.shape)
    row = lax.broadcasted_iota(jnp.int32, x.shape, 0) % GDN_CHUNK
    g = -jnp.exp(alog_ref[...]) * jax.nn.softplus(x + dtb_ref[...])
    sh = 1
    while sh < GDN_CHUNK:
        g = g + jnp.where(row >= sh, pltpu.roll(g, sh, 0), 0.0)
        sh *= 2
    o_ref[0] = jnp.where(lane < GDN_V_HEADS, jax.nn.sigmoid(x), g)


def _conv_silu_tile(x, tail, w):
    rows8 = lax.broadcasted_iota(jnp.int32, tail.shape, 0)
    acc = x * w[GDN_CONV - 1:GDN_CONV]
    for sh in range(1, GDN_CONV):
        xs = pltpu.roll(x, sh, 0)
        head = jnp.where(rows8 < sh, pltpu.roll(tail, sh, 0), xs[:SUBLANES])
        xs = jnp.concatenate([head, xs[SUBLANES:]], axis=0)
        acc = acc + xs * w[GDN_CONV - 1 - sh:GDN_CONV - sh]
    return _silu(acc)


def _gdn_kernel(q_ref, k_ref, v_ref, z_ref, gt_ref, wq_ref, wk_ref, wv_ref, nw_ref, o_ref,
                s_ref, kn_st, wk_st, qg_st, gl_st, in_st, *, seq):
    hq = pl.program_id(1)
    r_blk, c_blk, d = GDN_SUPER, GDN_CHUNK, GDN_HEAD_DIM
    ii = lax.broadcasted_iota(jnp.int32, (r_blk, r_blk), 0)
    jj = lax.broadcasted_iota(jnp.int32, (r_blk, r_blk), 1)
    blk_masks = [(ii >> sh) == (jj >> sh) for sh in (3, 4, 5, 6)]
    blk8 = blk_masks[0]
    tri_incl = blk_masks[3] & (ii >= jj)
    diag = ii == jj
    eye = jnp.where(diag, 1.0, 0.0)
    lane = _lane_iota((r_blk, LANES))
    s_ref[...] = jnp.zeros_like(s_ref)
    wq, wk, wv, nw = wq_ref[...], wk_ref[...], wv_ref[...], nw_ref[...]
    n_steps = seq // r_blk
    rep = GDN_V_HEADS // GDN_QK_HEADS

    def fetch(s):
        r0 = pl.multiple_of(s * r_blk, r_blk)
        rt = pl.multiple_of(jnp.maximum(r0 - SUBLANES, 0), SUBLANES)
        tail = lambda ref: jnp.where(s > 0, ref[0, pl.ds(rt, SUBLANES), :], 0.0)
        rows = pl.ds(r0, r_blk)
        return (q_ref[0, rows, :], tail(q_ref), k_ref[0, rows, :], tail(k_ref), v_ref[0, rows, :], tail(v_ref),
                gt_ref[0, rows, :])

    heads = range(GDN_STEP_QK * rep)
    qk_of = lambda j: j // rep

    def solve(raw, slot):
        xq, tq, xk, tk, xv, tv, gt = raw
        qc, kc, vc = _conv_silu_tile(xq, tq, wq), _conv_silu_tile(xk, tk, wk), _conv_silu_tile(xv, tv, wv)
        qn, kn = [], []
        for i in range(GDN_STEP_QK):
            qi, ki = qc[:, i * d:(i + 1) * d], kc[:, i * d:(i + 1) * d]
            qn.append(qi * lax.rsqrt(jnp.sum(qi * qi, axis=1, keepdims=True) + NORM_EPS) * d ** -0.5)
            kn.append(ki * lax.rsqrt(jnp.sum(ki * ki, axis=1, keepdims=True) + NORM_EPS))
            kn_st[slot, i] = kn[i]
        hv0 = len(heads) * hq
        beta = [jnp.sum(jnp.where(lane == hv0 + j, gt, 0.0), axis=1, keepdims=True) for j in heads]
        g = [jnp.sum(jnp.where(lane == GDN_V_HEADS + hv0 + j, gt, 0.0), axis=1, keepdims=True) for j in heads]
        gi = [jnp.broadcast_to(g[j], (r_blk, r_blk)) for j in heads]
        dec = [jnp.exp(jnp.where(tri_incl, gi[j] - jnp.sum(jnp.where(diag, gi[j], 0.0), axis=0, keepdims=True),
                                 -jnp.inf)) for j in heads]
        kb = [kn[qk_of(j)] * beta[j] for j in heads]
        for j in heads:
            qg_st[slot, j] = qn[qk_of(j)] * jnp.exp(g[j])
            gl_st[slot, j] = gi[j][:, :d]
        qk = [_bdot_nt(qn[i], kn[i]) for i in range(GDN_STEP_QK)]
        kk = [_bdot_nt(kb[j], kn[qk_of(j)]) for j in heads]
        yield
        for j in heads:
            in_st[slot, j] = qk[qk_of(j)] * dec[j]
        a = [jnp.where(diag, 0.0, kk[j] * dec[j]) for j in heads]
        a8 = [jnp.where(blk8, a[j], 0.0) for j in heads]
        a8_2 = [_bdot(a8[j], a8[j]) for j in heads]
        yield
        inv = [eye - a8[j] for j in heads]
        t = [_bdot(inv[j], a8_2[j]) for j in heads]
        a8_4 = [_bdot(a8_2[j], a8_2[j]) for j in heads]
        yield
        inv = [inv[j] + t[j] for j in heads]
        t = [_bdot(inv[j], a8_4[j]) for j in heads]
        yield
        inv = [inv[j] + t[j] for j in heads]
        for lvl in range(3):
            off = [jnp.where(blk_masks[lvl + 1] & ~blk_masks[lvl], a[j], 0.0) for j in heads]
            t = [_bdot(off[j], inv[j]) for j in heads]
            yield
            t = [_bdot(inv[j], t[j]) for j in heads]
            yield
            inv = [inv[j] - t[j] for j in heads]
        rhs = [jnp.concatenate([vc[:, j * d:(j + 1) * d] * beta[j], kb[j] * jnp.exp(g[j])], axis=1) for j in heads]
        x = [_bdot(inv[j], rhs[j]) for j in heads]
        yield
        for j in heads:
            wk_st[slot, j] = x[j]

    def recur(s, slot):
        r0 = pl.multiple_of(s * r_blk, r_blk)
        state = [s_ref[j] for j in heads]
        for c in range(r_blk // c_blk):
            rs = slice(c * c_blk, (c + 1) * c_blk)
            gl = [gl_st[slot, j, rs, :] for j in heads]
            g_last = [gl[j][c_blk - 1:c_blk, :] for j in heads]
            both = [_bdot(jnp.concatenate([wk_st[slot, j, rs, d:], qg_st[slot, j, rs, :]], axis=0), state[j])
                    for j in heads]
            yield
            v_new = [wk_st[slot, j, rs, :d] - both[j][:c_blk] for j in heads]
            kd = [kn_st[slot, qk_of(j), rs, :] * jnp.exp(g_last[j] - gl[j]) for j in heads]
            upd = [_bdot_tn(kd[j], v_new[j]) for j in heads]
            pad = []
            for j in heads:
                parts = []
                if c > 0:
                    parts.append(jnp.zeros((c * c_blk, d), F32))
                parts.append(v_new[j])
                if (c + 1) * c_blk < r_blk:
                    parts.append(jnp.zeros((r_blk - (c + 1) * c_blk, d), F32))
                pad.append(jnp.concatenate(parts, axis=0))
            o_in = [_bdot(in_st[slot, j, rs, :], pad[j]) for j in heads]
            yield
            state = [state[j] * jnp.exp(g_last[j]) + upd[j] for j in heads]
            rows = pl.ds(r0 + c * c_blk, c_blk)
            for j in heads:
                o_c = both[j][c_blk:] + o_in[j]
                on = o_c * lax.rsqrt(jnp.mean(o_c * o_c, axis=1, keepdims=True) + NORM_EPS) * nw
                zc = z_ref[0, rows, j * d:(j + 1) * d]
                o_ref[0, rows, j * d:(j + 1) * d] = (on * _silu(zc)).astype(o_ref.dtype)
        for j in heads:
            s_ref[j] = state[j]

    def run(*gens):
        live = list(gens)
        while live:
            live = [gen for gen in live if next(gen, StopIteration) is not StopIteration]

    run(solve(fetch(0), 0))

    def body(i, carry):
        s = 2 * i
        run(solve(fetch(s + 1), 1), recur(s, 0))
        run(solve(fetch(s + 2), 0), recur(s + 1, 1))
        return carry

    lax.fori_loop(0, n_steps // 2 - 1, body, 0)
    run(solve(fetch(n_steps - 1), 1), recur(n_steps - 2, 0))
    run(recur(n_steps - 1, 1))


def _gdn_core(y, yg, conv_w, layer, a_log, dt_bias, norm_w):
    bsz, l, _ = y.shape
    assert l % (2 * GDN_SUPER) == 0
    nh, d = GDN_QK_HEADS, GDN_HEAD_DIM
    rep = GDN_V_HEADS // GDN_QK_HEADS
    lane_vec = lambda t: jnp.zeros((1, LANES), F32).at[0, GDN_V_HEADS:2 * GDN_V_HEADS].set(t.astype(F32))
    gates = pl.pallas_call(
        _gdn_gates_kernel,
        out_shape=jax.ShapeDtypeStruct((bsz, l, LANES), F32),
        grid=(bsz,),
        in_specs=[pl.BlockSpec((1, l, LANES), lambda b: (b, 0, 0)),
                  pl.BlockSpec((1, LANES), lambda b: (0, 0)), pl.BlockSpec((1, LANES), lambda b: (0, 0))],
        out_specs=pl.BlockSpec((1, l, LANES), lambda b: (b, 0, 0)),
        compiler_params=pltpu.CompilerParams(dimension_semantics=("arbitrary",), vmem_limit_bytes=VMEM_LIMIT_BYTES),
        name="gdn_gates",
    )(yg, lane_vec(a_log), lane_vec(dt_bias))
    r_blk = GDN_SUPER
    nq = GDN_STEP_QK
    qw, vw = nq * d, nq * rep * d
    nv = nq * rep
    k0, v0, z0 = GDN_KEY_DIM // qw, 2 * GDN_KEY_DIM // vw, GDN_CONV_DIM // vw
    return pl.pallas_call(
        functools.partial(_gdn_kernel, seq=l),
        out_shape=jax.ShapeDtypeStruct((bsz, l, GDN_VAL_DIM), BF16),
        grid=(bsz, nh // nq),
        in_specs=[pl.BlockSpec((1, l, qw), lambda b, h: (b, 0, h)),
                  pl.BlockSpec((1, l, qw), lambda b, h: (b, 0, k0 + h)),
                  pl.BlockSpec((1, l, vw), lambda b, h: (b, 0, v0 + h)),
                  pl.BlockSpec((1, l, vw), lambda b, h: (b, 0, z0 + h)),
                  pl.BlockSpec((1, l, LANES), lambda b, h: (b, 0, 0)),
                  pl.BlockSpec((None, GDN_CONV, qw), lambda b, h: (layer, 0, h)),
                  pl.BlockSpec((None, GDN_CONV, qw), lambda b, h: (layer, 0, k0 + h)),
                  pl.BlockSpec((None, GDN_CONV, vw), lambda b, h: (layer, 0, v0 + h)),
                  pl.BlockSpec((1, d), lambda b, h: (0, 0))],
        out_specs=pl.BlockSpec((1, l, vw), lambda b, h: (b, 0, h)),
        scratch_shapes=[pltpu.VMEM((nv, d, d), F32),
                        pltpu.VMEM((2, nq, r_blk, d), F32),
                        pltpu.VMEM((2, nv, r_blk, 2 * d), F32),
                        pltpu.VMEM((2, nv, r_blk, d), F32),
                        pltpu.VMEM((2, nv, r_blk, d), F32),
                        pltpu.VMEM((2, nv, r_blk, r_blk), F32)],
        compiler_params=_ARB2,
        name="gdn_core",
    )(y, y, y, y, gates, conv_w, conv_w, conv_w, norm_w.reshape(1, d))


def _gdn_mixer(u, w_in, conv_w, layer, a_log, dt_bias, norm_w, w_out, res):
    bsz, l, dm = u.shape
    u2 = u.reshape(bsz * l, dm)
    n_main = GDN_CONV_DIM + GDN_VAL_DIM
    w_in_t = _transposed(w_in)
    y = _proj(u2, w_in_t, layer, n_out=n_main, w_t=True)
    yg = _proj_tail(u2, w_in_t, layer, n_main)
    o = _gdn_core(y.reshape(bsz, l, -1), yg.reshape(bsz, l, -1), conv_w, layer, a_log, dt_bias, norm_w)
    return _proj(o.reshape(bsz * l, GDN_VAL_DIM), w_out, layer, mode="residual", res=res)


def _ret_kernel(q_ref, k_ref, v_ref, gate_ref, cos_ref, sin_ref, dec_ref, qd_ref, kd_ref, cd_ref, o_ref, s_ref):
    @pl.when(pl.program_id(1) == 0)
    def _():
        s_ref[...] = jnp.zeros_like(s_ref)

    half = RET_QK_HEAD // 2

    def rope(ref, rs, h, c, s):
        x1 = ref[0, rs, h * RET_QK_HEAD:h * RET_QK_HEAD + half]
        x2 = ref[0, rs, h * RET_QK_HEAD + half:(h + 1) * RET_QK_HEAD]
        return jnp.concatenate([x1 * c - x2 * s, x2 * c + x1 * s], axis=1)

    heads = range(RET_HEADS)
    vs = [slice(h * RET_V_HEAD, (h + 1) * RET_V_HEAD) for h in heads]
    for c in range(RET_STEP // RET_CHUNK):
        rs = slice(c * RET_CHUNK, (c + 1) * RET_CHUNK)
        cos, sin = cos_ref[0, rs, :], sin_ref[0, rs, :]
        qr = [rope(q_ref, rs, h, cos, sin) for h in heads]
        kr = [rope(k_ref, rs, h, cos, sin) * RET_QK_HEAD ** -0.5 for h in heads]
        sc = [_bdot_nt(qr[h], kr[h]) for h in heads]
        cross = [_bdot(qr[h] * jnp.concatenate([qd_ref[h], qd_ref[h]], axis=1), s_ref[h]) for h in heads]
        upd = [_bdot_tn(kr[h] * jnp.concatenate([kd_ref[h], kd_ref[h]], axis=1), v_ref[0, rs, vs[h]]) for h in heads]
        inner = [_bdot(sc[h] * dec_ref[h], v_ref[0, rs, vs[h]]) for h in heads]
        for h in heads:
            s_ref[h] = s_ref[h] * cd_ref[h] + upd[h]
            o = inner[h] + cross[h]
            on = o * lax.rsqrt(jnp.mean(o * o, axis=1, keepdims=True) + NORM_EPS)
            o_ref[0, rs, vs[h]] = (_silu(gate_ref[0, rs, vs[h]]) * on).astype(o_ref.dtype)


def _ret_core(y, positions):
    bsz, l, _ = y.shape
    c = RET_CHUNK
    inv_freq = ROPE_THETA ** (-jnp.arange(0, RET_QK_HEAD, 2, dtype=F32) / RET_QK_HEAD)
    ang = positions.astype(F32)[..., None] * inv_freq
    cos, sin = jnp.cos(ang), jnp.sin(ang)
    log_gamma = jnp.log1p(-jnp.exp2(-5.0 - jnp.arange(RET_HEADS, dtype=F32)))
    idx = jnp.arange(c, dtype=F32)
    rel = idx[:, None] - idx[None, :]
    inner_decay = jnp.where(rel >= 0, jnp.exp(log_gamma[:, None, None] * jnp.maximum(rel, 0.0)), 0.0)
    lanes = lambda t: jnp.broadcast_to(t[:, :, None], (RET_HEADS, c, LANES))
    q_decay = lanes(jnp.exp(log_gamma[:, None] * (idx[None, :] + 1.0)))
    k_decay = lanes(jnp.exp(log_gamma[:, None] * (c - 1.0 - idx[None, :])))
    chunk_decay = jnp.exp(log_gamma * c)
    t = RET_STEP
    full = lambda shape: pl.BlockSpec(shape, lambda b, i: (0,) * len(shape))
    return pl.pallas_call(
        _ret_kernel,
        out_shape=jax.ShapeDtypeStruct((bsz, l, RET_V_DIM), BF16),
        grid=(bsz, l // t),
        in_specs=[pl.BlockSpec((1, t, RET_QK_DIM), lambda b, i: (b, i, 0)),
                  pl.BlockSpec((1, t, RET_QK_DIM), lambda b, i: (b, i, 1)),
                  pl.BlockSpec((1, t, RET_V_DIM), lambda b, i: (b, i, 1)),
                  pl.BlockSpec((1, t, RET_V_DIM), lambda b, i: (b, i, 2)),
                  pl.BlockSpec((1, t, LANES), lambda b, i: (b, i, 0)),
                  pl.BlockSpec((1, t, LANES), lambda b, i: (b, i, 0)),
                  full((RET_HEADS, c, c)), full((RET_HEADS, c, LANES)), full((RET_HEADS, c, LANES)),
                  pl.BlockSpec(memory_space=pltpu.SMEM)],
        out_specs=pl.BlockSpec((1, t, RET_V_DIM), lambda b, i: (b, i, 0)),
        scratch_shapes=[pltpu.VMEM((RET_HEADS, RET_QK_HEAD, RET_V_HEAD), F32)],
        compiler_params=_ARB2,
        name="ret_core",
    )(y, y, y, y, cos, sin, inner_decay, q_decay, k_decay, chunk_decay)


def _retention_mixer(u, positions, w_in, w_out, layer, res):
    bsz, l, dm = u.shape
    y = _proj(u.reshape(bsz * l, dm), w_in, layer)
    o = _ret_core(y.reshape(bsz, l, -1), positions)
    return _proj(o.reshape(bsz * l, RET_V_DIM), w_out, layer, mode="residual", res=res)


def _rope_tables(positions, dim):
    inv_freq = ROPE_THETA ** (-jnp.arange(0, dim, 2, dtype=F32) / dim)
    ang = positions.astype(F32)[..., None] * inv_freq
    c, s = jnp.cos(ang), jnp.sin(ang)
    reps = LANES // dim
    cos_full = jnp.tile(jnp.concatenate([c, c], axis=-1), (1, 1, reps))
    sin_signed = jnp.tile(jnp.concatenate([-s, s], axis=-1), (1, 1, reps))
    return cos_full, sin_signed


def _rope128(x, cos, sin_signed):
    return x * cos + pltpu.roll(x, 64, 1) * sin_signed


def _rope64(x, cos, sin_signed, low_half):
    partner = jnp.where(low_half, pltpu.roll(x, 96, 1), pltpu.roll(x, 32, 1))
    return x * cos + partner * sin_signed


def _dsa_prep_kernel(k_ref, v_ref, ki_ref, ck_ref, sk_ref, cik_ref, sik_ref, kb_ref, vb_ref, ki2_ref):
    ck, sk = ck_ref[0], sk_ref[0]
    for g in range(DSA_KV_HEADS):
        sl = slice(g * DSA_HEAD_DIM, (g + 1) * DSA_HEAD_DIM)
        kb_ref[0, g] = _rope128(k_ref[0, :, sl], ck, sk).astype(BF16)
        vb_ref[0, g] = v_ref[0, :, sl].astype(BF16)
    x = ki_ref[0]
    lane = _lane_iota(x.shape)
    xr = _rope64(x, cik_ref[0], sik_ref[0], (lane % IDX_HEAD_DIM) < IDX_HEAD_DIM // 2)
    ki2_ref[0] = jnp.where(lane < IDX_HEAD_DIM, xr, pltpu.roll(xr, IDX_HEAD_DIM, 1)).astype(BF16)


def _dsa_block(s_eff, q_ref, qi_ref, wi_ref, cq_ref, sq_ref, ciq_ref, siq_ref, kb_ref, vb_ref, ki2_ref, o_ref,
               key_ref, *, topk):
    t_blk = DSA_BLOCK
    qb = pl.program_id(1)
    lane = _lane_iota((t_blk, LANES))
    low_half = (lane % IDX_HEAD_DIM) < IDX_HEAD_DIM // 2
    lo64 = lane < IDX_HEAD_DIM

    wi = wi_ref[0] * (IDX_HEADS ** -0.5 * IDX_HEAD_DIM ** -0.5)
    ciq, siq = ciq_ref[0], siq_ref[0]
    ki2 = ki2_ref[0, :s_eff, :]
    score = jnp.zeros((t_blk, s_eff), F32)
    for quad in range(IDX_HEADS // 4):
        rows = []
        for pr in (2 * quad, 2 * quad + 1):
            xr = _rope64(qi_ref[0, :, pr * LANES:(pr + 1) * LANES], ciq, siq, low_half)
            rows.append(jnp.where(lo64, xr, 0.0))
            rows.append(jnp.where(lo64, 0.0, xr))
        s4 = _bdot_nt(jnp.concatenate(rows, axis=0), ki2)
        for j in range(4):
            hh = 4 * quad + j
            w_col = wi[:, IDX_HEAD_DIM + hh:IDX_HEAD_DIM + hh + 1]
            score = score + w_col * jnp.maximum(s4[j * t_blk:(j + 1) * t_blk], 0.0)

    t_idx = qb * t_blk + lax.broadcasted_iota(jnp.int32, (t_blk, 1), 0)
    s_idx = _lane_iota((t_blk, s_eff))
    score = jnp.where(s_idx <= t_idx, score, -jnp.inf)
    bits = lax.bitcast_convert_type(score, jnp.int32)
    key_ref[:, :s_eff] = bits ^ ((bits >> 31) & jnp.int32(0x7FFFFFFF))

    kk = jnp.minimum(t_idx + 1, topk).astype(F32)
    int_min = jnp.int32(-2 ** 31)

    def bisect(i, t_u):
        cand_u = t_u | lax.shift_left(jnp.int32(1), 31 - i)
        hit = key_ref[:, :s_eff] >= (cand_u ^ int_min)
        cnt = jnp.sum(jnp.where(hit, 1.0, 0.0), axis=1, keepdims=True)
        return jnp.where(cnt >= kk, cand_u, t_u)

    t_u = lax.fori_loop(0, 32, bisect, jnp.zeros((t_blk, 1), jnp.int32))
    sel = key_ref[:, :s_eff] >= (t_u ^ int_min)

    cq, sq = cq_ref[0], sq_ref[0]
    group = DSA_HEADS // DSA_KV_HEADS
    def group_logits(g):
        qs = []
        for j in range(group):
            hh = group * g + j
            qs.append(_rope128(q_ref[0, :, hh * DSA_HEAD_DIM:(hh + 1) * DSA_HEAD_DIM], cq, sq))
        return _bdot_nt(jnp.concatenate(qs, axis=0), kb_ref[0, g, :s_eff, :]) * DSA_HEAD_DIM ** -0.5

    next_logits = group_logits(0)
    for g in range(DSA_KV_HEADS):
        logits = next_logits
        if g + 1 < DSA_KV_HEADS:
            next_logits = group_logits(g + 1)
        ps, dens = [], []
        for j in range(group):
            lj = jnp.where(sel, logits[j * t_blk:(j + 1) * t_blk], -jnp.inf)
            pj = jnp.exp(lj - jnp.max(lj, axis=1, keepdims=True))
            dens.append(jnp.sum(pj, axis=1, keepdims=True))
            ps.append(pj)
        pv = _bdot(jnp.concatenate(ps, axis=0), vb_ref[0, g, :s_eff, :])
        for j in range(group):
            hh = group * g + j
            o_ref[0, :, hh * DSA_HEAD_DIM:(hh + 1) * DSA_HEAD_DIM] = (
                pv[j * t_blk:(j + 1) * t_blk] / dens[j]).astype(o_ref.dtype)


def _dsa_kernel(q_ref, qi_ref, wi_ref, cq_ref, sq_ref, ciq_ref, siq_ref, kb_ref, vb_ref, ki2_ref, o_ref, key_ref,
                *, topk, seq):
    qb = pl.program_id(1)
    blocks_per_step = DSA_KV_STEP // DSA_BLOCK
    for br in range(seq // DSA_KV_STEP):
        @pl.when(qb // blocks_per_step == br)
        def _():
            _dsa_block((br + 1) * DSA_KV_STEP, q_ref, qi_ref, wi_ref, cq_ref, sq_ref, ciq_ref, siq_ref, kb_ref,
                       vb_ref, ki2_ref, o_ref, key_ref, topk=topk)


def _dsa_core(y, yx, positions):
    bsz, l, _ = y.shape
    topk = min(DSA_TOPK_MAX, l // 4)
    cos_q, sin_q = _rope_tables(positions, DSA_HEAD_DIM)
    cos_i, sin_i = _rope_tables(positions, IDX_HEAD_DIM)
    k_blk = DSA_Q_DIM // DSA_KV_DIM
    ki_blk = 0
    ps = DSA_KV_STEP
    tab = pl.BlockSpec((1, ps, LANES), lambda b, i: (b, i, 0))
    kb, vb, ki2 = pl.pallas_call(
        _dsa_prep_kernel,
        out_shape=(jax.ShapeDtypeStruct((bsz, DSA_KV_HEADS, l, DSA_HEAD_DIM), BF16),
                   jax.ShapeDtypeStruct((bsz, DSA_KV_HEADS, l, DSA_HEAD_DIM), BF16),
                   jax.ShapeDtypeStruct((bsz, l, LANES), BF16)),
        grid=(bsz, l // ps),
        in_specs=[pl.BlockSpec((1, ps, DSA_KV_DIM), lambda b, i: (b, i, k_blk)),
                  pl.BlockSpec((1, ps, DSA_KV_DIM), lambda b, i: (b, i, k_blk + 1)),
                  pl.BlockSpec((1, ps, LANES), lambda b, i: (b, i, ki_blk)),
                  tab, tab, tab, tab],
        out_specs=(pl.BlockSpec((1, DSA_KV_HEADS, ps, DSA_HEAD_DIM), lambda b, i: (b, 0, i, 0)),
                   pl.BlockSpec((1, DSA_KV_HEADS, ps, DSA_HEAD_DIM), lambda b, i: (b, 0, i, 0)),
                   pl.BlockSpec((1, ps, LANES), lambda b, i: (b, i, 0))),
        compiler_params=_ARB2,
        name="dsa_prep",
    )(y, y, yx, cos_q, sin_q, cos_i, sin_i)

    t_blk = DSA_BLOCK
    qtab = pl.BlockSpec((1, t_blk, LANES), lambda b, i: (b, i, 0))
    kv_spec = pl.BlockSpec((1, DSA_KV_HEADS, l, DSA_HEAD_DIM), lambda b, i: (b, 0, 0, 0))
    return pl.pallas_call(
        functools.partial(_dsa_kernel, topk=topk, seq=l),
        out_shape=jax.ShapeDtypeStruct((bsz, l, DSA_Q_DIM), BF16),
        grid=(bsz, l // t_blk),
        in_specs=[pl.BlockSpec((1, t_blk, DSA_Q_DIM), lambda b, i: (b, i, 0)),
                  pl.BlockSpec((1, t_blk, IDX_HEADS * IDX_HEAD_DIM),
                               lambda b, i: (b, i, (DSA_Q_DIM + 2 * DSA_KV_DIM) // (IDX_HEADS * IDX_HEAD_DIM))),
                  pl.BlockSpec((1, t_blk, LANES), lambda b, i: (b, i, ki_blk)),
                  qtab, qtab, qtab, qtab,
                  kv_spec, kv_spec,
                  pl.BlockSpec((1, l, LANES), lambda b, i: (b, 0, 0))],
        out_specs=pl.BlockSpec((1, t_blk, DSA_Q_DIM), lambda b, i: (b, i, 0)),
        scratch_shapes=[pltpu.VMEM((t_blk, l), jnp.int32)],
        compiler_params=_ARB2,
        name="dsa_attn",
    )(y, y, yx, cos_q, sin_q, cos_i, sin_i, kb, vb, ki2)


def _dsa_mixer(u, positions, w_in, w_out, layer, res):
    bsz, l, dm = u.shape
    u2 = u.reshape(bsz * l, dm)
    n_main = DSA_Q_DIM + 2 * DSA_KV_DIM + IDX_HEADS * IDX_HEAD_DIM
    w_in_t = _transposed(w_in)
    y = _proj(u2, w_in_t, layer, n_out=n_main, w_t=True)
    yx = _proj_tail(u2, w_in_t, layer, n_main)
    o = _dsa_core(y.reshape(bsz, l, -1), yx.reshape(bsz, l, -1), positions)
    return _proj(o.reshape(bsz * l, DSA_Q_DIM), w_out, layer, mode="residual", res=res)


def kernel(x, p, positions, norm_mix, norm_mlp, norm_ple, norm_final, gdn_w_in, gdn_conv_w, gdn_a_log,
           gdn_dt_bias, gdn_norm, gdn_w_out, ret_w_in, ret_w_out, dsa_w_in, dsa_w_out, mlp_w_up, mlp_w_down,
           ple_w_gate, ple_w_proj):
    bsz, l, d = x.shape
    m = bsz * l
    h = x.reshape(m, d)
    for i in range(DEPTH):
        kind, j = i % N_MIXERS, i // N_MIXERS
        u = _rmsnorm(h, norm_mix[i], BF16).reshape(bsz, l, d)
        if kind == 0:
            h = _gdn_mixer(u, gdn_w_in, gdn_conv_w, j, gdn_a_log[j], gdn_dt_bias[j], gdn_norm[j], gdn_w_out, h)
        elif kind == 1:
            h = _retention_mixer(u, positions, ret_w_in, ret_w_out, j, h)
        else:
            h = _dsa_mixer(u, positions, dsa_w_in, dsa_w_out, j, h)
        u = _rmsnorm(h, norm_mlp[i], BF16)
        a = _proj(u, mlp_w_up, i, mode="relu2", out_dtype=BF16)
        h = _proj(a, mlp_w_down, i, mode="residual", res=h)
        u = _rmsnorm(h, norm_ple[i], BF16)
        h = _proj(u, ple_w_gate, i, mode="ple", res=h, p=p.reshape(DEPTH, m, -1), wp=ple_w_proj)
    return _rmsnorm(h, norm_final, x.dtype).reshape(bsz, l, d)
```

```python
import functools

import jax
import jax.numpy as jnp
from jax import lax
from jax.experimental import pallas as pl
from jax.experimental.pallas import tpu as pltpu

F32 = jnp.float32
BF16 = jnp.bfloat16

D_MODEL = 2048
DEPTH = 4
N_MIXERS = 3
NORM_EPS = 1e-6
ROPE_THETA = 10000.0

GDN_HEAD_DIM = 128
GDN_QK_HEADS = D_MODEL // 128
GDN_V_HEADS = 2 * GDN_QK_HEADS
GDN_CONV = 4
GDN_CHUNK = 64
GDN_KEY_DIM = GDN_QK_HEADS * GDN_HEAD_DIM
GDN_VAL_DIM = GDN_V_HEADS * GDN_HEAD_DIM
GDN_CONV_DIM = 2 * GDN_KEY_DIM + GDN_VAL_DIM
GDN_SUPER = 256
GDN_STEP_QK = 2

RET_HEADS = 8
RET_QK_DIM = D_MODEL
RET_V_DIM = 2 * D_MODEL
RET_QK_HEAD = RET_QK_DIM // RET_HEADS
RET_V_HEAD = RET_V_DIM // RET_HEADS
RET_CHUNK = 128
RET_STEP = 256

DSA_HEAD_DIM = 128
DSA_HEADS = D_MODEL // DSA_HEAD_DIM
DSA_KV_HEADS = 4
IDX_HEADS = 16
IDX_HEAD_DIM = 64
DSA_TOPK_MAX = 256
DSA_BLOCK = 128
DSA_KV_STEP = 512
DSA_Q_DIM = DSA_HEADS * DSA_HEAD_DIM
DSA_KV_DIM = DSA_KV_HEADS * DSA_HEAD_DIM

LANES = 128
SUBLANES = 8
VMEM_LIMIT_BYTES = 56 * 1024 * 1024

_ARB2 = pltpu.CompilerParams(dimension_semantics=("arbitrary", "arbitrary"), vmem_limit_bytes=VMEM_LIMIT_BYTES)


def _bdot(a, b):
    return jnp.dot(a.astype(BF16), b.astype(BF16), preferred_element_type=F32)


def _bdot_nt(a, b):
    return lax.dot_general(a.astype(BF16), b.astype(BF16), (((1,), (1,)), ((), ())), preferred_element_type=F32)


def _bdot_tn(a, b):
    return lax.dot_general(a.astype(BF16), b.astype(BF16), (((0,), (0,)), ((), ())), preferred_element_type=F32)


def _lane_iota(shape):
    return lax.broadcasted_iota(jnp.int32, shape, 1)


def _silu(x):
    return x * jax.nn.sigmoid(x)


def _mm_kernel(*refs, nk, mode, valid, w_t):
    x_ref, w_ref, o_ref = refs[0], refs[1], refs[-1]
    extra = refs[2:-1]
    if w_t:
        w = w_ref[...]
        if mode == "tail":
            w = jnp.where(lax.broadcasted_iota(jnp.int32, w.shape, 0) < valid, w, 0.0)
        o_ref[...] = _bdot_nt(x_ref[...], w)
        return
    if nk > 1:
        @pl.when(pl.program_id(2) == 0)
        def _():
            o_ref[...] = extra[0][...]

        half = o_ref.shape[1] // 2
        for cs in (slice(0, half), slice(half, 2 * half)):
            o_ref[:, cs] += _bdot(x_ref[...], w_ref[:, cs])
        return
    if mode == "tail":
        w = jnp.where(_lane_iota(w_ref.shape) < valid, w_ref[...], 0.0)
        o_ref[...] = _bdot(x_ref[...], w)
        return
    prod = _bdot(x_ref[...], w_ref[...])
    if mode == "plain":
        o_ref[...] = prod.astype(o_ref.dtype)
    elif mode == "relu2":
        o_ref[...] = jnp.square(jnp.maximum(prod, 0.0)).astype(o_ref.dtype)
    elif mode == "ple":
        res_ref, p_ref, wp_ref = extra
        o_ref[...] = res_ref[...] + jax.nn.sigmoid(prod) * _bdot(p_ref[...], wp_ref[...])
    else:
        o_ref[...] = extra[0][...] + prod


def _matmul(x, w, layer, *, n_out, bm, bn, bk, n_outer, col0=0, mode="plain", out_dtype=F32, res=None, p=None,
            wp=None, w_t=False):
    m, kdim = x.shape
    bm = min(bm, m)
    n_total = w.shape[1] if w_t else w.shape[2]
    assert m % bm == 0 and n_out % bn == 0 and col0 % bn == 0 and kdim % bk == 0, (x.shape, w.shape, bm, bn, bk)
    assert col0 + n_out <= n_total or (mode == "tail" and n_out == bn)
    nk = kdim // bk
    assert nk == 1 or mode == "residual"
    assert not w_t or mode in ("plain", "tail")
    cb0 = col0 // bn
    if n_outer:
        grid = (n_out // bn, m // bm, nk)
        mi, ni = (lambda j, i, k: i), (lambda j, i, k: j)
    else:
        grid = (m // bm, n_out // bn, nk)
        mi, ni = (lambda i, j, k: i), (lambda i, j, k: j)
    in_specs = [pl.BlockSpec((bm, bk), lambda a, b, k: (mi(a, b, k), k)),
                pl.BlockSpec((None, bn, bk), lambda a, b, k: (layer, cb0 + ni(a, b, k), k)) if w_t else
                pl.BlockSpec((None, bk, bn), lambda a, b, k: (layer, k, cb0 + ni(a, b, k)))]
    args = [x, w]
    o_map = lambda a, b, k: (mi(a, b, k), ni(a, b, k))
    if mode in ("residual", "ple"):
        in_specs.append(pl.BlockSpec((bm, bn), o_map))
        args.append(res)
    if mode == "ple":
        kp = p.shape[2]
        in_specs.append(pl.BlockSpec((None, bm, kp), lambda a, b, k: (layer, mi(a, b, k), 0)))
        in_specs.append(pl.BlockSpec((None, kp, bn), lambda a, b, k: (layer, 0, ni(a, b, k))))
        args += [p, wp]
    return pl.pallas_call(
        functools.partial(_mm_kernel, nk=nk, mode=mode, valid=n_total - col0, w_t=w_t),
        out_shape=jax.ShapeDtypeStruct((m, n_out), out_dtype),
        grid=grid,
        in_specs=in_specs,
        out_specs=pl.BlockSpec((bm, bn), o_map),
        compiler_params=pltpu.CompilerParams(
            dimension_semantics=("arbitrary", "arbitrary", "arbitrary"), vmem_limit_bytes=VMEM_LIMIT_BYTES),
        name="matmul_" + mode,
    )(*args)


def _proj(x, w, layer, n_out=None, **kw):
    w_t = kw.get("w_t", False)
    kdim = w.shape[2] if w_t else w.shape[1]
    n_out = (w.shape[1] if w_t else w.shape[2]) if n_out is None else n_out
    if kdim <= 2048:
        bm = 512 if kw.get("mode") == "ple" else 1024
        return _matmul(x, w, layer, n_out=n_out, bm=bm, bn=min(n_out, 1024), bk=kdim, n_outer=True, **kw)
    return _matmul(x, w, layer, n_out=n_out, bm=2048, bn=1024, bk=1024, n_outer=False, **kw)


def _proj_tail(x, w_t, layer, start):
    return _matmul(x, w_t, layer, n_out=LANES, col0=start, bm=1024, bn=LANES, bk=w_t.shape[2], n_outer=True,
                   mode="tail", w_t=True)


def _transposed(w):
    return jnp.swapaxes(w, 1, 2)


def _rmsnorm_kernel(x_ref, g_ref, o_ref):
    x = x_ref[...]
    y = x * lax.rsqrt(jnp.mean(x * x, axis=1, keepdims=True) + NORM_EPS) * g_ref[...]
    o_ref[...] = y.astype(o_ref.dtype)


def _rmsnorm(x, gain, out_dtype):
    m, d = x.shape
    bm = min(m, 512)
    return pl.pallas_call(
        _rmsnorm_kernel,
        out_shape=jax.ShapeDtypeStruct((m, d), out_dtype),
        grid=(m // bm,),
        in_specs=[pl.BlockSpec((bm, d), lambda i: (i, 0)), pl.BlockSpec((1, d), lambda i: (0, 0))],
        out_specs=pl.BlockSpec((bm, d), lambda i: (i, 0)),
        compiler_params=pltpu.CompilerParams(dimension_semantics=("arbitrary",), vmem_limit_bytes=VMEM_LIMIT_BYTES),
        name="rmsnorm",
    )(x, gain.reshape(1, d))


def _gdn_gates_kernel(gb_ref, alog_ref, dtb_ref, o_ref):
    x = gb_ref[0]
    lane = _lane_iota(x.shape)
    row = lax.broadcasted_iota(jnp.int32, x.shape, 0) % GDN_CHUNK
    g = -jnp.exp(alog_ref[...]) * jax.nn.softplus(x + dtb_ref[...])
    sh = 1
    while sh < GDN_CHUNK:
        g = g + jnp.where(row >= sh, pltpu.roll(g, sh, 0), 0.0)
        sh *= 2
    o_ref[0] = jnp.where(lane < GDN_V_HEADS, jax.nn.sigmoid(x), g)


def _conv_silu_tile(x, tail, w):
    rows8 = lax.broadcasted_iota(jnp.int32, tail.shape, 0)
    acc = x * w[GDN_CONV - 1:GDN_CONV]
    for sh in range(1, GDN_CONV):
        xs = pltpu.roll(x, sh, 0)
        head = jnp.where(rows8 < sh, pltpu.roll(tail, sh, 0), xs[:SUBLANES])
        xs = jnp.concatenate([head, xs[SUBLANES:]], axis=0)
        acc = acc + xs * w[GDN_CONV - 1 - sh:GDN_CONV - sh]
    return _silu(acc)


def _gdn_kernel(q_ref, k_ref, v_ref, z_ref, gt_ref, wq_ref, wk_ref, wv_ref, nw_ref, o_ref,
                s_ref, kn_st, wk_st, qg_st, gl_st, in_st, *, seq):
    hq = pl.program_id(1)
    r_blk, c_blk, d = GDN_SUPER, GDN_CHUNK, GDN_HEAD_DIM
    ii = lax.broadcasted_iota(jnp.int32, (r_blk, r_blk), 0)
    jj = lax.broadcasted_iota(jnp.int32, (r_blk, r_blk), 1)
    blk_masks = [(ii >> sh) == (jj >> sh) for sh in (3, 4, 5, 6)]
    blk8 = blk_masks[0]
    tri_incl = blk_masks[3] & (ii >= jj)
    diag = ii == jj
    eye = jnp.where(diag, 1.0, 0.0)
    lane = _lane_iota((r_blk, LANES))
    s_ref[...] = jnp.zeros_like(s_ref)
    wq, wk, wv, nw = wq_ref[...], wk_ref[...], wv_ref[...], nw_ref[...]
    n_steps = seq // r_blk
    rep = GDN_V_HEADS // GDN_QK_HEADS

    def fetch(s):
        r0 = pl.multiple_of(s * r_blk, r_blk)
        rt = pl.multiple_of(jnp.maximum(r0 - SUBLANES, 0), SUBLANES)
        tail = lambda ref: jnp.where(s > 0, ref[0, pl.ds(rt, SUBLANES), :], 0.0)
        rows = pl.ds(r0, r_blk)
        return (q_ref[0, rows, :], tail(q_ref), k_ref[0, rows, :], tail(k_ref), v_ref[0, rows, :], tail(v_ref),
                gt_ref[0, rows, :])

    heads = range(GDN_STEP_QK * rep)
    qk_of = lambda j: j // rep

    def solve(raw, slot):
        xq, tq, xk, tk, xv, tv, gt = raw
        qc, kc, vc = _conv_silu_tile(xq, tq, wq), _conv_silu_tile(xk, tk, wk), _conv_silu_tile(xv, tv, wv)
        qn, kn = [], []
        for i in range(GDN_STEP_QK):
            qi, ki = qc[:, i * d:(i + 1) * d], kc[:, i * d:(i + 1) * d]
            qn.append(qi * lax.rsqrt(jnp.sum(qi * qi, axis=1, keepdims=True) + NORM_EPS) * d ** -0.5)
            kn.append(ki * lax.rsqrt(jnp.sum(ki * ki, axis=1, keepdims=True) + NORM_EPS))
            kn_st[slot, i] = kn[i]
        hv0 = len(heads) * hq
        beta = [jnp.sum(jnp.where(lane == hv0 + j, gt, 0.0), axis=1, keepdims=True) for j in heads]
        g = [jnp.sum(jnp.where(lane == GDN_V_HEADS + hv0 + j, gt, 0.0), axis=1, keepdims=True) for j in heads]
        gi = [jnp.broadcast_to(g[j], (r_blk, r_blk)) for j in heads]
        dec = [jnp.exp(jnp.where(tri_incl, gi[j] - jnp.sum(jnp.where(diag, gi[j], 0.0), axis=0, keepdims=True),
                                 -jnp.inf)) for j in heads]
        kb = [kn[qk_of(j)] * beta[j] for j in heads]
        for j in heads:
            qg_st[slot, j] = qn[qk_of(j)] * jnp.exp(g[j])
            gl_st[slot, j] = gi[j][:, :d]
        qk = [_bdot_nt(qn[i], kn[i]) for i in range(GDN_STEP_QK)]
        kk = [_bdot_nt(kb[j], kn[qk_of(j)]) for j in heads]
        yield
        for j in heads:
            in_st[slot, j] = qk[qk_of(j)] * dec[j]
        a = [jnp.where(diag, 0.0, kk[j] * dec[j]) for j in heads]
        a8 = [jnp.where(blk8, a[j], 0.0) for j in heads]
        a8_2 = [_bdot(a8[j], a8[j]) for j in heads]
        yield
        inv = [eye - a8[j] for j in heads]
        t = [_bdot(inv[j], a8_2[j]) for j in heads]
        a8_4 = [_bdot(a8_2[j], a8_2[j]) for j in heads]
        yield
        inv = [inv[j] + t[j] for j in heads]
        t = [_bdot(inv[j], a8_4[j]) for j in heads]
        yield
        inv = [inv[j] + t[j] for j in heads]
        for lvl in range(3):
            off = [jnp.where(blk_masks[lvl + 1] & ~blk_masks[lvl], a[j], 0.0) for j in heads]
            t = [_bdot(off[j], inv[j]) for j in heads]
            yield
            t = [_bdot(inv[j], t[j]) for j in heads]
            yield
            inv = [inv[j] - t[j] for j in heads]
        rhs = [jnp.concatenate([vc[:, j * d:(j + 1) * d] * beta[j], kb[j] * jnp.exp(g[j])], axis=1) for j in heads]
        x = [_bdot(inv[j], rhs[j]) for j in heads]
        yield
        for j in heads:
            wk_st[slot, j] = x[j]

    def recur(s, slot):
        r0 = pl.multiple_of(s * r_blk, r_blk)
        state = [s_ref[j] for j in heads]
        for c in range(r_blk // c_blk):
            rs = slice(c * c_blk, (c + 1) * c_blk)
            gl = [gl_st[slot, j, rs, :] for j in heads]
            g_last = [gl[j][c_blk - 1:c_blk, :] for j in heads]
            both = [_bdot(jnp.concatenate([wk_st[slot, j, rs, d:], qg_st[slot, j, rs, :]], axis=0), state[j])
                    for j in heads]
            yield
            v_new = [wk_st[slot, j, rs, :d] - both[j][:c_blk] for j in heads]
            kd = [kn_st[slot, qk_of(j), rs, :] * jnp.exp(g_last[j] - gl[j]) for j in heads]
            upd = [_bdot_tn(kd[j], v_new[j]) for j in heads]
            pad = []
            for j in heads:
                parts = []
                if c > 0:
                    parts.append(jnp.zeros((c * c_blk, d), F32))
                parts.append(v_new[j])
                if (c + 1) * c_blk < r_blk:
                    parts.append(jnp.zeros((r_blk - (c + 1) * c_blk, d), F32))
                pad.append(jnp.concatenate(parts, axis=0))
            o_in = [_bdot(in_st[slot, j, rs, :], pad[j]) for j in heads]
            yield
            state = [state[j] * jnp.exp(g_last[j]) + upd[j] for j in heads]
            rows = pl.ds(r0 + c * c_blk, c_blk)
            for j in heads:
                o_c = both[j][c_blk:] + o_in[j]
                on = o_c * lax.rsqrt(jnp.mean(o_c * o_c, axis=1, keepdims=True) + NORM_EPS) * nw
                zc = z_ref[0, rows, j * d:(j + 1) * d]
                o_ref[0, rows, j * d:(j + 1) * d] = (on * _silu(zc)).astype(o_ref.dtype)
        for j in heads:
            s_ref[j] = state[j]

    def run(*gens):
        live = list(gens)
        while live:
            live = [gen for gen in live if next(gen, StopIteration) is not StopIteration]

    run(solve(fetch(0), 0))

    def body(i, carry):
        s = 2 * i
        run(solve(fetch(s + 1), 1), recur(s, 0))
        run(solve(fetch(s + 2), 0), recur(s + 1, 1))
        return carry

    lax.fori_loop(0, n_steps // 2 - 1, body, 0)
    run(solve(fetch(n_steps - 1), 1), recur(n_steps - 2, 0))
    run(recur(n_steps - 1, 1))


def _gdn_core(y, yg, conv_w, layer, a_log, dt_bias, norm_w):
    bsz, l, _ = y.shape
    assert l % (2 * GDN_SUPER) == 0
    nh, d = GDN_QK_HEADS, GDN_HEAD_DIM
    rep = GDN_V_HEADS // GDN_QK_HEADS
    lane_vec = lambda t: jnp.zeros((1, LANES), F32).at[0, GDN_V_HEADS:2 * GDN_V_HEADS].set(t.astype(F32))
    gates = pl.pallas_call(
        _gdn_gates_kernel,
        out_shape=jax.ShapeDtypeStruct((bsz, l, LANES), F32),
        grid=(bsz,),
        in_specs=[pl.BlockSpec((1, l, LANES), lambda b: (b, 0, 0)),
                  pl.BlockSpec((1, LANES), lambda b: (0, 0)), pl.BlockSpec((1, LANES), lambda b: (0, 0))],
        out_specs=pl.BlockSpec((1, l, LANES), lambda b: (b, 0, 0)),
        compiler_params=pltpu.CompilerParams(dimension_semantics=("arbitrary",), vmem_limit_bytes=VMEM_LIMIT_BYTES),
        name="gdn_gates",
    )(yg, lane_vec(a_log), lane_vec(dt_bias))
    r_blk = GDN_SUPER
    nq = GDN_STEP_QK
    qw, vw = nq * d, nq * rep * d
    nv = nq * rep
    k0, v0, z0 = GDN_KEY_DIM // qw, 2 * GDN_KEY_DIM // vw, GDN_CONV_DIM // vw
    return pl.pallas_call(
        functools.partial(_gdn_kernel, seq=l),
        out_shape=jax.ShapeDtypeStruct((bsz, l, GDN_VAL_DIM), BF16),
        grid=(bsz, nh // nq),
        in_specs=[pl.BlockSpec((1, l, qw), lambda b, h: (b, 0, h)),
                  pl.BlockSpec((1, l, qw), lambda b, h: (b, 0, k0 + h)),
                  pl.BlockSpec((1, l, vw), lambda b, h: (b, 0, v0 + h)),
                  pl.BlockSpec((1, l, vw), lambda b, h: (b, 0, z0 + h)),
                  pl.BlockSpec((1, l, LANES), lambda b, h: (b, 0, 0)),
                  pl.BlockSpec((None, GDN_CONV, qw), lambda b, h: (layer, 0, h)),
                  pl.BlockSpec((None, GDN_CONV, qw), lambda b, h: (layer, 0, k0 + h)),
                  pl.BlockSpec((None, GDN_CONV, vw), lambda b, h: (layer, 0, v0 + h)),
                  pl.BlockSpec((1, d), lambda b, h: (0, 0))],
        out_specs=pl.BlockSpec((1, l, vw), lambda b, h: (b, 0, h)),
        scratch_shapes=[pltpu.VMEM((nv, d, d), F32),
                        pltpu.VMEM((2, nq, r_blk, d), F32),
                        pltpu.VMEM((2, nv, r_blk, 2 * d), F32),
                        pltpu.VMEM((2, nv, r_blk, d), F32),
                        pltpu.VMEM((2, nv, r_blk, d), F32),
                        pltpu.VMEM((2, nv, r_blk, r_blk), F32)],
        compiler_params=_ARB2,
        name="gdn_core",
    )(y, y, y, y, gates, conv_w, conv_w, conv_w, norm_w.reshape(1, d))


def _gdn_mixer(u, w_in, conv_w, layer, a_log, dt_bias, norm_w, w_out, res):
    bsz, l, dm = u.shape
    u2 = u.reshape(bsz * l, dm)
    n_main = GDN_CONV_DIM + GDN_VAL_DIM
    w_in_t = _transposed(w_in)
    y = _proj(u2, w_in_t, layer, n_out=n_main, w_t=True)
    yg = _proj_tail(u2, w_in_t, layer, n_main)
    o = _gdn_core(y.reshape(bsz, l, -1), yg.reshape(bsz, l, -1), conv_w, layer, a_log, dt_bias, norm_w)
    return _proj(o.reshape(bsz * l, GDN_VAL_DIM), w_out, layer, mode="residual", res=res)


def _ret_kernel(q_ref, k_ref, v_ref, gate_ref, cos_ref, sin_ref, dec_ref, qd_ref, kd_ref, cd_ref, o_ref, s_ref):
    @pl.when(pl.program_id(1) == 0)
    def _():
        s_ref[...] = jnp.zeros_like(s_ref)

    half = RET_QK_HEAD // 2

    def rope(ref, rs, h, c, s):
        x1 = ref[0, rs, h * RET_QK_HEAD:h * RET_QK_HEAD + half]
        x2 = ref[0, rs, h * RET_QK_HEAD + half:(h + 1) * RET_QK_HEAD]
        return jnp.concatenate([x1 * c - x2 * s, x2 * c + x1 * s], axis=1)

    heads = range(RET_HEADS)
    vs = [slice(h * RET_V_HEAD, (h + 1) * RET_V_HEAD) for h in heads]
    for c in range(RET_STEP // RET_CHUNK):
        rs = slice(c * RET_CHUNK, (c + 1) * RET_CHUNK)
        cos, sin = cos_ref[0, rs, :], sin_ref[0, rs, :]
        qr = [rope(q_ref, rs, h, cos, sin) for h in heads]
        kr = [rope(k_ref, rs, h, cos, sin) * RET_QK_HEAD ** -0.5 for h in heads]
        sc = [_bdot_nt(qr[h], kr[h]) for h in heads]
        cross = [_bdot(qr[h] * jnp.concatenate([qd_ref[h], qd_ref[h]], axis=1), s_ref[h]) for h in heads]
        upd = [_bdot_tn(kr[h] * jnp.concatenate([kd_ref[h], kd_ref[h]], axis=1), v_ref[0, rs, vs[h]]) for h in heads]
        inner = [_bdot(sc[h] * dec_ref[h], v_ref[0, rs, vs[h]]) for h in heads]
        for h in heads:
            s_ref[h] = s_ref[h] * cd_ref[h] + upd[h]
            o = inner[h] + cross[h]
            on = o * lax.rsqrt(jnp.mean(o * o, axis=1, keepdims=True) + NORM_EPS)
            o_ref[0, rs, vs[h]] = (_silu(gate_ref[0, rs, vs[h]]) * on).astype(o_ref.dtype)


def _ret_core(y, positions):
    bsz, l, _ = y.shape
    c = RET_CHUNK
    inv_freq = ROPE_THETA ** (-jnp.arange(0, RET_QK_HEAD, 2, dtype=F32) / RET_QK_HEAD)
    ang = positions.astype(F32)[..., None] * inv_freq
    cos, sin = jnp.cos(ang), jnp.sin(ang)
    log_gamma = jnp.log1p(-jnp.exp2(-5.0 - jnp.arange(RET_HEADS, dtype=F32)))
    idx = jnp.arange(c, dtype=F32)
    rel = idx[:, None] - idx[None, :]
    inner_decay = jnp.where(rel >= 0, jnp.exp(log_gamma[:, None, None] * jnp.maximum(rel, 0.0)), 0.0)
    lanes = lambda t: jnp.broadcast_to(t[:, :, None], (RET_HEADS, c, LANES))
    q_decay = lanes(jnp.exp(log_gamma[:, None] * (idx[None, :] + 1.0)))
    k_decay = lanes(jnp.exp(log_gamma[:, None] * (c - 1.0 - idx[None, :])))
    chunk_decay = jnp.exp(log_gamma * c)
    t = RET_STEP
    full = lambda shape: pl.BlockSpec(shape, lambda b, i: (0,) * len(shape))
    return pl.pallas_call(
        _ret_kernel,
        out_shape=jax.ShapeDtypeStruct((bsz, l, RET_V_DIM), BF16),
        grid=(bsz, l // t),
        in_specs=[pl.BlockSpec((1, t, RET_QK_DIM), lambda b, i: (b, i, 0)),
                  pl.BlockSpec((1, t, RET_QK_DIM), lambda b, i: (b, i, 1)),
                  pl.BlockSpec((1, t, RET_V_DIM), lambda b, i: (b, i, 1)),
                  pl.BlockSpec((1, t, RET_V_DIM), lambda b, i: (b, i, 2)),
                  pl.BlockSpec((1, t, LANES), lambda b, i: (b, i, 0)),
                  pl.BlockSpec((1, t, LANES), lambda b, i: (b, i, 0)),
                  full((RET_HEADS, c, c)), full((RET_HEADS, c, LANES)), full((RET_HEADS, c, LANES)),
                  pl.BlockSpec(memory_space=pltpu.SMEM)],
        out_specs=pl.BlockSpec((1, t, RET_V_DIM), lambda b, i: (b, i, 0)),
        scratch_shapes=[pltpu.VMEM((RET_HEADS, RET_QK_HEAD, RET_V_HEAD), F32)],
        compiler_params=_ARB2,
        name="ret_core",
    )(y, y, y, y, cos, sin, inner_decay, q_decay, k_decay, chunk_decay)


def _retention_mixer(u, positions, w_in, w_out, layer, res):
    bsz, l, dm = u.shape
    y = _proj(u.reshape(bsz * l, dm), w_in, layer)
    o = _ret_core(y.reshape(bsz, l, -1), positions)
    return _proj(o.reshape(bsz * l, RET_V_DIM), w_out, layer, mode="residual", res=res)


def _rope_tables(positions, dim):
    inv_freq = ROPE_THETA ** (-jnp.arange(0, dim, 2, dtype=F32) / dim)
    ang = positions.astype(F32)[..., None] * inv_freq
    c, s = jnp.cos(ang), jnp.sin(ang)
    reps = LANES // dim
    cos_full = jnp.tile(jnp.concatenate([c, c], axis=-1), (1, 1, reps))
    sin_signed = jnp.tile(jnp.concatenate([-s, s], axis=-1), (1, 1, reps))
    return cos_full, sin_signed


def _rope128(x, cos, sin_signed):
    return x * cos + pltpu.roll(x, 64, 1) * sin_signed


def _rope64(x, cos, sin_signed, low_half):
    partner = jnp.where(low_half, pltpu.roll(x, 96, 1), pltpu.roll(x, 32, 1))
    return x * cos + partner * sin_signed


def _dsa_prep_kernel(k_ref, v_ref, ki_ref, ck_ref, sk_ref, cik_ref, sik_ref, kb_ref, vb_ref, ki2_ref):
    ck, sk = ck_ref[0], sk_ref[0]
    for g in range(DSA_KV_HEADS):
        sl = slice(g * DSA_HEAD_DIM, (g + 1) * DSA_HEAD_DIM)
        kb_ref[0, g] = _rope128(k_ref[0, :, sl], ck, sk).astype(BF16)
        vb_ref[0, g] = v_ref[0, :, sl].astype(BF16)
    x = ki_ref[0]
    lane = _lane_iota(x.shape)
    xr = _rope64(x, cik_ref[0], sik_ref[0], (lane % IDX_HEAD_DIM) < IDX_HEAD_DIM // 2)
    ki2_ref[0] = jnp.where(lane < IDX_HEAD_DIM, xr, pltpu.roll(xr, IDX_HEAD_DIM, 1)).astype(BF16)


def _dsa_block(s_eff, q_ref, qi_ref, wi_ref, cq_ref, sq_ref, ciq_ref, siq_ref, kb_ref, vb_ref, ki2_ref, o_ref,
               key_ref, keyt_ref, *, topk):
    t_blk = DSA_BLOCK
    qb = pl.program_id(1)
    lane = _lane_iota((t_blk, LANES))
    low_half = (lane % IDX_HEAD_DIM) < IDX_HEAD_DIM // 2
    lo64 = lane < IDX_HEAD_DIM

    wi = wi_ref[0] * (IDX_HEADS ** -0.5 * IDX_HEAD_DIM ** -0.5)
    ciq, siq = ciq_ref[0], siq_ref[0]
    ki2 = ki2_ref[0, :s_eff, :]
    score = jnp.zeros((t_blk, s_eff), F32)
    for quad in range(IDX_HEADS // 4):
        rows = []
        for pr in (2 * quad, 2 * quad + 1):
            xr = _rope64(qi_ref[0, :, pr * LANES:(pr + 1) * LANES], ciq, siq, low_half)
            rows.append(jnp.where(lo64, xr, 0.0))
            rows.append(jnp.where(lo64, 0.0, xr))
        s4 = _bdot_nt(jnp.concatenate(rows, axis=0), ki2)
        for j in range(4):
            hh = 4 * quad + j
            w_col = wi[:, IDX_HEAD_DIM + hh:IDX_HEAD_DIM + hh + 1]
            score = score + w_col * jnp.maximum(s4[j * t_blk:(j + 1) * t_blk], 0.0)

    t_idx = qb * t_blk + lax.broadcasted_iota(jnp.int32, (t_blk, 1), 0)
    s_idx = _lane_iota((t_blk, s_eff))
    score = jnp.where(s_idx <= t_idx, score, -jnp.inf)
    bits = lax.bitcast_convert_type(score, jnp.int32)
    keys = bits ^ ((bits >> 31) & jnp.int32(0x7FFFFFFF))
    key_ref[:, :s_eff] = keys
    keyt_ref[:s_eff, :] = keys.T

    kk = jnp.minimum(qb * t_blk + _lane_iota((1, t_blk)) + 1, topk).astype(F32)
    int_min = jnp.int32(-2 ** 31)

    n_chain = 8
    rows = s_eff // n_chain

    def bisect(i, t_u):
        cand_u = t_u | lax.shift_left(jnp.int32(1), 31 - i)
        cand = jnp.broadcast_to(cand_u ^ int_min, (SUBLANES, t_blk))
        acc = [jnp.zeros((SUBLANES, t_blk), F32) for _ in range(n_chain)]
        for r in range(0, rows, SUBLANES):
            for c in range(n_chain):
                hit = keyt_ref[c * rows + r:c * rows + r + SUBLANES, :] >= cand
                acc[c] = acc[c] + jnp.where(hit, 1.0, 0.0)
        while len(acc) > 1:
            acc = [acc[k] + acc[k + 1] for k in range(0, len(acc), 2)]
        cnt = jnp.sum(acc[0], axis=0, keepdims=True)
        return jnp.where(cnt >= kk, cand_u, t_u)

    t_u = lax.fori_loop(0, 32, bisect, jnp.zeros((1, t_blk), jnp.int32))
    thr = jnp.broadcast_to(t_u ^ int_min, (t_blk, t_blk)).T[:, :1]
    sel = key_ref[:, :s_eff] >= thr

    cq, sq = cq_ref[0], sq_ref[0]
    group = DSA_HEADS // DSA_KV_HEADS
    def group_logits(g):
        qs = []
        for j in range(group):
            hh = group * g + j
            qs.append(_rope128(q_ref[0, :, hh * DSA_HEAD_DIM:(hh + 1) * DSA_HEAD_DIM], cq, sq))
        return _bdot_nt(jnp.concatenate(qs, axis=0), kb_ref[0, g, :s_eff, :]) * DSA_HEAD_DIM ** -0.5

    next_logits = group_logits(0)
    for g in range(DSA_KV_HEADS):
        logits = next_logits
        if g + 1 < DSA_KV_HEADS:
            next_logits = group_logits(g + 1)
        ps, dens = [], []
        for j in range(group):
            lj = jnp.where(sel, logits[j * t_blk:(j + 1) * t_blk], -jnp.inf)
            pj = jnp.exp(lj - jnp.max(lj, axis=1, keepdims=True))
            dens.append(jnp.sum(pj, axis=1, keepdims=True))
            ps.append(pj)
        pv = _bdot(jnp.concatenate(ps, axis=0), vb_ref[0, g, :s_eff, :])
        for j in range(group):
            hh = group * g + j
            o_ref[0, :, hh * DSA_HEAD_DIM:(hh + 1) * DSA_HEAD_DIM] = (
                pv[j * t_blk:(j + 1) * t_blk] / dens[j]).astype(o_ref.dtype)


def _dsa_kernel(q_ref, qi_ref, wi_ref, cq_ref, sq_ref, ciq_ref, siq_ref, kb_ref, vb_ref, ki2_ref, o_ref, key_ref,
                keyt_ref, *, topk, seq):
    qb = pl.program_id(1)
    blocks_per_step = DSA_KV_STEP // DSA_BLOCK
    for br in range(seq // DSA_KV_STEP):
        @pl.when(qb // blocks_per_step == br)
        def _():
            _dsa_block((br + 1) * DSA_KV_STEP, q_ref, qi_ref, wi_ref, cq_ref, sq_ref, ciq_ref, siq_ref, kb_ref,
                       vb_ref, ki2_ref, o_ref, key_ref, keyt_ref, topk=topk)


def _dsa_core(y, yx, positions):
    bsz, l, _ = y.shape
    topk = min(DSA_TOPK_MAX, l // 4)
    cos_q, sin_q = _rope_tables(positions, DSA_HEAD_DIM)
    cos_i, sin_i = _rope_tables(positions, IDX_HEAD_DIM)
    k_blk = DSA_Q_DIM // DSA_KV_DIM
    ki_blk = 0
    ps = DSA_KV_STEP
    tab = pl.BlockSpec((1, ps, LANES), lambda b, i: (b, i, 0))
    kb, vb, ki2 = pl.pallas_call(
        _dsa_prep_kernel,
        out_shape=(jax.ShapeDtypeStruct((bsz, DSA_KV_HEADS, l, DSA_HEAD_DIM), BF16),
                   jax.ShapeDtypeStruct((bsz, DSA_KV_HEADS, l, DSA_HEAD_DIM), BF16),
                   jax.ShapeDtypeStruct((bsz, l, LANES), BF16)),
        grid=(bsz, l // ps),
        in_specs=[pl.BlockSpec((1, ps, DSA_KV_DIM), lambda b, i: (b, i, k_blk)),
                  pl.BlockSpec((1, ps, DSA_KV_DIM), lambda b, i: (b, i, k_blk + 1)),
                  pl.BlockSpec((1, ps, LANES), lambda b, i: (b, i, ki_blk)),
                  tab, tab, tab, tab],
        out_specs=(pl.BlockSpec((1, DSA_KV_HEADS, ps, DSA_HEAD_DIM), lambda b, i: (b, 0, i, 0)),
                   pl.BlockSpec((1, DSA_KV_HEADS, ps, DSA_HEAD_DIM), lambda b, i: (b, 0, i, 0)),
                   pl.BlockSpec((1, ps, LANES), lambda b, i: (b, i, 0))),
        compiler_params=_ARB2,
        name="dsa_prep",
    )(y, y, yx, cos_q, sin_q, cos_i, sin_i)

    t_blk = DSA_BLOCK
    qtab = pl.BlockSpec((1, t_blk, LANES), lambda b, i: (b, i, 0))
    kv_spec = pl.BlockSpec((1, DSA_KV_HEADS, l, DSA_HEAD_DIM), lambda b, i: (b, 0, 0, 0))
    return pl.pallas_call(
        functools.partial(_dsa_kernel, topk=topk, seq=l),
        out_shape=jax.ShapeDtypeStruct((bsz, l, DSA_Q_DIM), BF16),
        grid=(bsz, l // t_blk),
        in_specs=[pl.BlockSpec((1, t_blk, DSA_Q_DIM), lambda b, i: (b, i, 0)),
                  pl.BlockSpec((1, t_blk, IDX_HEADS * IDX_HEAD_DIM),
                               lambda b, i: (b, i, (DSA_Q_DIM + 2 * DSA_KV_DIM) // (IDX_HEADS * IDX_HEAD_DIM))),
                  pl.BlockSpec((1, t_blk, LANES), lambda b, i: (b, i, ki_blk)),
                  qtab, qtab, qtab, qtab,
                  kv_spec, kv_spec,
                  pl.BlockSpec((1, l, LANES), lambda b, i: (b, 0, 0))],
        out_specs=pl.BlockSpec((1, t_blk, DSA_Q_DIM), lambda b, i: (b, i, 0)),
        scratch_shapes=[pltpu.VMEM((t_blk, l), jnp.int32), pltpu.VMEM((l, t_blk), jnp.int32)],
        compiler_params=_ARB2,
        name="dsa_attn",
    )(y, y, yx, cos_q, sin_q, cos_i, sin_i, kb, vb, ki2)


def _dsa_mixer(u, positions, w_in, w_out, layer, res):
    bsz, l, dm = u.shape
    u2 = u.reshape(bsz * l, dm)
    n_main = DSA_Q_DIM + 2 * DSA_KV_DIM + IDX_HEADS * IDX_HEAD_DIM
    w_in_t = _transposed(w_in)
    y = _proj(u2, w_in_t, layer, n_out=n_main, w_t=True)
    yx = _proj_tail(u2, w_in_t, layer, n_main)
    o = _dsa_core(y.reshape(bsz, l, -1), yx.reshape(bsz, l, -1), positions)
    return _proj(o.reshape(bsz * l, DSA_Q_DIM), w_out, layer, mode="residual", res=res)


def kernel(x, p, positions, norm_mix, norm_mlp, norm_ple, norm_final, gdn_w_in, gdn_conv_w, gdn_a_log,
           gdn_dt_bias, gdn_norm, gdn_w_out, ret_w_in, ret_w_out, dsa_w_in, dsa_w_out, mlp_w_up, mlp_w_down,
           ple_w_gate, ple_w_proj):
    bsz, l, d = x.shape
    m = bsz * l
    h = x.reshape(m, d)
    for i in range(DEPTH):
        kind, j = i % N_MIXERS, i // N_MIXERS
        u = _rmsnorm(h, norm_mix[i], BF16).reshape(bsz, l, d)
        if kind == 0:
            h = _gdn_mixer(u, gdn_w_in, gdn_conv_w, j, gdn_a_log[j], gdn_dt_bias[j], gdn_norm[j], gdn_w_out, h)
        elif kind == 1:
            h = _retention_mixer(u, positions, ret_w_in, ret_w_out, j, h)
        else:
            h = _dsa_mixer(u, positions, dsa_w_in, dsa_w_out, j, h)
        u = _rmsnorm(h, norm_mlp[i], BF16)
        a = _proj(u, mlp_w_up, i, mode="relu2", out_dtype=BF16)
        h = _proj(a, mlp_w_down, i, mode="residual", res=h)
        u = _rmsnorm(h, norm_ple[i], BF16)
        h = _proj(u, ple_w_gate, i, mode="ple", res=h, p=p.reshape(DEPTH, m, -1), wp=ple_w_proj)
    return _rmsnorm(h, norm_final, x.dtype).reshape(bsz, l, d)
```

```python
import functools

import jax
import jax.numpy as jnp
from jax import lax
from jax.experimental import pallas as pl
from jax.experimental.pallas import tpu as pltpu

F32 = jnp.float32
BF16 = jnp.bfloat16

D_MODEL = 2048
DEPTH = 4
N_MIXERS = 3
NORM_EPS = 1e-6
ROPE_THETA = 10000.0

GDN_HEAD_DIM = 128
GDN_QK_HEADS = D_MODEL // 128
GDN_V_HEADS = 2 * GDN_QK_HEADS
GDN_CONV = 4
GDN_CHUNK = 64
GDN_KEY_DIM = GDN_QK_HEADS * GDN_HEAD_DIM
GDN_VAL_DIM = GDN_V_HEADS * GDN_HEAD_DIM
GDN_CONV_DIM = 2 * GDN_KEY_DIM + GDN_VAL_DIM
GDN_SUPER = 256
GDN_STEP_QK = 2

RET_HEADS = 8
RET_QK_DIM = D_MODEL
RET_V_DIM = 2 * D_MODEL
RET_QK_HEAD = RET_QK_DIM // RET_HEADS
RET_V_HEAD = RET_V_DIM // RET_HEADS
RET_CHUNK = 128
RET_STEP = 256

DSA_HEAD_DIM = 128
DSA_HEADS = D_MODEL // DSA_HEAD_DIM
DSA_KV_HEADS = 4
IDX_HEADS = 16
IDX_HEAD_DIM = 64
DSA_TOPK_MAX = 256
DSA_BLOCK = 128
DSA_KV_STEP = 512
DSA_Q_DIM = DSA_HEADS * DSA_HEAD_DIM
DSA_KV_DIM = DSA_KV_HEADS * DSA_HEAD_DIM

LANES = 128
SUBLANES = 8
VMEM_LIMIT_BYTES = 56 * 1024 * 1024

_ARB2 = pltpu.CompilerParams(dimension_semantics=("arbitrary", "arbitrary"), vmem_limit_bytes=VMEM_LIMIT_BYTES)


def _bdot(a, b):
    return jnp.dot(a.astype(BF16), b.astype(BF16), preferred_element_type=F32)


def _bdot_nt(a, b):
    return lax.dot_general(a.astype(BF16), b.astype(BF16), (((1,), (1,)), ((), ())), preferred_element_type=F32)


def _bdot_tn(a, b):
    return lax.dot_general(a.astype(BF16), b.astype(BF16), (((0,), (0,)), ((), ())), preferred_element_type=F32)


def _lane_iota(shape):
    return lax.broadcasted_iota(jnp.int32, shape, 1)


def _silu(x):
    return x * jax.nn.sigmoid(x)


def _mm_kernel(*refs, nk, mode, valid, w_t):
    x_ref, w_ref, o_ref = refs[0], refs[1], refs[-1]
    extra = refs[2:-1]
    if w_t:
        w = w_ref[...]
        if mode == "tail":
            w = jnp.where(lax.broadcasted_iota(jnp.int32, w.shape, 0) < valid, w, 0.0)
        o_ref[...] = _bdot_nt(x_ref[...], w)
        return
    if nk > 1:
        @pl.when(pl.program_id(2) == 0)
        def _():
            o_ref[...] = extra[0][...]

        half = o_ref.shape[1] // 2
        for cs in (slice(0, half), slice(half, 2 * half)):
            o_ref[:, cs] += _bdot(x_ref[...], w_ref[:, cs])
        return
    if mode == "tail":
        w = jnp.where(_lane_iota(w_ref.shape) < valid, w_ref[...], 0.0)
        o_ref[...] = _bdot(x_ref[...], w)
        return
    prod = _bdot(x_ref[...], w_ref[...])
    if mode == "plain":
        o_ref[...] = prod.astype(o_ref.dtype)
    elif mode == "relu2":
        o_ref[...] = jnp.square(jnp.maximum(prod, 0.0)).astype(o_ref.dtype)
    elif mode == "ple":
        res_ref, p_ref, wp_ref = extra
        o_ref[...] = res_ref[...] + jax.nn.sigmoid(prod) * _bdot(p_ref[...], wp_ref[...])
    else:
        o_ref[...] = extra[0][...] + prod


def _matmul(x, w, layer, *, n_out, bm, bn, bk, n_outer, col0=0, mode="plain", out_dtype=F32, res=None, p=None,
            wp=None, w_t=False):
    m, kdim = x.shape
    bm = min(bm, m)
    n_total = w.shape[1] if w_t else w.shape[2]
    assert m % bm == 0 and n_out % bn == 0 and col0 % bn == 0 and kdim % bk == 0, (x.shape, w.shape, bm, bn, bk)
    assert col0 + n_out <= n_total or (mode == "tail" and n_out == bn)
    nk = kdim // bk
    assert nk == 1 or mode == "residual"
    assert not w_t or mode in ("plain", "tail")
    cb0 = col0 // bn
    if n_outer:
        grid = (n_out // bn, m // bm, nk)
        mi, ni = (lambda j, i, k: i), (lambda j, i, k: j)
    else:
        grid = (m // bm, n_out // bn, nk)
        mi, ni = (lambda i, j, k: i), (lambda i, j, k: j)
    in_specs = [pl.BlockSpec((bm, bk), lambda a, b, k: (mi(a, b, k), k)),
                pl.BlockSpec((None, bn, bk), lambda a, b, k: (layer, cb0 + ni(a, b, k), k)) if w_t else
                pl.BlockSpec((None, bk, bn), lambda a, b, k: (layer, k, cb0 + ni(a, b, k)))]
    args = [x, w]
    o_map = lambda a, b, k: (mi(a, b, k), ni(a, b, k))
    if mode in ("residual", "ple"):
        in_specs.append(pl.BlockSpec((bm, bn), o_map))
        args.append(res)
    if mode == "ple":
        kp = p.shape[2]
        in_specs.append(pl.BlockSpec((None, bm, kp), lambda a, b, k: (layer, mi(a, b, k), 0)))
        in_specs.append(pl.BlockSpec((None, kp, bn), lambda a, b, k: (layer, 0, ni(a, b, k))))
        args += [p, wp]
    return pl.pallas_call(
        functools.partial(_mm_kernel, nk=nk, mode=mode, valid=n_total - col0, w_t=w_t),
        out_shape=jax.ShapeDtypeStruct((m, n_out), out_dtype),
        grid=grid,
        in_specs=in_specs,
        out_specs=pl.BlockSpec((bm, bn), o_map),
        compiler_params=pltpu.CompilerParams(
            dimension_semantics=("arbitrary", "arbitrary", "arbitrary"), vmem_limit_bytes=VMEM_LIMIT_BYTES),
        name="matmul_" + mode,
    )(*args)


def _proj(x, w, layer, n_out=None, **kw):
    w_t = kw.get("w_t", False)
    kdim = w.shape[2] if w_t else w.shape[1]
    n_out = (w.shape[1] if w_t else w.shape[2]) if n_out is None else n_out
    if kdim <= 2048:
        bm = 512 if kw.get("mode") == "ple" else 1024
        return _matmul(x, w, layer, n_out=n_out, bm=bm, bn=min(n_out, 1024), bk=kdim, n_outer=True, **kw)
    return _matmul(x, w, layer, n_out=n_out, bm=2048, bn=1024, bk=1024, n_outer=False, **kw)


def _proj_tail(x, w_t, layer, start):
    return _matmul(x, w_t, layer, n_out=LANES, col0=start, bm=1024, bn=LANES, bk=w_t.shape[2], n_outer=True,
                   mode="tail", w_t=True)


def _transposed(w):
    return jnp.swapaxes(w, 1, 2)


def _rmsnorm_kernel(x_ref, g_ref, o_ref):
    x = x_ref[...]
    y = x * lax.rsqrt(jnp.mean(x * x, axis=1, keepdims=True) + NORM_EPS) * g_ref[...]
    o_ref[...] = y.astype(o_ref.dtype)


def _rmsnorm(x, gain, out_dtype):
    m, d = x.shape
    bm = min(m, 512)
    return pl.pallas_call(
        _rmsnorm_kernel,
        out_shape=jax.ShapeDtypeStruct((m, d), out_dtype),
        grid=(m // bm,),
        in_specs=[pl.BlockSpec((bm, d), lambda i: (i, 0)), pl.BlockSpec((1, d), lambda i: (0, 0))],
        out_specs=pl.BlockSpec((bm, d), lambda i: (i, 0)),
        compiler_params=pltpu.CompilerParams(dimension_semantics=("arbitrary",), vmem_limit_bytes=VMEM_LIMIT_BYTES),
        name="rmsnorm",
    )(x, gain.reshape(1, d))


def _gdn_gates_kernel(gb_ref, alog_ref, dtb_ref, o_ref):
    x = gb_ref[0]
    lane = _lane_iota(x.shape)
    row = lax.broadcasted_iota(jnp.int32, x.shape, 0) % GDN_CHUNK
    g = -jnp.exp(alog_ref[...]) * jax.nn.softplus(x + dtb_ref[...])
    sh = 1
    while sh < GDN_CHUNK:
        g = g + jnp.where(row >= sh, pltpu.roll(g, sh, 0), 0.0)
        sh *= 2
    o_ref[0] = jnp.where(lane < GDN_V_HEADS, jax.nn.sigmoid(x), g)


def _conv_silu_tile(x, tail, w):
    rows8 = lax.broadcasted_iota(jnp.int32, tail.shape, 0)
    acc = x * w[GDN_CONV - 1:GDN_CONV]
    for sh in range(1, GDN_CONV):
        xs = pltpu.roll(x, sh, 0)
        head = jnp.where(rows8 < sh, pltpu.roll(tail, sh, 0), xs[:SUBLANES])
        xs = jnp.concatenate([head, xs[SUBLANES:]], axis=0)
        acc = acc + xs * w[GDN_CONV - 1 - sh:GDN_CONV - sh]
    return _silu(acc)


def _gdn_kernel(q_ref, k_ref, v_ref, z_ref, gt_ref, wq_ref, wk_ref, wv_ref, nw_ref, o_ref,
                s_ref, kn_st, wk_st, qg_st, gl_st, in_st, *, seq):
    hq = pl.program_id(1)
    r_blk, c_blk, d = GDN_SUPER, GDN_CHUNK, GDN_HEAD_DIM
    ii = lax.broadcasted_iota(jnp.int32, (r_blk, r_blk), 0)
    jj = lax.broadcasted_iota(jnp.int32, (r_blk, r_blk), 1)
    blk_masks = [(ii >> sh) == (jj >> sh) for sh in (3, 4, 5, 6)]
    blk8 = blk_masks[0]
    tri_incl = blk_masks[3] & (ii >= jj)
    diag = ii == jj
    eye = jnp.where(diag, 1.0, 0.0)
    lane = _lane_iota((r_blk, LANES))
    s_ref[...] = jnp.zeros_like(s_ref)
    wq, wk, wv, nw = wq_ref[...], wk_ref[...], wv_ref[...], nw_ref[...]
    n_steps = seq // r_blk
    rep = GDN_V_HEADS // GDN_QK_HEADS

    def fetch(s):
        r0 = pl.multiple_of(s * r_blk, r_blk)
        rt = pl.multiple_of(jnp.maximum(r0 - SUBLANES, 0), SUBLANES)
        tail = lambda ref: jnp.where(s > 0, ref[0, pl.ds(rt, SUBLANES), :], 0.0)
        rows = pl.ds(r0, r_blk)
        return (q_ref[0, rows, :], tail(q_ref), k_ref[0, rows, :], tail(k_ref), v_ref[0, rows, :], tail(v_ref),
                gt_ref[0, rows, :])

    heads = range(GDN_STEP_QK * rep)
    qk_of = lambda j: j // rep

    def solve(raw, slot):
        xq, tq, xk, tk, xv, tv, gt = raw
        qc, kc, vc = _conv_silu_tile(xq, tq, wq), _conv_silu_tile(xk, tk, wk), _conv_silu_tile(xv, tv, wv)
        qn, kn = [], []
        for i in range(GDN_STEP_QK):
            qi, ki = qc[:, i * d:(i + 1) * d], kc[:, i * d:(i + 1) * d]
            qn.append(qi * lax.rsqrt(jnp.sum(qi * qi, axis=1, keepdims=True) + NORM_EPS) * d ** -0.5)
            kn.append(ki * lax.rsqrt(jnp.sum(ki * ki, axis=1, keepdims=True) + NORM_EPS))
            kn_st[slot, i] = kn[i]
        hv0 = len(heads) * hq
        beta = [jnp.sum(jnp.where(lane == hv0 + j, gt, 0.0), axis=1, keepdims=True) for j in heads]
        g = [jnp.sum(jnp.where(lane == GDN_V_HEADS + hv0 + j, gt, 0.0), axis=1, keepdims=True) for j in heads]
        gi = [jnp.broadcast_to(g[j], (r_blk, r_blk)) for j in heads]
        dec = [jnp.exp(jnp.where(tri_incl, gi[j] - jnp.sum(jnp.where(diag, gi[j], 0.0), axis=0, keepdims=True),
                                 -jnp.inf)) for j in heads]
        kb = [kn[qk_of(j)] * beta[j] for j in heads]
        for j in heads:
            qg_st[slot, j] = qn[qk_of(j)] * jnp.exp(g[j])
            gl_st[slot, j] = gi[j][:, :d]
        qk = [_bdot_nt(qn[i], kn[i]) for i in range(GDN_STEP_QK)]
        kk = [_bdot_nt(kb[j], kn[qk_of(j)]) for j in heads]
        yield
        for j in heads:
            in_st[slot, j] = qk[qk_of(j)] * dec[j]
        a = [jnp.where(diag, 0.0, kk[j] * dec[j]) for j in heads]
        a8 = [jnp.where(blk8, a[j], 0.0) for j in heads]
        a8_2 = [_bdot(a8[j], a8[j]) for j in heads]
        yield
        inv = [eye - a8[j] for j in heads]
        t = [_bdot(inv[j], a8_2[j]) for j in heads]
        a8_4 = [_bdot(a8_2[j], a8_2[j]) for j in heads]
        yield
        inv = [inv[j] + t[j] for j in heads]
        t = [_bdot(inv[j], a8_4[j]) for j in heads]
        yield
        inv = [inv[j] + t[j] for j in heads]
        for lvl in range(3):
            off = [jnp.where(blk_masks[lvl + 1] & ~blk_masks[lvl], a[j], 0.0) for j in heads]
            t = [_bdot(off[j], inv[j]) for j in heads]
            yield
            t = [_bdot(inv[j], t[j]) for j in heads]
            yield
            inv = [inv[j] - t[j] for j in heads]
        rhs = [jnp.concatenate([vc[:, j * d:(j + 1) * d] * beta[j], kb[j] * jnp.exp(g[j])], axis=1) for j in heads]
        x = [_bdot(inv[j], rhs[j]) for j in heads]
        yield
        for j in heads:
            wk_st[slot, j] = x[j]

    def recur(s, slot):
        r0 = pl.multiple_of(s * r_blk, r_blk)
        state = [s_ref[j] for j in heads]
        for c in range(r_blk // c_blk):
            rs = slice(c * c_blk, (c + 1) * c_blk)
            gl = [gl_st[slot, j, rs, :] for j in heads]
            g_last = [gl[j][c_blk - 1:c_blk, :] for j in heads]
            both = [_bdot(jnp.concatenate([wk_st[slot, j, rs, d:], qg_st[slot, j, rs, :]], axis=0), state[j])
                    for j in heads]
            yield
            v_new = [wk_st[slot, j, rs, :d] - both[j][:c_blk] for j in heads]
            kd = [kn_st[slot, qk_of(j), rs, :] * jnp.exp(g_last[j] - gl[j]) for j in heads]
            upd = [_bdot_tn(kd[j], v_new[j]) for j in heads]
            pad = []
            for j in heads:
                parts = []
                if c > 0:
                    parts.append(jnp.zeros((c * c_blk, d), F32))
                parts.append(v_new[j])
                if (c + 1) * c_blk < r_blk:
                    parts.append(jnp.zeros((r_blk - (c + 1) * c_blk, d), F32))
                pad.append(jnp.concatenate(parts, axis=0))
            o_in = [_bdot(in_st[slot, j, rs, :], pad[j]) for j in heads]
            yield
            state = [state[j] * jnp.exp(g_last[j]) + upd[j] for j in heads]
            rows = pl.ds(r0 + c * c_blk, c_blk)
            for j in heads:
                o_c = both[j][c_blk:] + o_in[j]
                on = o_c * lax.rsqrt(jnp.mean(o_c * o_c, axis=1, keepdims=True) + NORM_EPS) * nw
                zc = z_ref[0, rows, j * d:(j + 1) * d]
                o_ref[0, rows, j * d:(j + 1) * d] = (on * _silu(zc)).astype(o_ref.dtype)
        for j in heads:
            s_ref[j] = state[j]

    def run(*gens):
        live = list(gens)
        while live:
            live = [gen for gen in live if next(gen, StopIteration) is not StopIteration]

    run(solve(fetch(0), 0))

    def body(i, carry):
        s = 2 * i
        run(solve(fetch(s + 1), 1), recur(s, 0))
        run(solve(fetch(s + 2), 0), recur(s + 1, 1))
        return carry

    lax.fori_loop(0, n_steps // 2 - 1, body, 0)
    run(solve(fetch(n_steps - 1), 1), recur(n_steps - 2, 0))
    run(recur(n_steps - 1, 1))


def _gdn_core(y, yg, conv_w, layer, a_log, dt_bias, norm_w):
    bsz, l, _ = y.shape
    assert l % (2 * GDN_SUPER) == 0
    nh, d = GDN_QK_HEADS, GDN_HEAD_DIM
    rep = GDN_V_HEADS // GDN_QK_HEADS
    lane_vec = lambda t: jnp.zeros((1, LANES), F32).at[0, GDN_V_HEADS:2 * GDN_V_HEADS].set(t.astype(F32))
    gates = pl.pallas_call(
        _gdn_gates_kernel,
        out_shape=jax.ShapeDtypeStruct((bsz, l, LANES), F32),
        grid=(bsz,),
        in_specs=[pl.BlockSpec((1, l, LANES), lambda b: (b, 0, 0)),
                  pl.BlockSpec((1, LANES), lambda b: (0, 0)), pl.BlockSpec((1, LANES), lambda b: (0, 0))],
        out_specs=pl.BlockSpec((1, l, LANES), lambda b: (b, 0, 0)),
        compiler_params=pltpu.CompilerParams(dimension_semantics=("arbitrary",), vmem_limit_bytes=VMEM_LIMIT_BYTES),
        name="gdn_gates",
    )(yg, lane_vec(a_log), lane_vec(dt_bias))
    r_blk = GDN_SUPER
    nq = GDN_STEP_QK
    qw, vw = nq * d, nq * rep * d
    nv = nq * rep
    k0, v0, z0 = GDN_KEY_DIM // qw, 2 * GDN_KEY_DIM // vw, GDN_CONV_DIM // vw
    return pl.pallas_call(
        functools.partial(_gdn_kernel, seq=l),
        out_shape=jax.ShapeDtypeStruct((bsz, l, GDN_VAL_DIM), BF16),
        grid=(bsz, nh // nq),
        in_specs=[pl.BlockSpec((1, l, qw), lambda b, h: (b, 0, h)),
                  pl.BlockSpec((1, l, qw), lambda b, h: (b, 0, k0 + h)),
                  pl.BlockSpec((1, l, vw), lambda b, h: (b, 0, v0 + h)),
                  pl.BlockSpec((1, l, vw), lambda b, h: (b, 0, z0 + h)),
                  pl.BlockSpec((1, l, LANES), lambda b, h: (b, 0, 0)),
                  pl.BlockSpec((None, GDN_CONV, qw), lambda b, h: (layer, 0, h)),
                  pl.BlockSpec((None, GDN_CONV, qw), lambda b, h: (layer, 0, k0 + h)),
                  pl.BlockSpec((None, GDN_CONV, vw), lambda b, h: (layer, 0, v0 + h)),
                  pl.BlockSpec((1, d), lambda b, h: (0, 0))],
        out_specs=pl.BlockSpec((1, l, vw), lambda b, h: (b, 0, h)),
        scratch_shapes=[pltpu.VMEM((nv, d, d), F32),
                        pltpu.VMEM((2, nq, r_blk, d), F32),
                        pltpu.VMEM((2, nv, r_blk, 2 * d), F32),
                        pltpu.VMEM((2, nv, r_blk, d), F32),
                        pltpu.VMEM((2, nv, r_blk, d), F32),
                        pltpu.VMEM((2, nv, r_blk, r_blk), F32)],
        compiler_params=_ARB2,
        name="gdn_core",
    )(y, y, y, y, gates, conv_w, conv_w, conv_w, norm_w.reshape(1, d))


def _gdn_mixer(u, w_in, conv_w, layer, a_log, dt_bias, norm_w, w_out, res):
    bsz, l, dm = u.shape
    u2 = u.reshape(bsz * l, dm)
    n_main = GDN_CONV_DIM + GDN_VAL_DIM
    w_in_t = _transposed(w_in)
    y = _proj(u2, w_in_t, layer, n_out=n_main, w_t=True)
    yg = _proj_tail(u2, w_in_t, layer, n_main)
    o = _gdn_core(y.reshape(bsz, l, -1), yg.reshape(bsz, l, -1), conv_w, layer, a_log, dt_bias, norm_w)
    return _proj(o.reshape(bsz * l, GDN_VAL_DIM), w_out, layer, mode="residual", res=res)


def _ret_kernel(q_ref, k_ref, v_ref, gate_ref, cos_ref, sin_ref, dec_ref, qd_ref, kd_ref, cd_ref, o_ref, s_ref):
    @pl.when(pl.program_id(1) == 0)
    def _():
        s_ref[...] = jnp.zeros_like(s_ref)

    half = RET_QK_HEAD // 2

    def rope(ref, rs, h, c, s):
        x1 = ref[0, rs, h * RET_QK_HEAD:h * RET_QK_HEAD + half]
        x2 = ref[0, rs, h * RET_QK_HEAD + half:(h + 1) * RET_QK_HEAD]
        return jnp.concatenate([x1 * c - x2 * s, x2 * c + x1 * s], axis=1)

    heads = range(RET_HEADS)
    vs = [slice(h * RET_V_HEAD, (h + 1) * RET_V_HEAD) for h in heads]
    for c in range(RET_STEP // RET_CHUNK):
        rs = slice(c * RET_CHUNK, (c + 1) * RET_CHUNK)
        cos, sin = cos_ref[0, rs, :], sin_ref[0, rs, :]
        qr = [rope(q_ref, rs, h, cos, sin) for h in heads]
        kr = [rope(k_ref, rs, h, cos, sin) * RET_QK_HEAD ** -0.5 for h in heads]
        sc = [_bdot_nt(qr[h], kr[h]) for h in heads]
        cross = [_bdot(qr[h] * jnp.concatenate([qd_ref[h], qd_ref[h]], axis=1), s_ref[h]) for h in heads]
        upd = [_bdot_tn(kr[h] * jnp.concatenate([kd_ref[h], kd_ref[h]], axis=1), v_ref[0, rs, vs[h]]) for h in heads]
        inner = [_bdot(sc[h] * dec_ref[h], v_ref[0, rs, vs[h]]) for h in heads]
        for h in heads:
            s_ref[h] = s_ref[h] * cd_ref[h] + upd[h]
            o = inner[h] + cross[h]
            on = o * lax.rsqrt(jnp.mean(o * o, axis=1, keepdims=True) + NORM_EPS)
            o_ref[0, rs, vs[h]] = (_silu(gate_ref[0, rs, vs[h]]) * on).astype(o_ref.dtype)


def _ret_core(y, positions):
    bsz, l, _ = y.shape
    c = RET_CHUNK
    inv_freq = ROPE_THETA ** (-jnp.arange(0, RET_QK_HEAD, 2, dtype=F32) / RET_QK_HEAD)
    ang = positions.astype(F32)[..., None] * inv_freq
    cos, sin = jnp.cos(ang), jnp.sin(ang)
    log_gamma = jnp.log1p(-jnp.exp2(-5.0 - jnp.arange(RET_HEADS, dtype=F32)))
    idx = jnp.arange(c, dtype=F32)
    rel = idx[:, None] - idx[None, :]
    inner_decay = jnp.where(rel >= 0, jnp.exp(log_gamma[:, None, None] * jnp.maximum(rel, 0.0)), 0.0)
    lanes = lambda t: jnp.broadcast_to(t[:, :, None], (RET_HEADS, c, LANES))
    q_decay = lanes(jnp.exp(log_gamma[:, None] * (idx[None, :] + 1.0)))
    k_decay = lanes(jnp.exp(log_gamma[:, None] * (c - 1.0 - idx[None, :])))
    chunk_decay = jnp.exp(log_gamma * c)
    t = RET_STEP
    full = lambda shape: pl.BlockSpec(shape, lambda b, i: (0,) * len(shape))
    return pl.pallas_call(
        _ret_kernel,
        out_shape=jax.ShapeDtypeStruct((bsz, l, RET_V_DIM), BF16),
        grid=(bsz, l // t),
        in_specs=[pl.BlockSpec((1, t, RET_QK_DIM), lambda b, i: (b, i, 0)),
                  pl.BlockSpec((1, t, RET_QK_DIM), lambda b, i: (b, i, 1)),
                  pl.BlockSpec((1, t, RET_V_DIM), lambda b, i: (b, i, 1)),
                  pl.BlockSpec((1, t, RET_V_DIM), lambda b, i: (b, i, 2)),
                  pl.BlockSpec((1, t, LANES), lambda b, i: (b, i, 0)),
                  pl.BlockSpec((1, t, LANES), lambda b, i: (b, i, 0)),
                  full((RET_HEADS, c, c)), full((RET_HEADS, c, LANES)), full((RET_HEADS, c, LANES)),
                  pl.BlockSpec(memory_space=pltpu.SMEM)],
        out_specs=pl.BlockSpec((1, t, RET_V_DIM), lambda b, i: (b, i, 0)),
        scratch_shapes=[pltpu.VMEM((RET_HEADS, RET_QK_HEAD, RET_V_HEAD), F32)],
        compiler_params=_ARB2,
        name="ret_core",
    )(y, y, y, y, cos, sin, inner_decay, q_decay, k_decay, chunk_decay)


def _retention_mixer(u, positions, w_in, w_out, layer, res):
    bsz, l, dm = u.shape
    y = _proj(u.reshape(bsz * l, dm), w_in, layer)
    o = _ret_core(y.reshape(bsz, l, -1), positions)
    return _proj(o.reshape(bsz * l, RET_V_DIM), w_out, layer, mode="residual", res=res)


def _rope_tables(positions, dim):
    inv_freq = ROPE_THETA ** (-jnp.arange(0, dim, 2, dtype=F32) / dim)
    ang = positions.astype(F32)[..., None] * inv_freq
    c, s = jnp.cos(ang), jnp.sin(ang)
    reps = LANES // dim
    cos_full = jnp.tile(jnp.concatenate([c, c], axis=-1), (1, 1, reps))
    sin_signed = jnp.tile(jnp.concatenate([-s, s], axis=-1), (1, 1, reps))
    return cos_full, sin_signed


def _rope128(x, cos, sin_signed):
    return x * cos + pltpu.roll(x, 64, 1) * sin_signed


def _rope64(x, cos, sin_signed, low_half):
    partner = jnp.where(low_half, pltpu.roll(x, 96, 1), pltpu.roll(x, 32, 1))
    return x * cos + partner * sin_signed


def _dsa_prep_kernel(k_ref, v_ref, ki_ref, ck_ref, sk_ref, cik_ref, sik_ref, kb_ref, vb_ref, ki2_ref):
    ck, sk = ck_ref[0], sk_ref[0]
    for g in range(DSA_KV_HEADS):
        sl = slice(g * DSA_HEAD_DIM, (g + 1) * DSA_HEAD_DIM)
        kb_ref[0, g] = _rope128(k_ref[0, :, sl], ck, sk).astype(BF16)
        vb_ref[0, g] = v_ref[0, :, sl].astype(BF16)
    x = ki_ref[0]
    lane = _lane_iota(x.shape)
    xr = _rope64(x, cik_ref[0], sik_ref[0], (lane % IDX_HEAD_DIM) < IDX_HEAD_DIM // 2)
    ki2_ref[0] = jnp.where(lane < IDX_HEAD_DIM, xr, pltpu.roll(xr, IDX_HEAD_DIM, 1)).astype(BF16)


def _dsa_block(s_eff, q_ref, qi_ref, wi_ref, cq_ref, sq_ref, ciq_ref, siq_ref, kb_ref, vb_ref, ki2_ref, o_ref,
               key_ref, keyt_ref, *, topk):
    t_blk = DSA_BLOCK
    qb = pl.program_id(1)
    lane = _lane_iota((t_blk, LANES))
    low_half = (lane % IDX_HEAD_DIM) < IDX_HEAD_DIM // 2
    lo64 = lane < IDX_HEAD_DIM

    wi = wi_ref[0] * (IDX_HEADS ** -0.5 * IDX_HEAD_DIM ** -0.5)
    ciq, siq = ciq_ref[0], siq_ref[0]
    ki2 = ki2_ref[0, :s_eff, :]
    score = jnp.zeros((t_blk, s_eff), F32)
    for quad in range(IDX_HEADS // 4):
        rows = []
        for pr in (2 * quad, 2 * quad + 1):
            xr = _rope64(qi_ref[0, :, pr * LANES:(pr + 1) * LANES], ciq, siq, low_half)
            rows.append(jnp.where(lo64, xr, 0.0))
            rows.append(jnp.where(lo64, 0.0, xr))
        s4 = _bdot_nt(jnp.concatenate(rows, axis=0), ki2)
        for j in range(4):
            hh = 4 * quad + j
            w_col = wi[:, IDX_HEAD_DIM + hh:IDX_HEAD_DIM + hh + 1]
            score = score + w_col * jnp.maximum(s4[j * t_blk:(j + 1) * t_blk], 0.0)

    t_idx = qb * t_blk + lax.broadcasted_iota(jnp.int32, (t_blk, 1), 0)
    s_idx = _lane_iota((t_blk, s_eff))
    score = jnp.where(s_idx <= t_idx, score, -jnp.inf)
    bits = lax.bitcast_convert_type(score, jnp.int32)
    keys = bits ^ ((bits >> 31) & jnp.int32(0x7FFFFFFF))
    key_ref[:, :s_eff] = keys
    keyt_ref[:s_eff, :] = keys.T

    kk = jnp.minimum(qb * t_blk + _lane_iota((1, t_blk)) + 1, topk).astype(F32)
    int_min = jnp.int32(-2 ** 31)

    n_chain = 8
    rows = s_eff // n_chain

    def bisect(i, t_u):
        cand_u = t_u | lax.shift_left(jnp.int32(1), 31 - i)
        cand = jnp.broadcast_to(cand_u ^ int_min, (SUBLANES, t_blk))
        acc = [jnp.zeros((SUBLANES, t_blk), F32) for _ in range(n_chain)]
        for r in range(0, rows, SUBLANES):
            for c in range(n_chain):
                hit = keyt_ref[c * rows + r:c * rows + r + SUBLANES, :] >= cand
                acc[c] = acc[c] + jnp.where(hit, 1.0, 0.0)
        while len(acc) > 1:
            acc = [acc[k] + acc[k + 1] for k in range(0, len(acc), 2)]
        cnt = jnp.sum(acc[0], axis=0, keepdims=True)
        return jnp.where(cnt >= kk, cand_u, t_u)

    t_u = lax.fori_loop(0, 32, bisect, jnp.zeros((1, t_blk), jnp.int32))
    thr_row = t_u ^ int_min

    kt = keyt_ref[:s_eff, :]
    n_ge = jnp.sum(jnp.where(kt >= thr_row, 1.0, 0.0), axis=0, keepdims=True)

    @pl.when(jnp.max(n_ge - kk) > 0.0)
    def _():
        tied = jnp.where(kt == thr_row, 1.0, 0.0)
        need = kk - jnp.sum(jnp.where(kt > thr_row, 1.0, 0.0), axis=0, keepdims=True)
        row = lax.broadcasted_iota(jnp.int32, kt.shape, 0)
        before = tied
        sh = 1
        while sh < s_eff:
            before = before + jnp.where(row >= sh, pltpu.roll(before, sh, 0), 0.0)
            sh *= 2
        surplus = jnp.where((tied > 0.0) & (before > need), 1.0, 0.0)
        key_ref[:, :s_eff] = jnp.where(surplus.T > 0.0, int_min, key_ref[:, :s_eff])

    thr = jnp.broadcast_to(thr_row, (t_blk, t_blk)).T[:, :1]
    sel = key_ref[:, :s_eff] >= thr

    cq, sq = cq_ref[0], sq_ref[0]
    group = DSA_HEADS // DSA_KV_HEADS
    def group_logits(g):
        qs = []
        for j in range(group):
            hh = group * g + j
            qs.append(_rope128(q_ref[0, :, hh * DSA_HEAD_DIM:(hh + 1) * DSA_HEAD_DIM], cq, sq))
        return _bdot_nt(jnp.concatenate(qs, axis=0), kb_ref[0, g, :s_eff, :]) * DSA_HEAD_DIM ** -0.5

    next_logits = group_logits(0)
    for g in range(DSA_KV_HEADS):
        logits = next_logits
        if g + 1 < DSA_KV_HEADS:
            next_logits = group_logits(g + 1)
        ps, dens = [], []
        for j in range(group):
            lj = jnp.where(sel, logits[j * t_blk:(j + 1) * t_blk], -jnp.inf)
            pj = jnp.exp(lj - jnp.max(lj, axis=1, keepdims=True))
            dens.append(jnp.sum(pj, axis=1, keepdims=True))
            ps.append(pj)
        pv = _bdot(jnp.concatenate(ps, axis=0), vb_ref[0, g, :s_eff, :])
        for j in range(group):
            hh = group * g + j
            o_ref[0, :, hh * DSA_HEAD_DIM:(hh + 1) * DSA_HEAD_DIM] = (
                pv[j * t_blk:(j + 1) * t_blk] / dens[j]).astype(o_ref.dtype)


def _dsa_kernel(q_ref, qi_ref, wi_ref, cq_ref, sq_ref, ciq_ref, siq_ref, kb_ref, vb_ref, ki2_ref, o_ref, key_ref,
                keyt_ref, *, topk, seq):
    qb = pl.program_id(1)
    blocks_per_step = DSA_KV_STEP // DSA_BLOCK
    for br in range(seq // DSA_KV_STEP):
        @pl.when(qb // blocks_per_step == br)
        def _():
            _dsa_block((br + 1) * DSA_KV_STEP, q_ref, qi_ref, wi_ref, cq_ref, sq_ref, ciq_ref, siq_ref, kb_ref,
                       vb_ref, ki2_ref, o_ref, key_ref, keyt_ref, topk=topk)


def _dsa_core(y, yx, positions):
    bsz, l, _ = y.shape
    topk = min(DSA_TOPK_MAX, l // 4)
    cos_q, sin_q = _rope_tables(positions, DSA_HEAD_DIM)
    cos_i, sin_i = _rope_tables(positions, IDX_HEAD_DIM)
    k_blk = DSA_Q_DIM // DSA_KV_DIM
    ki_blk = 0
    ps = DSA_KV_STEP
    tab = pl.BlockSpec((1, ps, LANES), lambda b, i: (b, i, 0))
    kb, vb, ki2 = pl.pallas_call(
        _dsa_prep_kernel,
        out_shape=(jax.ShapeDtypeStruct((bsz, DSA_KV_HEADS, l, DSA_HEAD_DIM), BF16),
                   jax.ShapeDtypeStruct((bsz, DSA_KV_HEADS, l, DSA_HEAD_DIM), BF16),
                   jax.ShapeDtypeStruct((bsz, l, LANES), BF16)),
        grid=(bsz, l // ps),
        in_specs=[pl.BlockSpec((1, ps, DSA_KV_DIM), lambda b, i: (b, i, k_blk)),
                  pl.BlockSpec((1, ps, DSA_KV_DIM), lambda b, i: (b, i, k_blk + 1)),
                  pl.BlockSpec((1, ps, LANES), lambda b, i: (b, i, ki_blk)),
                  tab, tab, tab, tab],
        out_specs=(pl.BlockSpec((1, DSA_KV_HEADS, ps, DSA_HEAD_DIM), lambda b, i: (b, 0, i, 0)),
                   pl.BlockSpec((1, DSA_KV_HEADS, ps, DSA_HEAD_DIM), lambda b, i: (b, 0, i, 0)),
                   pl.BlockSpec((1, ps, LANES), lambda b, i: (b, i, 0))),
        compiler_params=_ARB2,
        name="dsa_prep",
    )(y, y, yx, cos_q, sin_q, cos_i, sin_i)

    t_blk = DSA_BLOCK
    qtab = pl.BlockSpec((1, t_blk, LANES), lambda b, i: (b, i, 0))
    kv_spec = pl.BlockSpec((1, DSA_KV_HEADS, l, DSA_HEAD_DIM), lambda b, i: (b, 0, 0, 0))
    return pl.pallas_call(
        functools.partial(_dsa_kernel, topk=topk, seq=l),
        out_shape=jax.ShapeDtypeStruct((bsz, l, DSA_Q_DIM), BF16),
        grid=(bsz, l // t_blk),
        in_specs=[pl.BlockSpec((1, t_blk, DSA_Q_DIM), lambda b, i: (b, i, 0)),
                  pl.BlockSpec((1, t_blk, IDX_HEADS * IDX_HEAD_DIM),
                               lambda b, i: (b, i, (DSA_Q_DIM + 2 * DSA_KV_DIM) // (IDX_HEADS * IDX_HEAD_DIM))),
                  pl.BlockSpec((1, t_blk, LANES), lambda b, i: (b, i, ki_blk)),
                  qtab, qtab, qtab, qtab,
                  kv_spec, kv_spec,
                  pl.BlockSpec((1, l, LANES), lambda b, i: (b, 0, 0))],
        out_specs=pl.BlockSpec((1, t_blk, DSA_Q_DIM), lambda b, i: (b, i, 0)),
        scratch_shapes=[pltpu.VMEM((t_blk, l), jnp.int32), pltpu.VMEM((l, t_blk), jnp.int32)],
        compiler_params=_ARB2,
        name="dsa_attn",
    )(y, y, yx, cos_q, sin_q, cos_i, sin_i, kb, vb, ki2)


def _dsa_mixer(u, positions, w_in, w_out, layer, res):
    bsz, l, dm = u.shape
    u2 = u.reshape(bsz * l, dm)
    n_main = DSA_Q_DIM + 2 * DSA_KV_DIM + IDX_HEADS * IDX_HEAD_DIM
    w_in_t = _transposed(w_in)
    y = _proj(u2, w_in_t, layer, n_out=n_main, w_t=True)
    yx = _proj_tail(u2, w_in_t, layer, n_main)
    o = _dsa_core(y.reshape(bsz, l, -1), yx.reshape(bsz, l, -1), positions)
    return _proj(o.reshape(bsz * l, DSA_Q_DIM), w_out, layer, mode="residual", res=res)


def kernel(x, p, positions, norm_mix, norm_mlp, norm_ple, norm_final, gdn_w_in, gdn_conv_w, gdn_a_log,
           gdn_dt_bias, gdn_norm, gdn_w_out, ret_w_in, ret_w_out, dsa_w_in, dsa_w_out, mlp_w_up, mlp_w_down,
           ple_w_gate, ple_w_proj):
    bsz, l, d = x.shape
    m = bsz * l
    h = x.reshape(m, d)
    for i in range(DEPTH):
        kind, j = i % N_MIXERS, i // N_MIXERS
        u = _rmsnorm(h, norm_mix[i], BF16).reshape(bsz, l, d)
        if kind == 0:
            h = _gdn_mixer(u, gdn_w_in, gdn_conv_w, j, gdn_a_log[j], gdn_dt_bias[j], gdn_norm[j], gdn_w_out, h)
        elif kind == 1:
            h = _retention_mixer(u, positions, ret_w_in, ret_w_out, j, h)
        else:
            h = _dsa_mixer(u, positions, dsa_w_in, dsa_w_out, j, h)
        u = _rmsnorm(h, norm_mlp[i], BF16)
        a = _proj(u, mlp_w_up, i, mode="relu2", out_dtype=BF16)
        h = _proj(a, mlp_w_down, i, mode="residual", res=h)
        u = _rmsnorm(h, norm_ple[i], BF16)
        h = _proj(u, ple_w_gate, i, mode="ple", res=h, p=p.reshape(DEPTH, m, -1), wp=ple_w_proj)
    return _rmsnorm(h, norm_final, x.dtype).reshape(bsz, l, d)
```

```python
import functools

import jax
import jax.numpy as jnp
from jax import lax
from jax.experimental import pallas as pl
from jax.experimental.pallas import tpu as pltpu

F32 = jnp.float32
BF16 = jnp.bfloat16

D_MODEL = 2048
DEPTH = 4
N_MIXERS = 3
NORM_EPS = 1e-6
ROPE_THETA = 10000.0

GDN_HEAD_DIM = 128
GDN_QK_HEADS = D_MODEL // 128
GDN_V_HEADS = 2 * GDN_QK_HEADS
GDN_CONV = 4
GDN_CHUNK = 64
GDN_KEY_DIM = GDN_QK_HEADS * GDN_HEAD_DIM
GDN_VAL_DIM = GDN_V_HEADS * GDN_HEAD_DIM
GDN_CONV_DIM = 2 * GDN_KEY_DIM + GDN_VAL_DIM
GDN_SUPER = 256
GDN_STEP_QK = 2

RET_HEADS = 8
RET_QK_DIM = D_MODEL
RET_V_DIM = 2 * D_MODEL
RET_QK_HEAD = RET_QK_DIM // RET_HEADS
RET_V_HEAD = RET_V_DIM // RET_HEADS
RET_CHUNK = 128
RET_STEP = 256

DSA_HEAD_DIM = 128
DSA_HEADS = D_MODEL // DSA_HEAD_DIM
DSA_KV_HEADS = 4
IDX_HEADS = 16
IDX_HEAD_DIM = 64
DSA_TOPK_MAX = 256
DSA_BLOCK = 128
DSA_KV_STEP = 256
DSA_Q_DIM = DSA_HEADS * DSA_HEAD_DIM
DSA_KV_DIM = DSA_KV_HEADS * DSA_HEAD_DIM

LANES = 128
SUBLANES = 8
VMEM_LIMIT_BYTES = 56 * 1024 * 1024

_ARB2 = pltpu.CompilerParams(dimension_semantics=("arbitrary", "arbitrary"), vmem_limit_bytes=VMEM_LIMIT_BYTES)


def _bdot(a, b):
    return jnp.dot(a.astype(BF16), b.astype(BF16), preferred_element_type=F32)


def _bdot_nt(a, b):
    return lax.dot_general(a.astype(BF16), b.astype(BF16), (((1,), (1,)), ((), ())), preferred_element_type=F32)


def _bdot_tn(a, b):
    return lax.dot_general(a.astype(BF16), b.astype(BF16), (((0,), (0,)), ((), ())), preferred_element_type=F32)


def _lane_iota(shape):
    return lax.broadcasted_iota(jnp.int32, shape, 1)


def _silu(x):
    return x * jax.nn.sigmoid(x)


def _mm_kernel(*refs, nk, mode, valid, w_t):
    x_ref, w_ref, o_ref = refs[0], refs[1], refs[-1]
    extra = refs[2:-1]
    if w_t:
        w = w_ref[...]
        if mode == "tail":
            w = jnp.where(lax.broadcasted_iota(jnp.int32, w.shape, 0) < valid, w, 0.0)
        o_ref[...] = _bdot_nt(x_ref[...], w)
        return
    if nk > 1:
        @pl.when(pl.program_id(2) == 0)
        def _():
            o_ref[...] = extra[0][...]

        half = o_ref.shape[1] // 2
        for cs in (slice(0, half), slice(half, 2 * half)):
            o_ref[:, cs] += _bdot(x_ref[...], w_ref[:, cs])
        return
    if mode == "tail":
        w = jnp.where(_lane_iota(w_ref.shape) < valid, w_ref[...], 0.0)
        o_ref[...] = _bdot(x_ref[...], w)
        return
    prod = _bdot(x_ref[...], w_ref[...])
    if mode == "plain":
        o_ref[...] = prod.astype(o_ref.dtype)
    elif mode == "relu2":
        o_ref[...] = jnp.square(jnp.maximum(prod, 0.0)).astype(o_ref.dtype)
    elif mode == "ple":
        res_ref, p_ref, wp_ref = extra
        o_ref[...] = res_ref[...] + jax.nn.sigmoid(prod) * _bdot(p_ref[...], wp_ref[...])
    else:
        o_ref[...] = extra[0][...] + prod


def _matmul(x, w, layer, *, n_out, bm, bn, bk, n_outer, col0=0, mode="plain", out_dtype=F32, res=None, p=None,
            wp=None, w_t=False):
    m, kdim = x.shape
    bm = min(bm, m)
    n_total = w.shape[1] if w_t else w.shape[2]
    assert m % bm == 0 and n_out % bn == 0 and col0 % bn == 0 and kdim % bk == 0, (x.shape, w.shape, bm, bn, bk)
    assert col0 + n_out <= n_total or (mode == "tail" and n_out == bn)
    nk = kdim // bk
    assert nk == 1 or mode == "residual"
    assert not w_t or mode in ("plain", "tail")
    cb0 = col0 // bn
    if n_outer:
        grid = (n_out // bn, m // bm, nk)
        mi, ni = (lambda j, i, k: i), (lambda j, i, k: j)
    else:
        grid = (m // bm, n_out // bn, nk)
        mi, ni = (lambda i, j, k: i), (lambda i, j, k: j)
    in_specs = [pl.BlockSpec((bm, bk), lambda a, b, k: (mi(a, b, k), k)),
                pl.BlockSpec((None, bn, bk), lambda a, b, k: (layer, cb0 + ni(a, b, k), k)) if w_t else
                pl.BlockSpec((None, bk, bn), lambda a, b, k: (layer, k, cb0 + ni(a, b, k)))]
    args = [x, w]
    o_map = lambda a, b, k: (mi(a, b, k), ni(a, b, k))
    if mode in ("residual", "ple"):
        in_specs.append(pl.BlockSpec((bm, bn), o_map))
        args.append(res)
    if mode == "ple":
        kp = p.shape[2]
        in_specs.append(pl.BlockSpec((None, bm, kp), lambda a, b, k: (layer, mi(a, b, k), 0)))
        in_specs.append(pl.BlockSpec((None, kp, bn), lambda a, b, k: (layer, 0, ni(a, b, k))))
        args += [p, wp]
    return pl.pallas_call(
        functools.partial(_mm_kernel, nk=nk, mode=mode, valid=n_total - col0, w_t=w_t),
        out_shape=jax.ShapeDtypeStruct((m, n_out), out_dtype),
        grid=grid,
        in_specs=in_specs,
        out_specs=pl.BlockSpec((bm, bn), o_map),
        compiler_params=pltpu.CompilerParams(
            dimension_semantics=("arbitrary", "arbitrary", "arbitrary"), vmem_limit_bytes=VMEM_LIMIT_BYTES),
        name="matmul_" + mode,
    )(*args)


def _proj(x, w, layer, n_out=None, **kw):
    w_t = kw.get("w_t", False)
    kdim = w.shape[2] if w_t else w.shape[1]
    n_out = (w.shape[1] if w_t else w.shape[2]) if n_out is None else n_out
    if kdim <= 2048:
        bm = 512 if kw.get("mode") == "ple" else 1024
        return _matmul(x, w, layer, n_out=n_out, bm=bm, bn=min(n_out, 1024), bk=kdim, n_outer=True, **kw)
    return _matmul(x, w, layer, n_out=n_out, bm=2048, bn=1024, bk=1024, n_outer=False, **kw)


def _proj_tail(x, w_t, layer, start):
    return _matmul(x, w_t, layer, n_out=LANES, col0=start, bm=1024, bn=LANES, bk=w_t.shape[2], n_outer=True,
                   mode="tail", w_t=True)


def _transposed(w):
    return jnp.swapaxes(w, 1, 2)


def _rmsnorm_kernel(x_ref, g_ref, o_ref):
    x = x_ref[...]
    y = x * lax.rsqrt(jnp.mean(x * x, axis=1, keepdims=True) + NORM_EPS) * g_ref[...]
    o_ref[...] = y.astype(o_ref.dtype)


def _rmsnorm(x, gain, out_dtype):
    m, d = x.shape
    bm = min(m, 512)
    return pl.pallas_call(
        _rmsnorm_kernel,
        out_shape=jax.ShapeDtypeStruct((m, d), out_dtype),
        grid=(m // bm,),
        in_specs=[pl.BlockSpec((bm, d), lambda i: (i, 0)), pl.BlockSpec((1, d), lambda i: (0, 0))],
        out_specs=pl.BlockSpec((bm, d), lambda i: (i, 0)),
        compiler_params=pltpu.CompilerParams(dimension_semantics=("arbitrary",), vmem_limit_bytes=VMEM_LIMIT_BYTES),
        name="rmsnorm",
    )(x, gain.reshape(1, d))


def _gdn_gates_kernel(gb_ref, alog_ref, dtb_ref, o_ref):
    x = gb_ref[0]
    lane = _lane_iota(x.shape)
    row = lax.broadcasted_iota(jnp.int32, x.shape, 0) % GDN_CHUNK
    g = -jnp.exp(alog_ref[...]) * jax.nn.softplus(x + dtb_ref[...])
    sh = 1
    while sh < GDN_CHUNK:
        g = g + jnp.where(row >= sh, pltpu.roll(g, sh, 0), 0.0)
        sh *= 2
    o_ref[0] = jnp.where(lane < GDN_V_HEADS, jax.nn.sigmoid(x), g)


def _conv_silu_tile(x, tail, w):
    rows8 = lax.broadcasted_iota(jnp.int32, tail.shape, 0)
    acc = x * w[GDN_CONV - 1:GDN_CONV]
    for sh in range(1, GDN_CONV):
        xs = pltpu.roll(x, sh, 0)
        head = jnp.where(rows8 < sh, pltpu.roll(tail, sh, 0), xs[:SUBLANES])
        xs = jnp.concatenate([head, xs[SUBLANES:]], axis=0)
        acc = acc + xs * w[GDN_CONV - 1 - sh:GDN_CONV - sh]
    return _silu(acc)


def _gdn_kernel(q_ref, k_ref, v_ref, z_ref, gt_ref, wq_ref, wk_ref, wv_ref, nw_ref, o_ref,
                s_ref, kn_st, wk_st, qg_st, gl_st, in_st, *, seq):
    hq = pl.program_id(1)
    r_blk, c_blk, d = GDN_SUPER, GDN_CHUNK, GDN_HEAD_DIM
    ii = lax.broadcasted_iota(jnp.int32, (r_blk, r_blk), 0)
    jj = lax.broadcasted_iota(jnp.int32, (r_blk, r_blk), 1)
    blk_masks = [(ii >> sh) == (jj >> sh) for sh in (3, 4, 5, 6)]
    blk8 = blk_masks[0]
    tri_incl = blk_masks[3] & (ii >= jj)
    diag = ii == jj
    eye = jnp.where(diag, 1.0, 0.0)
    lane = _lane_iota((r_blk, LANES))
    s_ref[...] = jnp.zeros_like(s_ref)
    wq, wk, wv, nw = wq_ref[...], wk_ref[...], wv_ref[...], nw_ref[...]
    n_steps = seq // r_blk
    rep = GDN_V_HEADS // GDN_QK_HEADS

    def fetch(s):
        r0 = pl.multiple_of(s * r_blk, r_blk)
        rt = pl.multiple_of(jnp.maximum(r0 - SUBLANES, 0), SUBLANES)
        tail = lambda ref: jnp.where(s > 0, ref[0, pl.ds(rt, SUBLANES), :], 0.0)
        rows = pl.ds(r0, r_blk)
        return (q_ref[0, rows, :], tail(q_ref), k_ref[0, rows, :], tail(k_ref), v_ref[0, rows, :], tail(v_ref),
                gt_ref[0, rows, :])

    heads = range(GDN_STEP_QK * rep)
    qk_of = lambda j: j // rep

    def solve(raw, slot):
        xq, tq, xk, tk, xv, tv, gt = raw
        qc, kc, vc = _conv_silu_tile(xq, tq, wq), _conv_silu_tile(xk, tk, wk), _conv_silu_tile(xv, tv, wv)
        qn, kn = [], []
        for i in range(GDN_STEP_QK):
            qi, ki = qc[:, i * d:(i + 1) * d], kc[:, i * d:(i + 1) * d]
            qn.append(qi * lax.rsqrt(jnp.sum(qi * qi, axis=1, keepdims=True) + NORM_EPS) * d ** -0.5)
            kn.append(ki * lax.rsqrt(jnp.sum(ki * ki, axis=1, keepdims=True) + NORM_EPS))
            kn_st[slot, i] = kn[i]
        hv0 = len(heads) * hq
        beta = [jnp.sum(jnp.where(lane == hv0 + j, gt, 0.0), axis=1, keepdims=True) for j in heads]
        g = [jnp.sum(jnp.where(lane == GDN_V_HEADS + hv0 + j, gt, 0.0), axis=1, keepdims=True) for j in heads]
        gi = [jnp.broadcast_to(g[j], (r_blk, r_blk)) for j in heads]
        dec = [jnp.exp(jnp.where(tri_incl, gi[j] - jnp.sum(jnp.where(diag, gi[j], 0.0), axis=0, keepdims=True),
                                 -jnp.inf)) for j in heads]
        kb = [kn[qk_of(j)] * beta[j] for j in heads]
        for j in heads:
            qg_st[slot, j] = qn[qk_of(j)] * jnp.exp(g[j])
            gl_st[slot, j] = gi[j][:, :d]
        qk = [_bdot_nt(qn[i], kn[i]) for i in range(GDN_STEP_QK)]
        kk = [_bdot_nt(kb[j], kn[qk_of(j)]) for j in heads]
        yield
        for j in heads:
            in_st[slot, j] = qk[qk_of(j)] * dec[j]
        a = [jnp.where(diag, 0.0, kk[j] * dec[j]) for j in heads]
        a8 = [jnp.where(blk8, a[j], 0.0) for j in heads]
        a8_2 = [_bdot(a8[j], a8[j]) for j in heads]
        yield
        inv = [eye - a8[j] for j in heads]
        t = [_bdot(inv[j], a8_2[j]) for j in heads]
        a8_4 = [_bdot(a8_2[j], a8_2[j]) for j in heads]
        yield
        inv = [inv[j] + t[j] for j in heads]
        t = [_bdot(inv[j], a8_4[j]) for j in heads]
        yield
        inv = [inv[j] + t[j] for j in heads]
        for lvl in range(3):
            off = [jnp.where(blk_masks[lvl + 1] & ~blk_masks[lvl], a[j], 0.0) for j in heads]
            t = [_bdot(off[j], inv[j]) for j in heads]
            yield
            t = [_bdot(inv[j], t[j]) for j in heads]
            yield
            inv = [inv[j] - t[j] for j in heads]
        rhs = [jnp.concatenate([vc[:, j * d:(j + 1) * d] * beta[j], kb[j] * jnp.exp(g[j])], axis=1) for j in heads]
        x = [_bdot(inv[j], rhs[j]) for j in heads]
        yield
        for j in heads:
            wk_st[slot, j] = x[j]

    def recur(s, slot):
        r0 = pl.multiple_of(s * r_blk, r_blk)
        state = [s_ref[j] for j in heads]
        for c in range(r_blk // c_blk):
            rs = slice(c * c_blk, (c + 1) * c_blk)
            gl = [gl_st[slot, j, rs, :] for j in heads]
            g_last = [gl[j][c_blk - 1:c_blk, :] for j in heads]
            both = [_bdot(jnp.concatenate([wk_st[slot, j, rs, d:], qg_st[slot, j, rs, :]], axis=0), state[j])
                    for j in heads]
            yield
            v_new = [wk_st[slot, j, rs, :d] - both[j][:c_blk] for j in heads]
            kd = [kn_st[slot, qk_of(j), rs, :] * jnp.exp(g_last[j] - gl[j]) for j in heads]
            upd = [_bdot_tn(kd[j], v_new[j]) for j in heads]
            pad = []
            for j in heads:
                parts = []
                if c > 0:
                    parts.append(jnp.zeros((c * c_blk, d), F32))
                parts.append(v_new[j])
                if (c + 1) * c_blk < r_blk:
                    parts.append(jnp.zeros((r_blk - (c + 1) * c_blk, d), F32))
                pad.append(jnp.concatenate(parts, axis=0))
            o_in = [_bdot(in_st[slot, j, rs, :], pad[j]) for j in heads]
            yield
            state = [state[j] * jnp.exp(g_last[j]) + upd[j] for j in heads]
            rows = pl.ds(r0 + c * c_blk, c_blk)
            for j in heads:
                o_c = both[j][c_blk:] + o_in[j]
                on = o_c * lax.rsqrt(jnp.mean(o_c * o_c, axis=1, keepdims=True) + NORM_EPS) * nw
                zc = z_ref[0, rows, j * d:(j + 1) * d]
                o_ref[0, rows, j * d:(j + 1) * d] = (on * _silu(zc)).astype(o_ref.dtype)
        for j in heads:
            s_ref[j] = state[j]

    def run(*gens):
        live = list(gens)
        while live:
            live = [gen for gen in live if next(gen, StopIteration) is not StopIteration]

    run(solve(fetch(0), 0))

    def body(i, carry):
        s = 2 * i
        run(solve(fetch(s + 1), 1), recur(s, 0))
        run(solve(fetch(s + 2), 0), recur(s + 1, 1))
        return carry

    lax.fori_loop(0, n_steps // 2 - 1, body, 0)
    run(solve(fetch(n_steps - 1), 1), recur(n_steps - 2, 0))
    run(recur(n_steps - 1, 1))


def _gdn_core(y, yg, conv_w, layer, a_log, dt_bias, norm_w):
    bsz, l, _ = y.shape
    assert l % (2 * GDN_SUPER) == 0
    nh, d = GDN_QK_HEADS, GDN_HEAD_DIM
    rep = GDN_V_HEADS // GDN_QK_HEADS
    lane_vec = lambda t: jnp.zeros((1, LANES), F32).at[0, GDN_V_HEADS:2 * GDN_V_HEADS].set(t.astype(F32))
    gates = pl.pallas_call(
        _gdn_gates_kernel,
        out_shape=jax.ShapeDtypeStruct((bsz, l, LANES), F32),
        grid=(bsz,),
        in_specs=[pl.BlockSpec((1, l, LANES), lambda b: (b, 0, 0)),
                  pl.BlockSpec((1, LANES), lambda b: (0, 0)), pl.BlockSpec((1, LANES), lambda b: (0, 0))],
        out_specs=pl.BlockSpec((1, l, LANES), lambda b: (b, 0, 0)),
        compiler_params=pltpu.CompilerParams(dimension_semantics=("arbitrary",), vmem_limit_bytes=VMEM_LIMIT_BYTES),
        name="gdn_gates",
    )(yg, lane_vec(a_log), lane_vec(dt_bias))
    r_blk = GDN_SUPER
    nq = GDN_STEP_QK
    qw, vw = nq * d, nq * rep * d
    nv = nq * rep
    k0, v0, z0 = GDN_KEY_DIM // qw, 2 * GDN_KEY_DIM // vw, GDN_CONV_DIM // vw
    return pl.pallas_call(
        functools.partial(_gdn_kernel, seq=l),
        out_shape=jax.ShapeDtypeStruct((bsz, l, GDN_VAL_DIM), BF16),
        grid=(bsz, nh // nq),
        in_specs=[pl.BlockSpec((1, l, qw), lambda b, h: (b, 0, h)),
                  pl.BlockSpec((1, l, qw), lambda b, h: (b, 0, k0 + h)),
                  pl.BlockSpec((1, l, vw), lambda b, h: (b, 0, v0 + h)),
                  pl.BlockSpec((1, l, vw), lambda b, h: (b, 0, z0 + h)),
                  pl.BlockSpec((1, l, LANES), lambda b, h: (b, 0, 0)),
                  pl.BlockSpec((None, GDN_CONV, qw), lambda b, h: (layer, 0, h)),
                  pl.BlockSpec((None, GDN_CONV, qw), lambda b, h: (layer, 0, k0 + h)),
                  pl.BlockSpec((None, GDN_CONV, vw), lambda b, h: (layer, 0, v0 + h)),
                  pl.BlockSpec((1, d), lambda b, h: (0, 0))],
        out_specs=pl.BlockSpec((1, l, vw), lambda b, h: (b, 0, h)),
        scratch_shapes=[pltpu.VMEM((nv, d, d), F32),
                        pltpu.VMEM((2, nq, r_blk, d), F32),
                        pltpu.VMEM((2, nv, r_blk, 2 * d), F32),
                        pltpu.VMEM((2, nv, r_blk, d), F32),
                        pltpu.VMEM((2, nv, r_blk, d), F32),
                        pltpu.VMEM((2, nv, r_blk, r_blk), F32)],
        compiler_params=_ARB2,
        name="gdn_core",
    )(y, y, y, y, gates, conv_w, conv_w, conv_w, norm_w.reshape(1, d))


def _gdn_mixer(u, w_in, conv_w, layer, a_log, dt_bias, norm_w, w_out, res):
    bsz, l, dm = u.shape
    u2 = u.reshape(bsz * l, dm)
    n_main = GDN_CONV_DIM + GDN_VAL_DIM
    w_in_t = _transposed(w_in)
    y = _proj(u2, w_in_t, layer, n_out=n_main, w_t=True)
    yg = _proj_tail(u2, w_in_t, layer, n_main)
    o = _gdn_core(y.reshape(bsz, l, -1), yg.reshape(bsz, l, -1), conv_w, layer, a_log, dt_bias, norm_w)
    return _proj(o.reshape(bsz * l, GDN_VAL_DIM), w_out, layer, mode="residual", res=res)


def _ret_kernel(q_ref, k_ref, v_ref, gate_ref, cos_ref, sin_ref, dec_ref, qd_ref, kd_ref, cd_ref, o_ref, s_ref):
    @pl.when(pl.program_id(1) == 0)
    def _():
        s_ref[...] = jnp.zeros_like(s_ref)

    half = RET_QK_HEAD // 2

    def rope(ref, rs, h, c, s):
        x1 = ref[0, rs, h * RET_QK_HEAD:h * RET_QK_HEAD + half]
        x2 = ref[0, rs, h * RET_QK_HEAD + half:(h + 1) * RET_QK_HEAD]
        return jnp.concatenate([x1 * c - x2 * s, x2 * c + x1 * s], axis=1)

    heads = range(RET_HEADS)
    vs = [slice(h * RET_V_HEAD, (h + 1) * RET_V_HEAD) for h in heads]
    for c in range(RET_STEP // RET_CHUNK):
        rs = slice(c * RET_CHUNK, (c + 1) * RET_CHUNK)
        cos, sin = cos_ref[0, rs, :], sin_ref[0, rs, :]
        qr = [rope(q_ref, rs, h, cos, sin) for h in heads]
        kr = [rope(k_ref, rs, h, cos, sin) * RET_QK_HEAD ** -0.5 for h in heads]
        sc = [_bdot_nt(qr[h], kr[h]) for h in heads]
        cross = [_bdot(qr[h] * jnp.concatenate([qd_ref[h], qd_ref[h]], axis=1), s_ref[h]) for h in heads]
        upd = [_bdot_tn(kr[h] * jnp.concatenate([kd_ref[h], kd_ref[h]], axis=1), v_ref[0, rs, vs[h]]) for h in heads]
        inner = [_bdot(sc[h] * dec_ref[h], v_ref[0, rs, vs[h]]) for h in heads]
        for h in heads:
            s_ref[h] = s_ref[h] * cd_ref[h] + upd[h]
            o = inner[h] + cross[h]
            on = o * lax.rsqrt(jnp.mean(o * o, axis=1, keepdims=True) + NORM_EPS)
            o_ref[0, rs, vs[h]] = (_silu(gate_ref[0, rs, vs[h]]) * on).astype(o_ref.dtype)


def _ret_core(y, positions):
    bsz, l, _ = y.shape
    c = RET_CHUNK
    inv_freq = ROPE_THETA ** (-jnp.arange(0, RET_QK_HEAD, 2, dtype=F32) / RET_QK_HEAD)
    ang = positions.astype(F32)[..., None] * inv_freq
    cos, sin = jnp.cos(ang), jnp.sin(ang)
    log_gamma = jnp.log1p(-jnp.exp2(-5.0 - jnp.arange(RET_HEADS, dtype=F32)))
    idx = jnp.arange(c, dtype=F32)
    rel = idx[:, None] - idx[None, :]
    inner_decay = jnp.where(rel >= 0, jnp.exp(log_gamma[:, None, None] * jnp.maximum(rel, 0.0)), 0.0)
    lanes = lambda t: jnp.broadcast_to(t[:, :, None], (RET_HEADS, c, LANES))
    q_decay = lanes(jnp.exp(log_gamma[:, None] * (idx[None, :] + 1.0)))
    k_decay = lanes(jnp.exp(log_gamma[:, None] * (c - 1.0 - idx[None, :])))
    chunk_decay = jnp.exp(log_gamma * c)
    t = RET_STEP
    full = lambda shape: pl.BlockSpec(shape, lambda b, i: (0,) * len(shape))
    return pl.pallas_call(
        _ret_kernel,
        out_shape=jax.ShapeDtypeStruct((bsz, l, RET_V_DIM), BF16),
        grid=(bsz, l // t),
        in_specs=[pl.BlockSpec((1, t, RET_QK_DIM), lambda b, i: (b, i, 0)),
                  pl.BlockSpec((1, t, RET_QK_DIM), lambda b, i: (b, i, 1)),
                  pl.BlockSpec((1, t, RET_V_DIM), lambda b, i: (b, i, 1)),
                  pl.BlockSpec((1, t, RET_V_DIM), lambda b, i: (b, i, 2)),
                  pl.BlockSpec((1, t, LANES), lambda b, i: (b, i, 0)),
                  pl.BlockSpec((1, t, LANES), lambda b, i: (b, i, 0)),
                  full((RET_HEADS, c, c)), full((RET_HEADS, c, LANES)), full((RET_HEADS, c, LANES)),
                  pl.BlockSpec(memory_space=pltpu.SMEM)],
        out_specs=pl.BlockSpec((1, t, RET_V_DIM), lambda b, i: (b, i, 0)),
        scratch_shapes=[pltpu.VMEM((RET_HEADS, RET_QK_HEAD, RET_V_HEAD), F32)],
        compiler_params=_ARB2,
        name="ret_core",
    )(y, y, y, y, cos, sin, inner_decay, q_decay, k_decay, chunk_decay)


def _retention_mixer(u, positions, w_in, w_out, layer, res):
    bsz, l, dm = u.shape
    y = _proj(u.reshape(bsz * l, dm), w_in, layer)
    o = _ret_core(y.reshape(bsz, l, -1), positions)
    return _proj(o.reshape(bsz * l, RET_V_DIM), w_out, layer, mode="residual", res=res)


def _rope_tables(positions, dim):
    inv_freq = ROPE_THETA ** (-jnp.arange(0, dim, 2, dtype=F32) / dim)
    ang = positions.astype(F32)[..., None] * inv_freq
    c, s = jnp.cos(ang), jnp.sin(ang)
    reps = LANES // dim
    cos_full = jnp.tile(jnp.concatenate([c, c], axis=-1), (1, 1, reps))
    sin_signed = jnp.tile(jnp.concatenate([-s, s], axis=-1), (1, 1, reps))
    return cos_full, sin_signed


def _rope128(x, cos, sin_signed):
    return x * cos + pltpu.roll(x, 64, 1) * sin_signed


def _rope64(x, cos, sin_signed, low_half):
    partner = jnp.where(low_half, pltpu.roll(x, 96, 1), pltpu.roll(x, 32, 1))
    return x * cos + partner * sin_signed


def _dsa_prep_kernel(k_ref, v_ref, ki_ref, ck_ref, sk_ref, cik_ref, sik_ref, kb_ref, vb_ref, ki2_ref):
    ck, sk = ck_ref[0], sk_ref[0]
    for g in range(DSA_KV_HEADS):
        sl = slice(g * DSA_HEAD_DIM, (g + 1) * DSA_HEAD_DIM)
        kb_ref[0, g] = _rope128(k_ref[0, :, sl], ck, sk).astype(BF16)
        vb_ref[0, g] = v_ref[0, :, sl].astype(BF16)
    x = ki_ref[0]
    lane = _lane_iota(x.shape)
    xr = _rope64(x, cik_ref[0], sik_ref[0], (lane % IDX_HEAD_DIM) < IDX_HEAD_DIM // 2)
    ki2_ref[0] = jnp.where(lane < IDX_HEAD_DIM, xr, pltpu.roll(xr, IDX_HEAD_DIM, 1)).astype(BF16)


def _dsa_block(s_eff, q_ref, qi_ref, wi_ref, cq_ref, sq_ref, ciq_ref, siq_ref, kb_ref, vb_ref, ki2_ref, o_ref,
               key_ref, keyt_ref, *, topk):
    t_blk = DSA_BLOCK
    qb = pl.program_id(1)
    lane = _lane_iota((t_blk, LANES))
    low_half = (lane % IDX_HEAD_DIM) < IDX_HEAD_DIM // 2
    lo64 = lane < IDX_HEAD_DIM

    wi = wi_ref[0] * (IDX_HEADS ** -0.5 * IDX_HEAD_DIM ** -0.5)
    ciq, siq = ciq_ref[0], siq_ref[0]
    ki2 = ki2_ref[0, :s_eff, :]
    score = jnp.zeros((t_blk, s_eff), F32)
    for quad in range(IDX_HEADS // 4):
        rows = []
        for pr in (2 * quad, 2 * quad + 1):
            xr = _rope64(qi_ref[0, :, pr * LANES:(pr + 1) * LANES], ciq, siq, low_half)
            rows.append(jnp.where(lo64, xr, 0.0))
            rows.append(jnp.where(lo64, 0.0, xr))
        s4 = _bdot_nt(jnp.concatenate(rows, axis=0), ki2)
        for j in range(4):
            hh = 4 * quad + j
            w_col = wi[:, IDX_HEAD_DIM + hh:IDX_HEAD_DIM + hh + 1]
            score = score + w_col * jnp.maximum(s4[j * t_blk:(j + 1) * t_blk], 0.0)

    t_idx = qb * t_blk + lax.broadcasted_iota(jnp.int32, (t_blk, 1), 0)
    s_idx = _lane_iota((t_blk, s_eff))
    score = jnp.where(s_idx <= t_idx, score, -jnp.inf)
    bits = lax.bitcast_convert_type(score, jnp.int32)
    keys = bits ^ ((bits >> 31) & jnp.int32(0x7FFFFFFF))
    key_ref[:, :s_eff] = keys
    keyt_ref[:s_eff, :] = keys.T

    kk = jnp.minimum(qb * t_blk + _lane_iota((1, t_blk)) + 1, topk).astype(F32)
    int_min = jnp.int32(-2 ** 31)

    n_chain = 8
    rows = s_eff // n_chain

    def bisect(i, t_u):
        cand_u = t_u | lax.shift_left(jnp.int32(1), 31 - i)
        cand = jnp.broadcast_to(cand_u ^ int_min, (SUBLANES, t_blk))
        acc = [jnp.zeros((SUBLANES, t_blk), F32) for _ in range(n_chain)]
        for r in range(0, rows, SUBLANES):
            for c in range(n_chain):
                hit = keyt_ref[c * rows + r:c * rows + r + SUBLANES, :] >= cand
                acc[c] = acc[c] + jnp.where(hit, 1.0, 0.0)
        while len(acc) > 1:
            acc = [acc[k] + acc[k + 1] for k in range(0, len(acc), 2)]
        cnt = jnp.sum(acc[0], axis=0, keepdims=True)
        return jnp.where(cnt >= kk, cand_u, t_u)

    t_u = lax.fori_loop(0, 32, bisect, jnp.zeros((1, t_blk), jnp.int32))
    thr_row = t_u ^ int_min

    kt = keyt_ref[:s_eff, :]
    n_ge = jnp.sum(jnp.where(kt >= thr_row, 1.0, 0.0), axis=0, keepdims=True)

    @pl.when(jnp.max(n_ge - kk) > 0.0)
    def _():
        tied = jnp.where(kt == thr_row, 1.0, 0.0)
        need = kk - jnp.sum(jnp.where(kt > thr_row, 1.0, 0.0), axis=0, keepdims=True)
        row = lax.broadcasted_iota(jnp.int32, kt.shape, 0)
        before = tied
        sh = 1
        while sh < s_eff:
            before = before + jnp.where(row >= sh, pltpu.roll(before, sh, 0), 0.0)
            sh *= 2
        surplus = jnp.where((tied > 0.0) & (before > need), 1.0, 0.0)
        key_ref[:, :s_eff] = jnp.where(surplus.T > 0.0, int_min, key_ref[:, :s_eff])

    thr = jnp.broadcast_to(thr_row, (t_blk, t_blk)).T[:, :1]
    sel = key_ref[:, :s_eff] >= thr

    cq, sq = cq_ref[0], sq_ref[0]
    group = DSA_HEADS // DSA_KV_HEADS
    def group_logits(g):
        qs = []
        for j in range(group):
            hh = group * g + j
            qs.append(_rope128(q_ref[0, :, hh * DSA_HEAD_DIM:(hh + 1) * DSA_HEAD_DIM], cq, sq) * DSA_HEAD_DIM ** -0.5)
        return _bdot_nt(jnp.concatenate(qs, axis=0), kb_ref[0, g, :s_eff, :])

    next_logits = group_logits(0)
    for g in range(DSA_KV_HEADS):
        logits = next_logits
        if g + 1 < DSA_KV_HEADS:
            next_logits = group_logits(g + 1)
        ps, dens = [], []
        for j in range(group):
            lj = jnp.where(sel, logits[j * t_blk:(j + 1) * t_blk], -jnp.inf)
            pj = jnp.exp(lj - jnp.max(lj, axis=1, keepdims=True))
            dens.append(jnp.sum(pj, axis=1, keepdims=True))
            ps.append(pj)
        pv = _bdot(jnp.concatenate(ps, axis=0), vb_ref[0, g, :s_eff, :])
        for j in range(group):
            hh = group * g + j
            o_ref[0, :, hh * DSA_HEAD_DIM:(hh + 1) * DSA_HEAD_DIM] = (
                pv[j * t_blk:(j + 1) * t_blk] / dens[j]).astype(o_ref.dtype)


def _dsa_kernel(q_ref, qi_ref, wi_ref, cq_ref, sq_ref, ciq_ref, siq_ref, kb_ref, vb_ref, ki2_ref, o_ref, key_ref,
                keyt_ref, *, topk, seq):
    qb = pl.program_id(1)
    blocks_per_step = DSA_KV_STEP // DSA_BLOCK
    for br in range(seq // DSA_KV_STEP):
        @pl.when(qb // blocks_per_step == br)
        def _():
            _dsa_block((br + 1) * DSA_KV_STEP, q_ref, qi_ref, wi_ref, cq_ref, sq_ref, ciq_ref, siq_ref, kb_ref,
                       vb_ref, ki2_ref, o_ref, key_ref, keyt_ref, topk=topk)


def _dsa_core(y, yx, positions):
    bsz, l, _ = y.shape
    topk = min(DSA_TOPK_MAX, l // 4)
    cos_q, sin_q = _rope_tables(positions, DSA_HEAD_DIM)
    cos_i, sin_i = _rope_tables(positions, IDX_HEAD_DIM)
    k_blk = DSA_Q_DIM // DSA_KV_DIM
    ki_blk = 0
    ps = DSA_KV_STEP
    tab = pl.BlockSpec((1, ps, LANES), lambda b, i: (b, i, 0))
    kb, vb, ki2 = pl.pallas_call(
        _dsa_prep_kernel,
        out_shape=(jax.ShapeDtypeStruct((bsz, DSA_KV_HEADS, l, DSA_HEAD_DIM), BF16),
                   jax.ShapeDtypeStruct((bsz, DSA_KV_HEADS, l, DSA_HEAD_DIM), BF16),
                   jax.ShapeDtypeStruct((bsz, l, LANES), BF16)),
        grid=(bsz, l // ps),
        in_specs=[pl.BlockSpec((1, ps, DSA_KV_DIM), lambda b, i: (b, i, k_blk)),
                  pl.BlockSpec((1, ps, DSA_KV_DIM), lambda b, i: (b, i, k_blk + 1)),
                  pl.BlockSpec((1, ps, LANES), lambda b, i: (b, i, ki_blk)),
                  tab, tab, tab, tab],
        out_specs=(pl.BlockSpec((1, DSA_KV_HEADS, ps, DSA_HEAD_DIM), lambda b, i: (b, 0, i, 0)),
                   pl.BlockSpec((1, DSA_KV_HEADS, ps, DSA_HEAD_DIM), lambda b, i: (b, 0, i, 0)),
                   pl.BlockSpec((1, ps, LANES), lambda b, i: (b, i, 0))),
        compiler_params=_ARB2,
        name="dsa_prep",
    )(y, y, yx, cos_q, sin_q, cos_i, sin_i)

    t_blk = DSA_BLOCK
    qtab = pl.BlockSpec((1, t_blk, LANES), lambda b, i: (b, i, 0))
    kv_spec = pl.BlockSpec((1, DSA_KV_HEADS, l, DSA_HEAD_DIM), lambda b, i: (b, 0, 0, 0))
    return pl.pallas_call(
        functools.partial(_dsa_kernel, topk=topk, seq=l),
        out_shape=jax.ShapeDtypeStruct((bsz, l, DSA_Q_DIM), BF16),
        grid=(bsz, l // t_blk),
        in_specs=[pl.BlockSpec((1, t_blk, DSA_Q_DIM), lambda b, i: (b, i, 0)),
                  pl.BlockSpec((1, t_blk, IDX_HEADS * IDX_HEAD_DIM),
                               lambda b, i: (b, i, (DSA_Q_DIM + 2 * DSA_KV_DIM) // (IDX_HEADS * IDX_HEAD_DIM))),
                  pl.BlockSpec((1, t_blk, LANES), lambda b, i: (b, i, ki_blk)),
                  qtab, qtab, qtab, qtab,
                  kv_spec, kv_spec,
                  pl.BlockSpec((1, l, LANES), lambda b, i: (b, 0, 0))],
        out_specs=pl.BlockSpec((1, t_blk, DSA_Q_DIM), lambda b, i: (b, i, 0)),
        scratch_shapes=[pltpu.VMEM((t_blk, l), jnp.int32), pltpu.VMEM((l, t_blk), jnp.int32)],
        compiler_params=_ARB2,
        name="dsa_attn",
    )(y, y, yx, cos_q, sin_q, cos_i, sin_i, kb, vb, ki2)


def _dsa_mixer(u, positions, w_in, w_out, layer, res):
    bsz, l, dm = u.shape
    u2 = u.reshape(bsz * l, dm)
    n_main = DSA_Q_DIM + 2 * DSA_KV_DIM + IDX_HEADS * IDX_HEAD_DIM
    w_in_t = _transposed(w_in)
    y = _proj(u2, w_in_t, layer, n_out=n_main, w_t=True)
    yx = _proj_tail(u2, w_in_t, layer, n_main)
    o = _dsa_core(y.reshape(bsz, l, -1), yx.reshape(bsz, l, -1), positions)
    return _proj(o.reshape(bsz * l, DSA_Q_DIM), w_out, layer, mode="residual", res=res)


def kernel(x, p, positions, norm_mix, norm_mlp, norm_ple, norm_final, gdn_w_in, gdn_conv_w, gdn_a_log,
           gdn_dt_bias, gdn_norm, gdn_w_out, ret_w_in, ret_w_out, dsa_w_in, dsa_w_out, mlp_w_up, mlp_w_down,
           ple_w_gate, ple_w_proj):
    bsz, l, d = x.shape
    m = bsz * l
    h = x.reshape(m, d)
    for i in range(DEPTH):
        kind, j = i % N_MIXERS, i // N_MIXERS
        u = _rmsnorm(h, norm_mix[i], BF16).reshape(bsz, l, d)
        if kind == 0:
            h = _gdn_mixer(u, gdn_w_in, gdn_conv_w, j, gdn_a_log[j], gdn_dt_bias[j], gdn_norm[j], gdn_w_out, h)
        elif kind == 1:
            h = _retention_mixer(u, positions, ret_w_in, ret_w_out, j, h)
        else:
            h = _dsa_mixer(u, positions, dsa_w_in, dsa_w_out, j, h)
        u = _rmsnorm(h, norm_mlp[i], BF16)
        a = _proj(u, mlp_w_up, i, mode="relu2", out_dtype=BF16)
        h = _proj(a, mlp_w_down, i, mode="residual", res=h)
        u = _rmsnorm(h, norm_ple[i], BF16)
        h = _proj(u, ple_w_gate, i, mode="ple", res=h, p=p.reshape(DEPTH, m, -1), wp=ple_w_proj)
    return _rmsnorm(h, norm_final, x.dtype).reshape(bsz, l, d)
```

```python
import functools

import jax
import jax.numpy as jnp
from jax import lax
from jax.experimental import pallas as pl
from jax.experimental.pallas import tpu as pltpu

F32 = jnp.float32
BF16 = jnp.bfloat16

D_MODEL = 2048
DEPTH = 4
N_MIXERS = 3
NORM_EPS = 1e-6
ROPE_THETA = 10000.0

GDN_HEAD_DIM = 128
GDN_QK_HEADS = D_MODEL // 128
GDN_V_HEADS = 2 * GDN_QK_HEADS
GDN_CONV = 4
GDN_CHUNK = 64
GDN_KEY_DIM = GDN_QK_HEADS * GDN_HEAD_DIM
GDN_VAL_DIM = GDN_V_HEADS * GDN_HEAD_DIM
GDN_CONV_DIM = 2 * GDN_KEY_DIM + GDN_VAL_DIM
GDN_SUPER = 256
GDN_STEP_QK = 2

RET_HEADS = 8
RET_QK_DIM = D_MODEL
RET_V_DIM = 2 * D_MODEL
RET_QK_HEAD = RET_QK_DIM // RET_HEADS
RET_V_HEAD = RET_V_DIM // RET_HEADS
RET_CHUNK = 128
RET_STEP = 256

DSA_HEAD_DIM = 128
DSA_HEADS = D_MODEL // DSA_HEAD_DIM
DSA_KV_HEADS = 4
IDX_HEADS = 16
IDX_HEAD_DIM = 64
DSA_TOPK_MAX = 256
DSA_BLOCK = 128
DSA_KV_STEP = 512
DSA_Q_DIM = DSA_HEADS * DSA_HEAD_DIM
DSA_KV_DIM = DSA_KV_HEADS * DSA_HEAD_DIM

LANES = 128
SUBLANES = 8
VMEM_LIMIT_BYTES = 56 * 1024 * 1024

_ARB2 = pltpu.CompilerParams(dimension_semantics=("arbitrary", "arbitrary"), vmem_limit_bytes=VMEM_LIMIT_BYTES)


def _bdot(a, b):
    return jnp.dot(a.astype(BF16), b.astype(BF16), preferred_element_type=F32)


def _bdot_nt(a, b):
    return lax.dot_general(a.astype(BF16), b.astype(BF16), (((1,), (1,)), ((), ())), preferred_element_type=F32)


def _bdot_tn(a, b):
    return lax.dot_general(a.astype(BF16), b.astype(BF16), (((0,), (0,)), ((), ())), preferred_element_type=F32)


def _lane_iota(shape):
    return lax.broadcasted_iota(jnp.int32, shape, 1)


def _silu(x):
    return x * jax.nn.sigmoid(x)


def _mm_kernel(*refs, nk, mode, valid, w_t):
    x_ref, w_ref, o_ref = refs[0], refs[1], refs[-1]
    extra = refs[2:-1]
    if w_t:
        w = w_ref[...]
        if mode == "tail":
            w = jnp.where(lax.broadcasted_iota(jnp.int32, w.shape, 0) < valid, w, 0.0)
        o_ref[...] = _bdot_nt(x_ref[...], w)
        return
    if nk > 1:
        @pl.when(pl.program_id(2) == 0)
        def _():
            o_ref[...] = extra[0][...]

        half = o_ref.shape[1] // 2
        for cs in (slice(0, half), slice(half, 2 * half)):
            o_ref[:, cs] += _bdot(x_ref[...], w_ref[:, cs])
        return
    if mode == "tail":
        w = jnp.where(_lane_iota(w_ref.shape) < valid, w_ref[...], 0.0)
        o_ref[...] = _bdot(x_ref[...], w)
        return
    prod = _bdot(x_ref[...], w_ref[...])
    if mode == "plain":
        o_ref[...] = prod.astype(o_ref.dtype)
    elif mode == "relu2":
        o_ref[...] = jnp.square(jnp.maximum(prod, 0.0)).astype(o_ref.dtype)
    elif mode == "ple":
        res_ref, p_ref, wp_ref = extra
        o_ref[...] = res_ref[...] + jax.nn.sigmoid(prod) * _bdot(p_ref[...], wp_ref[...])
    else:
        o_ref[...] = extra[0][...] + prod


def _matmul(x, w, layer, *, n_out, bm, bn, bk, n_outer, col0=0, mode="plain", out_dtype=F32, res=None, p=None,
            wp=None, w_t=False):
    m, kdim = x.shape
    bm = min(bm, m)
    n_total = w.shape[1] if w_t else w.shape[2]
    assert m % bm == 0 and n_out % bn == 0 and col0 % bn == 0 and kdim % bk == 0, (x.shape, w.shape, bm, bn, bk)
    assert col0 + n_out <= n_total or (mode == "tail" and n_out == bn)
    nk = kdim // bk
    assert nk == 1 or mode == "residual"
    assert not w_t or mode in ("plain", "tail")
    cb0 = col0 // bn
    if n_outer:
        grid = (n_out // bn, m // bm, nk)
        mi, ni = (lambda j, i, k: i), (lambda j, i, k: j)
    else:
        grid = (m // bm, n_out // bn, nk)
        mi, ni = (lambda i, j, k: i), (lambda i, j, k: j)
    in_specs = [pl.BlockSpec((bm, bk), lambda a, b, k: (mi(a, b, k), k)),
                pl.BlockSpec((None, bn, bk), lambda a, b, k: (layer, cb0 + ni(a, b, k), k)) if w_t else
                pl.BlockSpec((None, bk, bn), lambda a, b, k: (layer, k, cb0 + ni(a, b, k)))]
    args = [x, w]
    o_map = lambda a, b, k: (mi(a, b, k), ni(a, b, k))
    if mode in ("residual", "ple"):
        in_specs.append(pl.BlockSpec((bm, bn), o_map))
        args.append(res)
    if mode == "ple":
        kp = p.shape[2]
        in_specs.append(pl.BlockSpec((None, bm, kp), lambda a, b, k: (layer, mi(a, b, k), 0)))
        in_specs.append(pl.BlockSpec((None, kp, bn), lambda a, b, k: (layer, 0, ni(a, b, k))))
        args += [p, wp]
    return pl.pallas_call(
        functools.partial(_mm_kernel, nk=nk, mode=mode, valid=n_total - col0, w_t=w_t),
        out_shape=jax.ShapeDtypeStruct((m, n_out), out_dtype),
        grid=grid,
        in_specs=in_specs,
        out_specs=pl.BlockSpec((bm, bn), o_map),
        compiler_params=pltpu.CompilerParams(
            dimension_semantics=("arbitrary", "arbitrary", "arbitrary"), vmem_limit_bytes=VMEM_LIMIT_BYTES),
        name="matmul_" + mode,
    )(*args)


def _proj(x, w, layer, n_out=None, **kw):
    w_t = kw.get("w_t", False)
    kdim = w.shape[2] if w_t else w.shape[1]
    n_out = (w.shape[1] if w_t else w.shape[2]) if n_out is None else n_out
    if kdim <= 2048:
        bm = 512 if kw.get("mode") == "ple" else 1024
        return _matmul(x, w, layer, n_out=n_out, bm=bm, bn=min(n_out, 1024), bk=kdim, n_outer=True, **kw)
    return _matmul(x, w, layer, n_out=n_out, bm=2048, bn=1024, bk=1024, n_outer=False, **kw)


def _proj_tail(x, w_t, layer, start):
    return _matmul(x, w_t, layer, n_out=LANES, col0=start, bm=1024, bn=LANES, bk=w_t.shape[2], n_outer=True,
                   mode="tail", w_t=True)


def _transposed(w):
    return jnp.swapaxes(w, 1, 2)


def _rmsnorm_kernel(x_ref, g_ref, o_ref):
    x = x_ref[...]
    y = x * lax.rsqrt(jnp.mean(x * x, axis=1, keepdims=True) + NORM_EPS) * g_ref[...]
    o_ref[...] = y.astype(o_ref.dtype)


def _rmsnorm(x, gain, out_dtype):
    m, d = x.shape
    bm = min(m, 512)
    return pl.pallas_call(
        _rmsnorm_kernel,
        out_shape=jax.ShapeDtypeStruct((m, d), out_dtype),
        grid=(m // bm,),
        in_specs=[pl.BlockSpec((bm, d), lambda i: (i, 0)), pl.BlockSpec((1, d), lambda i: (0, 0))],
        out_specs=pl.BlockSpec((bm, d), lambda i: (i, 0)),
        compiler_params=pltpu.CompilerParams(dimension_semantics=("arbitrary",), vmem_limit_bytes=VMEM_LIMIT_BYTES),
        name="rmsnorm",
    )(x, gain.reshape(1, d))


def _gdn_gates_kernel(gb_ref, alog_ref, dtb_ref, o_ref):
    x = gb_ref[0]
    lane = _lane_iota(x.shape)
    row = lax.broadcasted_iota(jnp.int32, x.shape, 0) % GDN_CHUNK
    g = -jnp.exp(alog_ref[...]) * jax.nn.softplus(x + dtb_ref[...])
    sh = 1
    while sh < GDN_CHUNK:
        g = g + jnp.where(row >= sh, pltpu.roll(g, sh, 0), 0.0)
        sh *= 2
    o_ref[0] = jnp.where(lane < GDN_V_HEADS, jax.nn.sigmoid(x), g)


def _conv_silu_tile(x, tail, w):
    rows8 = lax.broadcasted_iota(jnp.int32, tail.shape, 0)
    acc = x * w[GDN_CONV - 1:GDN_CONV]
    for sh in range(1, GDN_CONV):
        xs = pltpu.roll(x, sh, 0)
        head = jnp.where(rows8 < sh, pltpu.roll(tail, sh, 0), xs[:SUBLANES])
        xs = jnp.concatenate([head, xs[SUBLANES:]], axis=0)
        acc = acc + xs * w[GDN_CONV - 1 - sh:GDN_CONV - sh]
    return _silu(acc)


def _gdn_kernel(q_ref, k_ref, v_ref, z_ref, gt_ref, wq_ref, wk_ref, wv_ref, nw_ref, o_ref,
                s_ref, kn_st, wk_st, qg_st, gl_st, in_st, *, seq):
    hq = pl.program_id(1)
    r_blk, c_blk, d = GDN_SUPER, GDN_CHUNK, GDN_HEAD_DIM
    ii = lax.broadcasted_iota(jnp.int32, (r_blk, r_blk), 0)
    jj = lax.broadcasted_iota(jnp.int32, (r_blk, r_blk), 1)
    blk_masks = [(ii >> sh) == (jj >> sh) for sh in (3, 4, 5, 6)]
    blk8 = blk_masks[0]
    tri_incl = blk_masks[3] & (ii >= jj)
    diag = ii == jj
    eye = jnp.where(diag, 1.0, 0.0)
    lane = _lane_iota((r_blk, LANES))
    s_ref[...] = jnp.zeros_like(s_ref)
    wq, wk, wv, nw = wq_ref[...], wk_ref[...], wv_ref[...], nw_ref[...]
    n_steps = seq // r_blk
    rep = GDN_V_HEADS // GDN_QK_HEADS

    def fetch(s):
        r0 = pl.multiple_of(s * r_blk, r_blk)
        rt = pl.multiple_of(jnp.maximum(r0 - SUBLANES, 0), SUBLANES)
        tail = lambda ref: jnp.where(s > 0, ref[0, pl.ds(rt, SUBLANES), :], 0.0)
        rows = pl.ds(r0, r_blk)
        return (q_ref[0, rows, :], tail(q_ref), k_ref[0, rows, :], tail(k_ref), v_ref[0, rows, :], tail(v_ref),
                gt_ref[0, rows, :])

    heads = range(GDN_STEP_QK * rep)
    qk_of = lambda j: j // rep

    def solve(raw, slot):
        xq, tq, xk, tk, xv, tv, gt = raw
        qc, kc, vc = _conv_silu_tile(xq, tq, wq), _conv_silu_tile(xk, tk, wk), _conv_silu_tile(xv, tv, wv)
        qn, kn = [], []
        for i in range(GDN_STEP_QK):
            qi, ki = qc[:, i * d:(i + 1) * d], kc[:, i * d:(i + 1) * d]
            qn.append(qi * lax.rsqrt(jnp.sum(qi * qi, axis=1, keepdims=True) + NORM_EPS) * d ** -0.5)
            kn.append(ki * lax.rsqrt(jnp.sum(ki * ki, axis=1, keepdims=True) + NORM_EPS))
            kn_st[slot, i] = kn[i]
        hv0 = len(heads) * hq
        beta = [jnp.sum(jnp.where(lane == hv0 + j, gt, 0.0), axis=1, keepdims=True) for j in heads]
        g = [jnp.sum(jnp.where(lane == GDN_V_HEADS + hv0 + j, gt, 0.0), axis=1, keepdims=True) for j in heads]
        gi = [jnp.broadcast_to(g[j], (r_blk, r_blk)) for j in heads]
        dec = [jnp.exp(jnp.where(tri_incl, gi[j] - jnp.sum(jnp.where(diag, gi[j], 0.0), axis=0, keepdims=True),
                                 -jnp.inf)) for j in heads]
        kb = [kn[qk_of(j)] * beta[j] for j in heads]
        for j in heads:
            qg_st[slot, j] = qn[qk_of(j)] * jnp.exp(g[j])
            gl_st[slot, j] = gi[j][:, :d]
        qk = [_bdot_nt(qn[i], kn[i]) for i in range(GDN_STEP_QK)]
        kk = [_bdot_nt(kb[j], kn[qk_of(j)]) for j in heads]
        yield
        for j in heads:
            in_st[slot, j] = qk[qk_of(j)] * dec[j]
        a = [jnp.where(diag, 0.0, kk[j] * dec[j]) for j in heads]
        a8 = [jnp.where(blk8, a[j], 0.0) for j in heads]
        a8_2 = [_bdot(a8[j], a8[j]) for j in heads]
        yield
        inv = [eye - a8[j] for j in heads]
        t = [_bdot(inv[j], a8_2[j]) for j in heads]
        a8_4 = [_bdot(a8_2[j], a8_2[j]) for j in heads]
        yield
        inv = [inv[j] + t[j] for j in heads]
        t = [_bdot(inv[j], a8_4[j]) for j in heads]
        yield
        inv = [inv[j] + t[j] for j in heads]
        for lvl in range(3):
            off = [jnp.where(blk_masks[lvl + 1] & ~blk_masks[lvl], a[j], 0.0) for j in heads]
            t = [_bdot(off[j], inv[j]) for j in heads]
            yield
            t = [_bdot(inv[j], t[j]) for j in heads]
            yield
            inv = [inv[j] - t[j] for j in heads]
        rhs = [jnp.concatenate([vc[:, j * d:(j + 1) * d] * beta[j], kb[j] * jnp.exp(g[j])], axis=1) for j in heads]
        x = [_bdot(inv[j], rhs[j]) for j in heads]
        yield
        for j in heads:
            wk_st[slot, j] = x[j]

    def recur(s, slot):
        r0 = pl.multiple_of(s * r_blk, r_blk)
        state = [s_ref[j] for j in heads]
        for c in range(r_blk // c_blk):
            rs = slice(c * c_blk, (c + 1) * c_blk)
            gl = [gl_st[slot, j, rs, :] for j in heads]
            g_last = [gl[j][c_blk - 1:c_blk, :] for j in heads]
            both = [_bdot(jnp.concatenate([wk_st[slot, j, rs, d:], qg_st[slot, j, rs, :]], axis=0), state[j])
                    for j in heads]
            yield
            v_new = [wk_st[slot, j, rs, :d] - both[j][:c_blk] for j in heads]
            kd = [kn_st[slot, qk_of(j), rs, :] * jnp.exp(g_last[j] - gl[j]) for j in heads]
            upd = [_bdot_tn(kd[j], v_new[j]) for j in heads]
            pad = []
            for j in heads:
                parts = []
                if c > 0:
                    parts.append(jnp.zeros((c * c_blk, d), F32))
                parts.append(v_new[j])
                if (c + 1) * c_blk < r_blk:
                    parts.append(jnp.zeros((r_blk - (c + 1) * c_blk, d), F32))
                pad.append(jnp.concatenate(parts, axis=0))
            o_in = [_bdot(in_st[slot, j, rs, :], pad[j]) for j in heads]
            yield
            state = [state[j] * jnp.exp(g_last[j]) + upd[j] for j in heads]
            rows = pl.ds(r0 + c * c_blk, c_blk)
            for j in heads:
                o_c = both[j][c_blk:] + o_in[j]
                on = o_c * lax.rsqrt(jnp.mean(o_c * o_c, axis=1, keepdims=True) + NORM_EPS) * nw
                zc = z_ref[0, rows, j * d:(j + 1) * d]
                o_ref[0, rows, j * d:(j + 1) * d] = (on * _silu(zc)).astype(o_ref.dtype)
        for j in heads:
            s_ref[j] = state[j]

    def run(*gens):
        live = list(gens)
        while live:
            live = [gen for gen in live if next(gen, StopIteration) is not StopIteration]

    run(solve(fetch(0), 0))

    def body(i, carry):
        s = 2 * i
        run(solve(fetch(s + 1), 1), recur(s, 0))
        run(solve(fetch(s + 2), 0), recur(s + 1, 1))
        return carry

    lax.fori_loop(0, n_steps // 2 - 1, body, 0)
    run(solve(fetch(n_steps - 1), 1), recur(n_steps - 2, 0))
    run(recur(n_steps - 1, 1))


def _gdn_core(y, yg, conv_w, layer, a_log, dt_bias, norm_w):
    bsz, l, _ = y.shape
    assert l % (2 * GDN_SUPER) == 0
    nh, d = GDN_QK_HEADS, GDN_HEAD_DIM
    rep = GDN_V_HEADS // GDN_QK_HEADS
    lane_vec = lambda t: jnp.zeros((1, LANES), F32).at[0, GDN_V_HEADS:2 * GDN_V_HEADS].set(t.astype(F32))
    gates = pl.pallas_call(
        _gdn_gates_kernel,
        out_shape=jax.ShapeDtypeStruct((bsz, l, LANES), F32),
        grid=(bsz,),
        in_specs=[pl.BlockSpec((1, l, LANES), lambda b: (b, 0, 0)),
                  pl.BlockSpec((1, LANES), lambda b: (0, 0)), pl.BlockSpec((1, LANES), lambda b: (0, 0))],
        out_specs=pl.BlockSpec((1, l, LANES), lambda b: (b, 0, 0)),
        compiler_params=pltpu.CompilerParams(dimension_semantics=("arbitrary",), vmem_limit_bytes=VMEM_LIMIT_BYTES),
        name="gdn_gates",
    )(yg, lane_vec(a_log), lane_vec(dt_bias))
    r_blk = GDN_SUPER
    nq = GDN_STEP_QK
    qw, vw = nq * d, nq * rep * d
    nv = nq * rep
    k0, v0, z0 = GDN_KEY_DIM // qw, 2 * GDN_KEY_DIM // vw, GDN_CONV_DIM // vw
    return pl.pallas_call(
        functools.partial(_gdn_kernel, seq=l),
        out_shape=jax.ShapeDtypeStruct((bsz, l, GDN_VAL_DIM), BF16),
        grid=(bsz, nh // nq),
        in_specs=[pl.BlockSpec((1, l, qw), lambda b, h: (b, 0, h)),
                  pl.BlockSpec((1, l, qw), lambda b, h: (b, 0, k0 + h)),
                  pl.BlockSpec((1, l, vw), lambda b, h: (b, 0, v0 + h)),
                  pl.BlockSpec((1, l, vw), lambda b, h: (b, 0, z0 + h)),
                  pl.BlockSpec((1, l, LANES), lambda b, h: (b, 0, 0)),
                  pl.BlockSpec((None, GDN_CONV, qw), lambda b, h: (layer, 0, h)),
                  pl.BlockSpec((None, GDN_CONV, qw), lambda b, h: (layer, 0, k0 + h)),
                  pl.BlockSpec((None, GDN_CONV, vw), lambda b, h: (layer, 0, v0 + h)),
                  pl.BlockSpec((1, d), lambda b, h: (0, 0))],
        out_specs=pl.BlockSpec((1, l, vw), lambda b, h: (b, 0, h)),
        scratch_shapes=[pltpu.VMEM((nv, d, d), F32),
                        pltpu.VMEM((2, nq, r_blk, d), F32),
                        pltpu.VMEM((2, nv, r_blk, 2 * d), F32),
                        pltpu.VMEM((2, nv, r_blk, d), F32),
                        pltpu.VMEM((2, nv, r_blk, d), F32),
                        pltpu.VMEM((2, nv, r_blk, r_blk), F32)],
        compiler_params=_ARB2,
        name="gdn_core",
    )(y, y, y, y, gates, conv_w, conv_w, conv_w, norm_w.reshape(1, d))


def _gdn_mixer(u, w_in, conv_w, layer, a_log, dt_bias, norm_w, w_out, res):
    bsz, l, dm = u.shape
    u2 = u.reshape(bsz * l, dm)
    n_main = GDN_CONV_DIM + GDN_VAL_DIM
    w_in_t = _transposed(w_in)
    y = _proj(u2, w_in_t, layer, n_out=n_main, w_t=True)
    yg = _proj_tail(u2, w_in_t, layer, n_main)
    o = _gdn_core(y.reshape(bsz, l, -1), yg.reshape(bsz, l, -1), conv_w, layer, a_log, dt_bias, norm_w)
    return _proj(o.reshape(bsz * l, GDN_VAL_DIM), w_out, layer, mode="residual", res=res)


def _ret_kernel(q_ref, k_ref, v_ref, gate_ref, cos_ref, sin_ref, dec_ref, qd_ref, kd_ref, cd_ref, o_ref, s_ref):
    @pl.when(pl.program_id(1) == 0)
    def _():
        s_ref[...] = jnp.zeros_like(s_ref)

    half = RET_QK_HEAD // 2

    def rope(ref, rs, h, c, s):
        x1 = ref[0, rs, h * RET_QK_HEAD:h * RET_QK_HEAD + half]
        x2 = ref[0, rs, h * RET_QK_HEAD + half:(h + 1) * RET_QK_HEAD]
        return jnp.concatenate([x1 * c - x2 * s, x2 * c + x1 * s], axis=1)

    heads = range(RET_HEADS)
    vs = [slice(h * RET_V_HEAD, (h + 1) * RET_V_HEAD) for h in heads]
    for c in range(RET_STEP // RET_CHUNK):
        rs = slice(c * RET_CHUNK, (c + 1) * RET_CHUNK)
        cos, sin = cos_ref[0, rs, :], sin_ref[0, rs, :]
        qr = [rope(q_ref, rs, h, cos, sin) for h in heads]
        kr = [rope(k_ref, rs, h, cos, sin) * RET_QK_HEAD ** -0.5 for h in heads]
        sc = [_bdot_nt(qr[h], kr[h]) for h in heads]
        cross = [_bdot(qr[h] * jnp.concatenate([qd_ref[h], qd_ref[h]], axis=1), s_ref[h]) for h in heads]
        upd = [_bdot_tn(kr[h] * jnp.concatenate([kd_ref[h], kd_ref[h]], axis=1), v_ref[0, rs, vs[h]]) for h in heads]
        inner = [_bdot(sc[h] * dec_ref[h], v_ref[0, rs, vs[h]]) for h in heads]
        for h in heads:
            s_ref[h] = s_ref[h] * cd_ref[h] + upd[h]
            o = inner[h] + cross[h]
            on = o * lax.rsqrt(jnp.mean(o * o, axis=1, keepdims=True) + NORM_EPS)
            o_ref[0, rs, vs[h]] = (_silu(gate_ref[0, rs, vs[h]]) * on).astype(o_ref.dtype)


def _ret_core(y, positions):
    bsz, l, _ = y.shape
    c = RET_CHUNK
    inv_freq = ROPE_THETA ** (-jnp.arange(0, RET_QK_HEAD, 2, dtype=F32) / RET_QK_HEAD)
    ang = positions.astype(F32)[..., None] * inv_freq
    cos, sin = jnp.cos(ang), jnp.sin(ang)
    log_gamma = jnp.log1p(-jnp.exp2(-5.0 - jnp.arange(RET_HEADS, dtype=F32)))
    idx = jnp.arange(c, dtype=F32)
    rel = idx[:, None] - idx[None, :]
    inner_decay = jnp.where(rel >= 0, jnp.exp(log_gamma[:, None, None] * jnp.maximum(rel, 0.0)), 0.0)
    lanes = lambda t: jnp.broadcast_to(t[:, :, None], (RET_HEADS, c, LANES))
    q_decay = lanes(jnp.exp(log_gamma[:, None] * (idx[None, :] + 1.0)))
    k_decay = lanes(jnp.exp(log_gamma[:, None] * (c - 1.0 - idx[None, :])))
    chunk_decay = jnp.exp(log_gamma * c)
    t = RET_STEP
    full = lambda shape: pl.BlockSpec(shape, lambda b, i: (0,) * len(shape))
    return pl.pallas_call(
        _ret_kernel,
        out_shape=jax.ShapeDtypeStruct((bsz, l, RET_V_DIM), BF16),
        grid=(bsz, l // t),
        in_specs=[pl.BlockSpec((1, t, RET_QK_DIM), lambda b, i: (b, i, 0)),
                  pl.BlockSpec((1, t, RET_QK_DIM), lambda b, i: (b, i, 1)),
                  pl.BlockSpec((1, t, RET_V_DIM), lambda b, i: (b, i, 1)),
                  pl.BlockSpec((1, t, RET_V_DIM), lambda b, i: (b, i, 2)),
                  pl.BlockSpec((1, t, LANES), lambda b, i: (b, i, 0)),
                  pl.BlockSpec((1, t, LANES), lambda b, i: (b, i, 0)),
                  full((RET_HEADS, c, c)), full((RET_HEADS, c, LANES)), full((RET_HEADS, c, LANES)),
                  pl.BlockSpec(memory_space=pltpu.SMEM)],
        out_specs=pl.BlockSpec((1, t, RET_V_DIM), lambda b, i: (b, i, 0)),
        scratch_shapes=[pltpu.VMEM((RET_HEADS, RET_QK_HEAD, RET_V_HEAD), F32)],
        compiler_params=_ARB2,
        name="ret_core",
    )(y, y, y, y, cos, sin, inner_decay, q_decay, k_decay, chunk_decay)


def _retention_mixer(u, positions, w_in, w_out, layer, res):
    bsz, l, dm = u.shape
    y = _proj(u.reshape(bsz * l, dm), w_in, layer)
    o = _ret_core(y.reshape(bsz, l, -1), positions)
    return _proj(o.reshape(bsz * l, RET_V_DIM), w_out, layer, mode="residual", res=res)


def _rope_tables(positions, dim):
    inv_freq = ROPE_THETA ** (-jnp.arange(0, dim, 2, dtype=F32) / dim)
    ang = positions.astype(F32)[..., None] * inv_freq
    c, s = jnp.cos(ang), jnp.sin(ang)
    reps = LANES // dim
    cos_full = jnp.tile(jnp.concatenate([c, c], axis=-1), (1, 1, reps))
    sin_signed = jnp.tile(jnp.concatenate([-s, s], axis=-1), (1, 1, reps))
    return cos_full, sin_signed


def _rope128(x, cos, sin_signed):
    return x * cos + pltpu.roll(x, 64, 1) * sin_signed


def _rope64(x, cos, sin_signed, low_half):
    partner = jnp.where(low_half, pltpu.roll(x, 96, 1), pltpu.roll(x, 32, 1))
    return x * cos + partner * sin_signed


def _dsa_prep_kernel(k_ref, v_ref, ki_ref, ck_ref, sk_ref, cik_ref, sik_ref, kb_ref, vb_ref, ki2_ref):
    ck, sk = ck_ref[0], sk_ref[0]
    for g in range(DSA_KV_HEADS):
        sl = slice(g * DSA_HEAD_DIM, (g + 1) * DSA_HEAD_DIM)
        kb_ref[0, g] = _rope128(k_ref[0, :, sl], ck, sk).astype(BF16)
        vb_ref[0, g] = v_ref[0, :, sl].astype(BF16)
    x = ki_ref[0]
    lane = _lane_iota(x.shape)
    xr = _rope64(x, cik_ref[0], sik_ref[0], (lane % IDX_HEAD_DIM) < IDX_HEAD_DIM // 2)
    ki2_ref[0] = jnp.where(lane < IDX_HEAD_DIM, xr, pltpu.roll(xr, IDX_HEAD_DIM, 1)).astype(BF16)


def _dsa_block(s_eff, q_ref, qi_ref, wi_ref, cq_ref, sq_ref, ciq_ref, siq_ref, kb_ref, vb_ref, ki2_ref, o_ref,
               key_ref, keyt_ref, *, topk):
    t_blk = DSA_BLOCK
    qb = pl.program_id(1)
    lane = _lane_iota((t_blk, LANES))
    low_half = (lane % IDX_HEAD_DIM) < IDX_HEAD_DIM // 2
    lo64 = lane < IDX_HEAD_DIM

    wi = wi_ref[0] * (IDX_HEADS ** -0.5 * IDX_HEAD_DIM ** -0.5)
    ciq, siq = ciq_ref[0], siq_ref[0]
    ki2 = ki2_ref[0, :s_eff, :]
    score = jnp.zeros((t_blk, s_eff), F32)
    for quad in range(IDX_HEADS // 4):
        rows = []
        for pr in (2 * quad, 2 * quad + 1):
            xr = _rope64(qi_ref[0, :, pr * LANES:(pr + 1) * LANES], ciq, siq, low_half)
            rows.append(jnp.where(lo64, xr, 0.0))
            rows.append(jnp.where(lo64, 0.0, xr))
        s4 = _bdot_nt(jnp.concatenate(rows, axis=0), ki2)
        for j in range(4):
            hh = 4 * quad + j
            w_col = wi[:, IDX_HEAD_DIM + hh:IDX_HEAD_DIM + hh + 1]
            score = score + w_col * jnp.maximum(s4[j * t_blk:(j + 1) * t_blk], 0.0)

    t_idx = qb * t_blk + lax.broadcasted_iota(jnp.int32, (t_blk, 1), 0)
    s_idx = _lane_iota((t_blk, s_eff))
    score = jnp.where(s_idx <= t_idx, score, -jnp.inf)
    bits = lax.bitcast_convert_type(score, jnp.int32)
    keys = bits ^ ((bits >> 31) & jnp.int32(0x7FFFFFFF))
    key_ref[:, :s_eff] = keys
    keyt_ref[:s_eff, :] = keys.T

    kk = jnp.minimum(qb * t_blk + _lane_iota((1, t_blk)) + 1, topk).astype(F32)
    int_min = jnp.int32(-2 ** 31)

    n_chain = 8
    rows = s_eff // n_chain

    def bisect(i, t_u):
        cand_u = t_u | lax.shift_left(jnp.int32(1), 31 - i)
        cand = jnp.broadcast_to(cand_u ^ int_min, (SUBLANES, t_blk))
        acc = [jnp.zeros((SUBLANES, t_blk), F32) for _ in range(n_chain)]
        for r in range(0, rows, SUBLANES):
            for c in range(n_chain):
                hit = keyt_ref[c * rows + r:c * rows + r + SUBLANES, :] >= cand
                acc[c] = acc[c] + jnp.where(hit, 1.0, 0.0)
        while len(acc) > 1:
            acc = [acc[k] + acc[k + 1] for k in range(0, len(acc), 2)]
        cnt = jnp.sum(acc[0], axis=0, keepdims=True)
        return jnp.where(cnt >= kk, cand_u, t_u)

    t_u = lax.fori_loop(0, 32, bisect, jnp.zeros((1, t_blk), jnp.int32))
    thr_row = t_u ^ int_min

    kt = keyt_ref[:s_eff, :]
    n_ge = jnp.sum(jnp.where(kt >= thr_row, 1.0, 0.0), axis=0, keepdims=True)

    @pl.when(jnp.max(n_ge - kk) > 0.0)
    def _():
        tied = jnp.where(kt == thr_row, 1.0, 0.0)
        need = kk - jnp.sum(jnp.where(kt > thr_row, 1.0, 0.0), axis=0, keepdims=True)
        row = lax.broadcasted_iota(jnp.int32, kt.shape, 0)
        before = tied
        sh = 1
        while sh < s_eff:
            before = before + jnp.where(row >= sh, pltpu.roll(before, sh, 0), 0.0)
            sh *= 2
        surplus = jnp.where((tied > 0.0) & (before > need), 1.0, 0.0)
        key_ref[:, :s_eff] = jnp.where(surplus.T > 0.0, int_min, key_ref[:, :s_eff])

    thr = jnp.broadcast_to(thr_row, (t_blk, t_blk)).T[:, :1]
    sel = key_ref[:, :s_eff] >= thr

    cq, sq = cq_ref[0], sq_ref[0]
    group = DSA_HEADS // DSA_KV_HEADS
    def group_logits(g):
        qs = []
        for j in range(group):
            hh = group * g + j
            qs.append(_rope128(q_ref[0, :, hh * DSA_HEAD_DIM:(hh + 1) * DSA_HEAD_DIM], cq, sq) * DSA_HEAD_DIM ** -0.5)
        return _bdot_nt(jnp.concatenate(qs, axis=0), kb_ref[0, g, :s_eff, :])

    next_logits = group_logits(0)
    for g in range(DSA_KV_HEADS):
        logits = next_logits
        if g + 1 < DSA_KV_HEADS:
            next_logits = group_logits(g + 1)
        ps, dens = [], []
        for j in range(group):
            lj = jnp.where(sel, logits[j * t_blk:(j + 1) * t_blk], -jnp.inf)
            pj = jnp.exp(lj - jnp.max(lj, axis=1, keepdims=True))
            dens.append(jnp.sum(pj, axis=1, keepdims=True))
            ps.append(pj)
        pv = _bdot(jnp.concatenate(ps, axis=0), vb_ref[0, g, :s_eff, :])
        for j in range(group):
            hh = group * g + j
            o_ref[0, :, hh * DSA_HEAD_DIM:(hh + 1) * DSA_HEAD_DIM] = (
                pv[j * t_blk:(j + 1) * t_blk] / dens[j]).astype(o_ref.dtype)


def _dsa_kernel(q_ref, qi_ref, wi_ref, cq_ref, sq_ref, ciq_ref, siq_ref, kb_ref, vb_ref, ki2_ref, o_ref, key_ref,
                keyt_ref, *, topk, seq):
    qb = pl.program_id(1)
    blocks_per_step = DSA_KV_STEP // DSA_BLOCK
    for br in range(seq // DSA_KV_STEP):
        @pl.when(qb // blocks_per_step == br)
        def _():
            _dsa_block((br + 1) * DSA_KV_STEP, q_ref, qi_ref, wi_ref, cq_ref, sq_ref, ciq_ref, siq_ref, kb_ref,
                       vb_ref, ki2_ref, o_ref, key_ref, keyt_ref, topk=topk)


def _dsa_core(y, yx, positions):
    bsz, l, _ = y.shape
    topk = min(DSA_TOPK_MAX, l // 4)
    cos_q, sin_q = _rope_tables(positions, DSA_HEAD_DIM)
    cos_i, sin_i = _rope_tables(positions, IDX_HEAD_DIM)
    k_blk = DSA_Q_DIM // DSA_KV_DIM
    ki_blk = 0
    ps = DSA_KV_STEP
    tab = pl.BlockSpec((1, ps, LANES), lambda b, i: (b, i, 0))
    kb, vb, ki2 = pl.pallas_call(
        _dsa_prep_kernel,
        out_shape=(jax.ShapeDtypeStruct((bsz, DSA_KV_HEADS, l, DSA_HEAD_DIM), BF16),
                   jax.ShapeDtypeStruct((bsz, DSA_KV_HEADS, l, DSA_HEAD_DIM), BF16),
                   jax.ShapeDtypeStruct((bsz, l, LANES), BF16)),
        grid=(bsz, l // ps),
        in_specs=[pl.BlockSpec((1, ps, DSA_KV_DIM), lambda b, i: (b, i, k_blk)),
                  pl.BlockSpec((1, ps, DSA_KV_DIM), lambda b, i: (b, i, k_blk + 1)),
                  pl.BlockSpec((1, ps, LANES), lambda b, i: (b, i, ki_blk)),
                  tab, tab, tab, tab],
        out_specs=(pl.BlockSpec((1, DSA_KV_HEADS, ps, DSA_HEAD_DIM), lambda b, i: (b, 0, i, 0)),
                   pl.BlockSpec((1, DSA_KV_HEADS, ps, DSA_HEAD_DIM), lambda b, i: (b, 0, i, 0)),
                   pl.BlockSpec((1, ps, LANES), lambda b, i: (b, i, 0))),
        compiler_params=_ARB2,
        name="dsa_prep",
    )(y, y, yx, cos_q, sin_q, cos_i, sin_i)

    t_blk = DSA_BLOCK
    qtab = pl.BlockSpec((1, t_blk, LANES), lambda b, i: (b, i, 0))
    kv_spec = pl.BlockSpec((1, DSA_KV_HEADS, l, DSA_HEAD_DIM), lambda b, i: (b, 0, 0, 0))
    return pl.pallas_call(
        functools.partial(_dsa_kernel, topk=topk, seq=l),
        out_shape=jax.ShapeDtypeStruct((bsz, l, DSA_Q_DIM), BF16),
        grid=(bsz, l // t_blk),
        in_specs=[pl.BlockSpec((1, t_blk, DSA_Q_DIM), lambda b, i: (b, i, 0)),
                  pl.BlockSpec((1, t_blk, IDX_HEADS * IDX_HEAD_DIM),
                               lambda b, i: (b, i, (DSA_Q_DIM + 2 * DSA_KV_DIM) // (IDX_HEADS * IDX_HEAD_DIM))),
                  pl.BlockSpec((1, t_blk, LANES), lambda b, i: (b, i, ki_blk)),
                  qtab, qtab, qtab, qtab,
                  kv_spec, kv_spec,
                  pl.BlockSpec((1, l, LANES), lambda b, i: (b, 0, 0))],
        out_specs=pl.BlockSpec((1, t_blk, DSA_Q_DIM), lambda b, i: (b, i, 0)),
        scratch_shapes=[pltpu.VMEM((t_blk, l), jnp.int32), pltpu.VMEM((l, t_blk), jnp.int32)],
        compiler_params=_ARB2,
        name="dsa_attn",
    )(y, y, yx, cos_q, sin_q, cos_i, sin_i, kb, vb, ki2)


def _dsa_mixer(u, positions, w_in, w_out, layer, res):
    bsz, l, dm = u.shape
    u2 = u.reshape(bsz * l, dm)
    n_main = DSA_Q_DIM + 2 * DSA_KV_DIM + IDX_HEADS * IDX_HEAD_DIM
    w_in_t = _transposed(w_in)
    y = _proj(u2, w_in_t, layer, n_out=n_main, w_t=True)
    yx = _proj_tail(u2, w_in_t, layer, n_main)
    o = _dsa_core(y.reshape(bsz, l, -1), yx.reshape(bsz, l, -1), positions)
    return _proj(o.reshape(bsz * l, DSA_Q_DIM), w_out, layer, mode="residual", res=res)


def kernel(x, p, positions, norm_mix, norm_mlp, norm_ple, norm_final, gdn_w_in, gdn_conv_w, gdn_a_log,
           gdn_dt_bias, gdn_norm, gdn_w_out, ret_w_in, ret_w_out, dsa_w_in, dsa_w_out, mlp_w_up, mlp_w_down,
           ple_w_gate, ple_w_proj):
    bsz, l, d = x.shape
    m = bsz * l
    h = x.reshape(m, d)
    for i in range(DEPTH):
        kind, j = i % N_MIXERS, i // N_MIXERS
        u = _rmsnorm(h, norm_mix[i], BF16).reshape(bsz, l, d)
        if kind == 0:
            h = _gdn_mixer(u, gdn_w_in, gdn_conv_w, j, gdn_a_log[j], gdn_dt_bias[j], gdn_norm[j], gdn_w_out, h)
        elif kind == 1:
            h = _retention_mixer(u, positions, ret_w_in, ret_w_out, j, h)
        else:
            h = _dsa_mixer(u, positions, dsa_w_in, dsa_w_out, j, h)
        u = _rmsnorm(h, norm_mlp[i], BF16)
        a = _proj(u, mlp_w_up, i, mode="relu2", out_dtype=BF16)
        h = _proj(a, mlp_w_down, i, mode="residual", res=h)
        u = _rmsnorm(h, norm_ple[i], BF16)
        h = _proj(u, ple_w_gate, i, mode="ple", res=h, p=p.reshape(DEPTH, m, -1), wp=ple_w_proj)
    return _rmsnorm(h, norm_final, x.dtype).reshape(bsz, l, d)
```

```python
import functools

import jax
import jax.numpy as jnp
from jax import lax
from jax.experimental import pallas as pl
from jax.experimental.pallas import tpu as pltpu

F32 = jnp.float32
BF16 = jnp.bfloat16

D_MODEL = 2048
DEPTH = 4
N_MIXERS = 3
NORM_EPS = 1e-6
ROPE_THETA = 10000.0

GDN_HEAD_DIM = 128
GDN_QK_HEADS = D_MODEL // 128
GDN_V_HEADS = 2 * GDN_QK_HEADS
GDN_CONV = 4
GDN_CHUNK = 64
GDN_KEY_DIM = GDN_QK_HEADS * GDN_HEAD_DIM
GDN_VAL_DIM = GDN_V_HEADS * GDN_HEAD_DIM
GDN_CONV_DIM = 2 * GDN_KEY_DIM + GDN_VAL_DIM
GDN_SUPER = 256
GDN_STEP_QK = 2

RET_HEADS = 8
RET_QK_DIM = D_MODEL
RET_V_DIM = 2 * D_MODEL
RET_QK_HEAD = RET_QK_DIM // RET_HEADS
RET_V_HEAD = RET_V_DIM // RET_HEADS
RET_CHUNK = 128
RET_STEP = 256

DSA_HEAD_DIM = 128
DSA_HEADS = D_MODEL // DSA_HEAD_DIM
DSA_KV_HEADS = 4
IDX_HEADS = 16
IDX_HEAD_DIM = 64
DSA_TOPK_MAX = 256
DSA_BLOCK = 128
DSA_KV_STEP = 512
DSA_Q_DIM = DSA_HEADS * DSA_HEAD_DIM
DSA_KV_DIM = DSA_KV_HEADS * DSA_HEAD_DIM

LANES = 128
SUBLANES = 8
VMEM_LIMIT_BYTES = 56 * 1024 * 1024

_ARB2 = pltpu.CompilerParams(dimension_semantics=("arbitrary", "arbitrary"), vmem_limit_bytes=VMEM_LIMIT_BYTES)


def _bdot(a, b):
    return jnp.dot(a.astype(BF16), b.astype(BF16), preferred_element_type=F32)


def _bdot_nt(a, b):
    return lax.dot_general(a.astype(BF16), b.astype(BF16), (((1,), (1,)), ((), ())), preferred_element_type=F32)


def _bdot_tn(a, b):
    return lax.dot_general(a.astype(BF16), b.astype(BF16), (((0,), (0,)), ((), ())), preferred_element_type=F32)


def _lane_iota(shape):
    return lax.broadcasted_iota(jnp.int32, shape, 1)


def _silu(x):
    return x * jax.nn.sigmoid(x)


def _mm_kernel(*refs, nk, mode, valid, w_t):
    x_ref, w_ref, o_ref = refs[0], refs[1], refs[-1]
    extra = refs[2:-1]
    if w_t:
        w = w_ref[...]
        if mode == "tail":
            w = jnp.where(lax.broadcasted_iota(jnp.int32, w.shape, 0) < valid, w, 0.0)
        o_ref[...] = _bdot_nt(x_ref[...], w)
        return
    if nk > 1:
        @pl.when(pl.program_id(2) == 0)
        def _():
            o_ref[...] = extra[0][...]

        half = o_ref.shape[1] // 2
        for cs in (slice(0, half), slice(half, 2 * half)):
            o_ref[:, cs] += _bdot(x_ref[...], w_ref[:, cs])
        return
    if mode == "tail":
        w = jnp.where(_lane_iota(w_ref.shape) < valid, w_ref[...], 0.0)
        o_ref[...] = _bdot(x_ref[...], w)
        return
    prod = _bdot(x_ref[...], w_ref[...])
    if mode == "plain":
        o_ref[...] = prod.astype(o_ref.dtype)
    elif mode == "relu2":
        o_ref[...] = jnp.square(jnp.maximum(prod, 0.0)).astype(o_ref.dtype)
    elif mode == "ple":
        res_ref, p_ref, wp_ref = extra
        o_ref[...] = res_ref[...] + jax.nn.sigmoid(prod) * _bdot(p_ref[...], wp_ref[...])
    else:
        o_ref[...] = extra[0][...] + prod


def _matmul(x, w, layer, *, n_out, bm, bn, bk, n_outer, col0=0, mode="plain", out_dtype=F32, res=None, p=None,
            wp=None, w_t=False):
    m, kdim = x.shape
    bm = min(bm, m)
    n_total = w.shape[1] if w_t else w.shape[2]
    assert m % bm == 0 and n_out % bn == 0 and col0 % bn == 0 and kdim % bk == 0, (x.shape, w.shape, bm, bn, bk)
    assert col0 + n_out <= n_total or (mode == "tail" and n_out == bn)
    nk = kdim // bk
    assert nk == 1 or mode == "residual"
    assert not w_t or mode in ("plain", "tail")
    cb0 = col0 // bn
    if n_outer:
        grid = (n_out // bn, m // bm, nk)
        mi, ni = (lambda j, i, k: i), (lambda j, i, k: j)
    else:
        grid = (m // bm, n_out // bn, nk)
        mi, ni = (lambda i, j, k: i), (lambda i, j, k: j)
    in_specs = [pl.BlockSpec((bm, bk), lambda a, b, k: (mi(a, b, k), k)),
                pl.BlockSpec((None, bn, bk), lambda a, b, k: (layer, cb0 + ni(a, b, k), k)) if w_t else
                pl.BlockSpec((None, bk, bn), lambda a, b, k: (layer, k, cb0 + ni(a, b, k)))]
    args = [x, w]
    o_map = lambda a, b, k: (mi(a, b, k), ni(a, b, k))
    if mode in ("residual", "ple"):
        in_specs.append(pl.BlockSpec((bm, bn), o_map))
        args.append(res)
    if mode == "ple":
        kp = p.shape[2]
        in_specs.append(pl.BlockSpec((None, bm, kp), lambda a, b, k: (layer, mi(a, b, k), 0)))
        in_specs.append(pl.BlockSpec((None, kp, bn), lambda a, b, k: (layer, 0, ni(a, b, k))))
        args += [p, wp]
    return pl.pallas_call(
        functools.partial(_mm_kernel, nk=nk, mode=mode, valid=n_total - col0, w_t=w_t),
        out_shape=jax.ShapeDtypeStruct((m, n_out), out_dtype),
        grid=grid,
        in_specs=in_specs,
        out_specs=pl.BlockSpec((bm, bn), o_map),
        compiler_params=pltpu.CompilerParams(
            dimension_semantics=("arbitrary", "arbitrary", "arbitrary"), vmem_limit_bytes=VMEM_LIMIT_BYTES),
        name="matmul_" + mode,
    )(*args)


def _proj(x, w, layer, n_out=None, **kw):
    w_t = kw.get("w_t", False)
    kdim = w.shape[2] if w_t else w.shape[1]
    n_out = (w.shape[1] if w_t else w.shape[2]) if n_out is None else n_out
    if kdim <= 2048:
        bm = 512 if kw.get("mode") == "ple" else 1024
        return _matmul(x, w, layer, n_out=n_out, bm=bm, bn=min(n_out, 1024), bk=kdim, n_outer=True, **kw)
    return _matmul(x, w, layer, n_out=n_out, bm=2048, bn=1024, bk=1024, n_outer=False, **kw)


def _transposed(w):
    return jnp.swapaxes(w, 1, 2)


def _rmsnorm_kernel(x_ref, g_ref, *refs, valid):
    o_ref = refs[-2] if valid else refs[-1]
    x = x_ref[...]
    y = (x * lax.rsqrt(jnp.mean(x * x, axis=1, keepdims=True) + NORM_EPS) * g_ref[...]).astype(o_ref.dtype)
    o_ref[...] = y
    if valid:
        w = refs[0][...]
        w = jnp.where(lax.broadcasted_iota(jnp.int32, w.shape, 0) < valid, w, 0.0)
        refs[-1][...] = _bdot_nt(y, w)


def _rmsnorm(x, gain, out_dtype, tail=None):
    m, d = x.shape
    bm = min(m, 512)
    in_specs = [pl.BlockSpec((bm, d), lambda i: (i, 0)), pl.BlockSpec((1, d), lambda i: (0, 0))]
    out_shape = [jax.ShapeDtypeStruct((m, d), out_dtype)]
    out_specs = [pl.BlockSpec((bm, d), lambda i: (i, 0))]
    args, valid = [x, gain.reshape(1, d)], 0
    if tail is not None:
        w_t, layer, start = tail
        valid = w_t.shape[1] - start
        in_specs.append(pl.BlockSpec((None, LANES, d), lambda i: (layer, start // LANES, 0)))
        out_shape.append(jax.ShapeDtypeStruct((m, LANES), F32))
        out_specs.append(pl.BlockSpec((bm, LANES), lambda i: (i, 0)))
        args.append(w_t)
    out = pl.pallas_call(
        functools.partial(_rmsnorm_kernel, valid=valid),
        out_shape=out_shape,
        grid=(m // bm,),
        in_specs=in_specs,
        out_specs=out_specs,
        compiler_params=pltpu.CompilerParams(dimension_semantics=("arbitrary",), vmem_limit_bytes=VMEM_LIMIT_BYTES),
        name="rmsnorm",
    )(*args)
    return out if tail is not None else out[0]


def _gdn_gates_kernel(gb_ref, alog_ref, dtb_ref, o_ref):
    x = gb_ref[0]
    lane = _lane_iota(x.shape)
    row = lax.broadcasted_iota(jnp.int32, x.shape, 0) % GDN_CHUNK
    g = -jnp.exp(alog_ref[...]) * jax.nn.softplus(x + dtb_ref[...])
    sh = 1
    while sh < GDN_CHUNK:
        g = g + jnp.where(row >= sh, pltpu.roll(g, sh, 0), 0.0)
        sh *= 2
    o_ref[0] = jnp.where(lane < GDN_V_HEADS, jax.nn.sigmoid(x), g)


def _conv_silu_tile(x, tail, w):
    rows8 = lax.broadcasted_iota(jnp.int32, tail.shape, 0)
    acc = x * w[GDN_CONV - 1:GDN_CONV]
    for sh in range(1, GDN_CONV):
        xs = pltpu.roll(x, sh, 0)
        head = jnp.where(rows8 < sh, pltpu.roll(tail, sh, 0), xs[:SUBLANES])
        xs = jnp.concatenate([head, xs[SUBLANES:]], axis=0)
        acc = acc + xs * w[GDN_CONV - 1 - sh:GDN_CONV - sh]
    return _silu(acc)


def _gdn_kernel(q_ref, k_ref, v_ref, z_ref, gt_ref, wq_ref, wk_ref, wv_ref, nw_ref, o_ref,
                s_ref, kn_st, wk_st, qg_st, gl_st, in_st, *, seq):
    hq = pl.program_id(1)
    r_blk, c_blk, d = GDN_SUPER, GDN_CHUNK, GDN_HEAD_DIM
    ii = lax.broadcasted_iota(jnp.int32, (r_blk, r_blk), 0)
    jj = lax.broadcasted_iota(jnp.int32, (r_blk, r_blk), 1)
    blk_masks = [(ii >> sh) == (jj >> sh) for sh in (3, 4, 5, 6)]
    blk8 = blk_masks[0]
    tri_incl = blk_masks[3] & (ii >= jj)
    diag = ii == jj
    eye = jnp.where(diag, 1.0, 0.0)
    lane = _lane_iota((r_blk, LANES))
    s_ref[...] = jnp.zeros_like(s_ref)
    wq, wk, wv, nw = wq_ref[...], wk_ref[...], wv_ref[...], nw_ref[...]
    n_steps = seq // r_blk
    rep = GDN_V_HEADS // GDN_QK_HEADS

    def fetch(s):
        r0 = pl.multiple_of(s * r_blk, r_blk)
        rt = pl.multiple_of(jnp.maximum(r0 - SUBLANES, 0), SUBLANES)
        tail = lambda ref: jnp.where(s > 0, ref[0, pl.ds(rt, SUBLANES), :], 0.0)
        rows = pl.ds(r0, r_blk)
        return (q_ref[0, rows, :], tail(q_ref), k_ref[0, rows, :], tail(k_ref), v_ref[0, rows, :], tail(v_ref),
                gt_ref[0, rows, :])

    heads = range(GDN_STEP_QK * rep)
    qk_of = lambda j: j // rep

    def solve(raw, slot):
        xq, tq, xk, tk, xv, tv, gt = raw
        qc, kc, vc = _conv_silu_tile(xq, tq, wq), _conv_silu_tile(xk, tk, wk), _conv_silu_tile(xv, tv, wv)
        qn, kn = [], []
        for i in range(GDN_STEP_QK):
            qi, ki = qc[:, i * d:(i + 1) * d], kc[:, i * d:(i + 1) * d]
            qn.append(qi * lax.rsqrt(jnp.sum(qi * qi, axis=1, keepdims=True) + NORM_EPS) * d ** -0.5)
            kn.append(ki * lax.rsqrt(jnp.sum(ki * ki, axis=1, keepdims=True) + NORM_EPS))
            kn_st[slot, i] = kn[i]
        hv0 = len(heads) * hq
        beta = [jnp.sum(jnp.where(lane == hv0 + j, gt, 0.0), axis=1, keepdims=True) for j in heads]
        g = [jnp.sum(jnp.where(lane == GDN_V_HEADS + hv0 + j, gt, 0.0), axis=1, keepdims=True) for j in heads]
        gi = [jnp.broadcast_to(g[j], (r_blk, r_blk)) for j in heads]
        dec = [jnp.exp(jnp.where(tri_incl, gi[j] - jnp.sum(jnp.where(diag, gi[j], 0.0), axis=0, keepdims=True),
                                 -jnp.inf)) for j in heads]
        kb = [kn[qk_of(j)] * beta[j] for j in heads]
        for j in heads:
            qg_st[slot, j] = qn[qk_of(j)] * jnp.exp(g[j])
            gl_st[slot, j] = gi[j][:, :d]
        qk = [_bdot_nt(qn[i], kn[i]) for i in range(GDN_STEP_QK)]
        kk = [_bdot_nt(kb[j], kn[qk_of(j)]) for j in heads]
        yield
        for j in heads:
            in_st[slot, j] = qk[qk_of(j)] * dec[j]
        a = [jnp.where(diag, 0.0, kk[j] * dec[j]) for j in heads]
        a8 = [jnp.where(blk8, a[j], 0.0) for j in heads]
        a8_2 = [_bdot(a8[j], a8[j]) for j in heads]
        yield
        inv = [eye - a8[j] for j in heads]
        t = [_bdot(inv[j], a8_2[j]) for j in heads]
        a8_4 = [_bdot(a8_2[j], a8_2[j]) for j in heads]
        yield
        inv = [inv[j] + t[j] for j in heads]
        t = [_bdot(inv[j], a8_4[j]) for j in heads]
        yield
        inv = [inv[j] + t[j] for j in heads]
        for lvl in range(3):
            off = [jnp.where(blk_masks[lvl + 1] & ~blk_masks[lvl], a[j], 0.0) for j in heads]
            t = [_bdot(off[j], inv[j]) for j in heads]
            yield
            t = [_bdot(inv[j], t[j]) for j in heads]
            yield
            inv = [inv[j] - t[j] for j in heads]
        rhs = [jnp.concatenate([vc[:, j * d:(j + 1) * d] * beta[j], kb[j] * jnp.exp(g[j])], axis=1) for j in heads]
        x = [_bdot(inv[j], rhs[j]) for j in heads]
        yield
        for j in heads:
            wk_st[slot, j] = x[j]

    def recur(s, slot):
        r0 = pl.multiple_of(s * r_blk, r_blk)
        state = [s_ref[j] for j in heads]
        for c in range(r_blk // c_blk):
            rs = slice(c * c_blk, (c + 1) * c_blk)
            gl = [gl_st[slot, j, rs, :] for j in heads]
            g_last = [gl[j][c_blk - 1:c_blk, :] for j in heads]
            both = [_bdot(jnp.concatenate([wk_st[slot, j, rs, d:], qg_st[slot, j, rs, :]], axis=0), state[j])
                    for j in heads]
            yield
            v_new = [wk_st[slot, j, rs, :d] - both[j][:c_blk] for j in heads]
            kd = [kn_st[slot, qk_of(j), rs, :] * jnp.exp(g_last[j] - gl[j]) for j in heads]
            upd = [_bdot_tn(kd[j], v_new[j]) for j in heads]
            pad = []
            for j in heads:
                parts = []
                if c > 0:
                    parts.append(jnp.zeros((c * c_blk, d), F32))
                parts.append(v_new[j])
                if (c + 1) * c_blk < r_blk:
                    parts.append(jnp.zeros((r_blk - (c + 1) * c_blk, d), F32))
                pad.append(jnp.concatenate(parts, axis=0))
            o_in = [_bdot(in_st[slot, j, rs, :], pad[j]) for j in heads]
            yield
            state = [state[j] * jnp.exp(g_last[j]) + upd[j] for j in heads]
            rows = pl.ds(r0 + c * c_blk, c_blk)
            for j in heads:
                o_c = both[j][c_blk:] + o_in[j]
                on = o_c * lax.rsqrt(jnp.mean(o_c * o_c, axis=1, keepdims=True) + NORM_EPS) * nw
                zc = z_ref[0, rows, j * d:(j + 1) * d]
                o_ref[0, rows, j * d:(j + 1) * d] = (on * _silu(zc)).astype(o_ref.dtype)
        for j in heads:
            s_ref[j] = state[j]

    def run(*gens):
        live = list(gens)
        while live:
            live = [gen for gen in live if next(gen, StopIteration) is not StopIteration]

    run(solve(fetch(0), 0))

    def body(i, carry):
        s = 2 * i
        run(solve(fetch(s + 1), 1), recur(s, 0))
        run(solve(fetch(s + 2), 0), recur(s + 1, 1))
        return carry

    lax.fori_loop(0, n_steps // 2 - 1, body, 0)
    run(solve(fetch(n_steps - 1), 1), recur(n_steps - 2, 0))
    run(recur(n_steps - 1, 1))


def _gdn_core(y, yg, conv_w, layer, a_log, dt_bias, norm_w):
    bsz, l, _ = y.shape
    assert l % (2 * GDN_SUPER) == 0
    nh, d = GDN_QK_HEADS, GDN_HEAD_DIM
    rep = GDN_V_HEADS // GDN_QK_HEADS
    lane_vec = lambda t: jnp.zeros((1, LANES), F32).at[0, GDN_V_HEADS:2 * GDN_V_HEADS].set(t.astype(F32))
    gates = pl.pallas_call(
        _gdn_gates_kernel,
        out_shape=jax.ShapeDtypeStruct((bsz, l, LANES), F32),
        grid=(bsz,),
        in_specs=[pl.BlockSpec((1, l, LANES), lambda b: (b, 0, 0)),
                  pl.BlockSpec((1, LANES), lambda b: (0, 0)), pl.BlockSpec((1, LANES), lambda b: (0, 0))],
        out_specs=pl.BlockSpec((1, l, LANES), lambda b: (b, 0, 0)),
        compiler_params=pltpu.CompilerParams(dimension_semantics=("arbitrary",), vmem_limit_bytes=VMEM_LIMIT_BYTES),
        name="gdn_gates",
    )(yg, lane_vec(a_log), lane_vec(dt_bias))
    r_blk = GDN_SUPER
    nq = GDN_STEP_QK
    qw, vw = nq * d, nq * rep * d
    nv = nq * rep
    k0, v0, z0 = GDN_KEY_DIM // qw, 2 * GDN_KEY_DIM // vw, GDN_CONV_DIM // vw
    return pl.pallas_call(
        functools.partial(_gdn_kernel, seq=l),
        out_shape=jax.ShapeDtypeStruct((bsz, l, GDN_VAL_DIM), BF16),
        grid=(bsz, nh // nq),
        in_specs=[pl.BlockSpec((1, l, qw), lambda b, h: (b, 0, h)),
                  pl.BlockSpec((1, l, qw), lambda b, h: (b, 0, k0 + h)),
                  pl.BlockSpec((1, l, vw), lambda b, h: (b, 0, v0 + h)),
                  pl.BlockSpec((1, l, vw), lambda b, h: (b, 0, z0 + h)),
                  pl.BlockSpec((1, l, LANES), lambda b, h: (b, 0, 0)),
                  pl.BlockSpec((None, GDN_CONV, qw), lambda b, h: (layer, 0, h)),
                  pl.BlockSpec((None, GDN_CONV, qw), lambda b, h: (layer, 0, k0 + h)),
                  pl.BlockSpec((None, GDN_CONV, vw), lambda b, h: (layer, 0, v0 + h)),
                  pl.BlockSpec((1, d), lambda b, h: (0, 0))],
        out_specs=pl.BlockSpec((1, l, vw), lambda b, h: (b, 0, h)),
        scratch_shapes=[pltpu.VMEM((nv, d, d), F32),
                        pltpu.VMEM((2, nq, r_blk, d), F32),
                        pltpu.VMEM((2, nv, r_blk, 2 * d), F32),
                        pltpu.VMEM((2, nv, r_blk, d), F32),
                        pltpu.VMEM((2, nv, r_blk, d), F32),
                        pltpu.VMEM((2, nv, r_blk, r_blk), F32)],
        compiler_params=_ARB2,
        name="gdn_core",
    )(y, y, y, y, gates, conv_w, conv_w, conv_w, norm_w.reshape(1, d))


def _gdn_mixer(res, gain, bsz, w_in, conv_w, layer, a_log, dt_bias, norm_w, w_out):
    l = res.shape[0] // bsz
    n_main = GDN_CONV_DIM + GDN_VAL_DIM
    w_in_t = _transposed(w_in)
    u2, yg = _rmsnorm(res, gain, BF16, tail=(w_in_t, layer, n_main))
    y = _proj(u2, w_in_t, layer, n_out=n_main, w_t=True)
    o = _gdn_core(y.reshape(bsz, l, -1), yg.reshape(bsz, l, -1), conv_w, layer, a_log, dt_bias, norm_w)
    return _proj(o.reshape(bsz * l, GDN_VAL_DIM), w_out, layer, mode="residual", res=res)


def _ret_kernel(q_ref, k_ref, v_ref, gate_ref, cos_ref, sin_ref, dec_ref, qd_ref, kd_ref, cd_ref, o_ref, s_ref):
    @pl.when(pl.program_id(1) == 0)
    def _():
        s_ref[...] = jnp.zeros_like(s_ref)

    half = RET_QK_HEAD // 2

    def rope(ref, rs, h, c, s):
        x1 = ref[0, rs, h * RET_QK_HEAD:h * RET_QK_HEAD + half]
        x2 = ref[0, rs, h * RET_QK_HEAD + half:(h + 1) * RET_QK_HEAD]
        return jnp.concatenate([x1 * c - x2 * s, x2 * c + x1 * s], axis=1)

    heads = range(RET_HEADS)
    vs = [slice(h * RET_V_HEAD, (h + 1) * RET_V_HEAD) for h in heads]
    for c in range(RET_STEP // RET_CHUNK):
        rs = slice(c * RET_CHUNK, (c + 1) * RET_CHUNK)
        cos, sin = cos_ref[0, rs, :], sin_ref[0, rs, :]
        qr = [rope(q_ref, rs, h, cos, sin) for h in heads]
        kr = [rope(k_ref, rs, h, cos, sin) * RET_QK_HEAD ** -0.5 for h in heads]
        sc = [_bdot_nt(qr[h], kr[h]) for h in heads]
        cross = [_bdot(qr[h] * jnp.concatenate([qd_ref[h], qd_ref[h]], axis=1), s_ref[h]) for h in heads]
        upd = [_bdot_tn(kr[h] * jnp.concatenate([kd_ref[h], kd_ref[h]], axis=1), v_ref[0, rs, vs[h]]) for h in heads]
        inner = [_bdot(sc[h] * dec_ref[h], v_ref[0, rs, vs[h]]) for h in heads]
        for h in heads:
            s_ref[h] = s_ref[h] * cd_ref[h] + upd[h]
            o = inner[h] + cross[h]
            on = o * lax.rsqrt(jnp.mean(o * o, axis=1, keepdims=True) + NORM_EPS)
            o_ref[0, rs, vs[h]] = (_silu(gate_ref[0, rs, vs[h]]) * on).astype(o_ref.dtype)


def _ret_core(y, positions):
    bsz, l, _ = y.shape
    c = RET_CHUNK
    inv_freq = ROPE_THETA ** (-jnp.arange(0, RET_QK_HEAD, 2, dtype=F32) / RET_QK_HEAD)
    ang = positions.astype(F32)[..., None] * inv_freq
    cos, sin = jnp.cos(ang), jnp.sin(ang)
    log_gamma = jnp.log1p(-jnp.exp2(-5.0 - jnp.arange(RET_HEADS, dtype=F32)))
    idx = jnp.arange(c, dtype=F32)
    rel = idx[:, None] - idx[None, :]
    inner_decay = jnp.where(rel >= 0, jnp.exp(log_gamma[:, None, None] * jnp.maximum(rel, 0.0)), 0.0)
    lanes = lambda t: jnp.broadcast_to(t[:, :, None], (RET_HEADS, c, LANES))
    q_decay = lanes(jnp.exp(log_gamma[:, None] * (idx[None, :] + 1.0)))
    k_decay = lanes(jnp.exp(log_gamma[:, None] * (c - 1.0 - idx[None, :])))
    chunk_decay = jnp.exp(log_gamma * c)
    t = RET_STEP
    full = lambda shape: pl.BlockSpec(shape, lambda b, i: (0,) * len(shape))
    return pl.pallas_call(
        _ret_kernel,
        out_shape=jax.ShapeDtypeStruct((bsz, l, RET_V_DIM), BF16),
        grid=(bsz, l // t),
        in_specs=[pl.BlockSpec((1, t, RET_QK_DIM), lambda b, i: (b, i, 0)),
                  pl.BlockSpec((1, t, RET_QK_DIM), lambda b, i: (b, i, 1)),
                  pl.BlockSpec((1, t, RET_V_DIM), lambda b, i: (b, i, 1)),
                  pl.BlockSpec((1, t, RET_V_DIM), lambda b, i: (b, i, 2)),
                  pl.BlockSpec((1, t, LANES), lambda b, i: (b, i, 0)),
                  pl.BlockSpec((1, t, LANES), lambda b, i: (b, i, 0)),
                  full((RET_HEADS, c, c)), full((RET_HEADS, c, LANES)), full((RET_HEADS, c, LANES)),
                  pl.BlockSpec(memory_space=pltpu.SMEM)],
        out_specs=pl.BlockSpec((1, t, RET_V_DIM), lambda b, i: (b, i, 0)),
        scratch_shapes=[pltpu.VMEM((RET_HEADS, RET_QK_HEAD, RET_V_HEAD), F32)],
        compiler_params=_ARB2,
        name="ret_core",
    )(y, y, y, y, cos, sin, inner_decay, q_decay, k_decay, chunk_decay)


def _retention_mixer(u, positions, w_in, w_out, layer, res):
    bsz, l, dm = u.shape
    y = _proj(u.reshape(bsz * l, dm), w_in, layer)
    o = _ret_core(y.reshape(bsz, l, -1), positions)
    return _proj(o.reshape(bsz * l, RET_V_DIM), w_out, layer, mode="residual", res=res)


def _rope_tables(positions, dim):
    inv_freq = ROPE_THETA ** (-jnp.arange(0, dim, 2, dtype=F32) / dim)
    ang = positions.astype(F32)[..., None] * inv_freq
    c, s = jnp.cos(ang), jnp.sin(ang)
    reps = LANES // dim
    cos_full = jnp.tile(jnp.concatenate([c, c], axis=-1), (1, 1, reps))
    sin_signed = jnp.tile(jnp.concatenate([-s, s], axis=-1), (1, 1, reps))
    return cos_full, sin_signed


def _rope128(x, cos, sin_signed):
    return x * cos + pltpu.roll(x, 64, 1) * sin_signed


def _rope64(x, cos, sin_signed, low_half):
    partner = jnp.where(low_half, pltpu.roll(x, 96, 1), pltpu.roll(x, 32, 1))
    return x * cos + partner * sin_signed


def _dsa_prep_kernel(k_ref, v_ref, ki_ref, ck_ref, sk_ref, cik_ref, sik_ref, kb_ref, vb_ref, ki2_ref):
    ck, sk = ck_ref[0], sk_ref[0]
    for g in range(DSA_KV_HEADS):
        sl = slice(g * DSA_HEAD_DIM, (g + 1) * DSA_HEAD_DIM)
        kb_ref[0, g] = _rope128(k_ref[0, :, sl], ck, sk).astype(BF16)
        vb_ref[0, g] = v_ref[0, :, sl].astype(BF16)
    x = ki_ref[0]
    lane = _lane_iota(x.shape)
    xr = _rope64(x, cik_ref[0], sik_ref[0], (lane % IDX_HEAD_DIM) < IDX_HEAD_DIM // 2)
    ki2_ref[0] = jnp.where(lane < IDX_HEAD_DIM, xr, pltpu.roll(xr, IDX_HEAD_DIM, 1)).astype(BF16)


def _dsa_block(s_eff, q_ref, qi_ref, wi_ref, cq_ref, sq_ref, ciq_ref, siq_ref, kb_ref, vb_ref, ki2_ref, o_ref,
               key_ref, keyt_ref, *, topk):
    t_blk = DSA_BLOCK
    qb = pl.program_id(1)
    lane = _lane_iota((t_blk, LANES))
    low_half = (lane % IDX_HEAD_DIM) < IDX_HEAD_DIM // 2
    lo64 = lane < IDX_HEAD_DIM

    wi = wi_ref[0] * (IDX_HEADS ** -0.5 * IDX_HEAD_DIM ** -0.5)
    ciq, siq = ciq_ref[0], siq_ref[0]
    ki2 = ki2_ref[0, :s_eff, :]
    score = jnp.zeros((t_blk, s_eff), F32)
    for quad in range(IDX_HEADS // 4):
        rows = []
        for pr in (2 * quad, 2 * quad + 1):
            xr = _rope64(qi_ref[0, :, pr * LANES:(pr + 1) * LANES], ciq, siq, low_half)
            rows.append(jnp.where(lo64, xr, 0.0))
            rows.append(jnp.where(lo64, 0.0, xr))
        s4 = _bdot_nt(jnp.concatenate(rows, axis=0), ki2)
        for j in range(4):
            hh = 4 * quad + j
            w_col = wi[:, IDX_HEAD_DIM + hh:IDX_HEAD_DIM + hh + 1]
            score = score + w_col * jnp.maximum(s4[j * t_blk:(j + 1) * t_blk], 0.0)

    t_idx = qb * t_blk + lax.broadcasted_iota(jnp.int32, (t_blk, 1), 0)
    s_idx = _lane_iota((t_blk, s_eff))
    score = jnp.where(s_idx <= t_idx, score, -jnp.inf)
    bits = lax.bitcast_convert_type(score, jnp.int32)
    keys = bits ^ ((bits >> 31) & jnp.int32(0x7FFFFFFF))
    key_ref[:, :s_eff] = keys
    keyt_ref[:s_eff, :] = keys.T

    kk = jnp.minimum(qb * t_blk + _lane_iota((1, t_blk)) + 1, topk).astype(F32)
    int_min = jnp.int32(-2 ** 31)

    n_chain = 8
    rows = s_eff // n_chain

    def bisect(i, t_u):
        cand_u = t_u | lax.shift_left(jnp.int32(1), 31 - i)
        cand = jnp.broadcast_to(cand_u ^ int_min, (SUBLANES, t_blk))
        acc = [jnp.zeros((SUBLANES, t_blk), F32) for _ in range(n_chain)]
        for r in range(0, rows, SUBLANES):
            for c in range(n_chain):
                hit = keyt_ref[c * rows + r:c * rows + r + SUBLANES, :] >= cand
                acc[c] = acc[c] + jnp.where(hit, 1.0, 0.0)
        while len(acc) > 1:
            acc = [acc[k] + acc[k + 1] for k in range(0, len(acc), 2)]
        cnt = jnp.sum(acc[0], axis=0, keepdims=True)
        return jnp.where(cnt >= kk, cand_u, t_u)

    t_u = lax.fori_loop(0, 32, bisect, jnp.zeros((1, t_blk), jnp.int32))
    thr_row = t_u ^ int_min

    kt = keyt_ref[:s_eff, :]
    n_ge = jnp.sum(jnp.where(kt >= thr_row, 1.0, 0.0), axis=0, keepdims=True)

    @pl.when(jnp.max(n_ge - kk) > 0.0)
    def _():
        tied = jnp.where(kt == thr_row, 1.0, 0.0)
        need = kk - jnp.sum(jnp.where(kt > thr_row, 1.0, 0.0), axis=0, keepdims=True)
        row = lax.broadcasted_iota(jnp.int32, kt.shape, 0)
        before = tied
        sh = 1
        while sh < s_eff:
            before = before + jnp.where(row >= sh, pltpu.roll(before, sh, 0), 0.0)
            sh *= 2
        surplus = jnp.where((tied > 0.0) & (before > need), 1.0, 0.0)
        key_ref[:, :s_eff] = jnp.where(surplus.T > 0.0, int_min, key_ref[:, :s_eff])

    thr = jnp.broadcast_to(thr_row, (t_blk, t_blk)).T[:, :1]
    sel = key_ref[:, :s_eff] >= thr

    cq, sq = cq_ref[0], sq_ref[0]
    group = DSA_HEADS // DSA_KV_HEADS
    def group_logits(g):
        qs = []
        for j in range(group):
            hh = group * g + j
            qs.append(_rope128(q_ref[0, :, hh * DSA_HEAD_DIM:(hh + 1) * DSA_HEAD_DIM], cq, sq) * DSA_HEAD_DIM ** -0.5)
        return _bdot_nt(jnp.concatenate(qs, axis=0), kb_ref[0, g, :s_eff, :])

    next_logits = group_logits(0)
    for g in range(DSA_KV_HEADS):
        logits = next_logits
        if g + 1 < DSA_KV_HEADS:
            next_logits = group_logits(g + 1)
        ps, dens = [], []
        for j in range(group):
            lj = jnp.where(sel, logits[j * t_blk:(j + 1) * t_blk], -jnp.inf)
            pj = jnp.exp(lj - jnp.max(lj, axis=1, keepdims=True))
            dens.append(jnp.sum(pj, axis=1, keepdims=True))
            ps.append(pj)
        pv = _bdot(jnp.concatenate(ps, axis=0), vb_ref[0, g, :s_eff, :])
        for j in range(group):
            hh = group * g + j
            o_ref[0, :, hh * DSA_HEAD_DIM:(hh + 1) * DSA_HEAD_DIM] = (
                pv[j * t_blk:(j + 1) * t_blk] / dens[j]).astype(o_ref.dtype)


def _dsa_kernel(q_ref, qi_ref, wi_ref, cq_ref, sq_ref, ciq_ref, siq_ref, kb_ref, vb_ref, ki2_ref, o_ref, key_ref,
                keyt_ref, *, topk, seq):
    qb = pl.program_id(1)
    blocks_per_step = DSA_KV_STEP // DSA_BLOCK
    for br in range(seq // DSA_KV_STEP):
        @pl.when(qb // blocks_per_step == br)
        def _():
            _dsa_block((br + 1) * DSA_KV_STEP, q_ref, qi_ref, wi_ref, cq_ref, sq_ref, ciq_ref, siq_ref, kb_ref,
                       vb_ref, ki2_ref, o_ref, key_ref, keyt_ref, topk=topk)


def _dsa_core(y, yx, positions):
    bsz, l, _ = y.shape
    topk = min(DSA_TOPK_MAX, l // 4)
    cos_q, sin_q = _rope_tables(positions, DSA_HEAD_DIM)
    cos_i, sin_i = _rope_tables(positions, IDX_HEAD_DIM)
    k_blk = DSA_Q_DIM // DSA_KV_DIM
    ki_blk = 0
    ps = DSA_KV_STEP
    tab = pl.BlockSpec((1, ps, LANES), lambda b, i: (b, i, 0))
    kb, vb, ki2 = pl.pallas_call(
        _dsa_prep_kernel,
        out_shape=(jax.ShapeDtypeStruct((bsz, DSA_KV_HEADS, l, DSA_HEAD_DIM), BF16),
                   jax.ShapeDtypeStruct((bsz, DSA_KV_HEADS, l, DSA_HEAD_DIM), BF16),
                   jax.ShapeDtypeStruct((bsz, l, LANES), BF16)),
        grid=(bsz, l // ps),
        in_specs=[pl.BlockSpec((1, ps, DSA_KV_DIM), lambda b, i: (b, i, k_blk)),
                  pl.BlockSpec((1, ps, DSA_KV_DIM), lambda b, i: (b, i, k_blk + 1)),
                  pl.BlockSpec((1, ps, LANES), lambda b, i: (b, i, ki_blk)),
                  tab, tab, tab, tab],
        out_specs=(pl.BlockSpec((1, DSA_KV_HEADS, ps, DSA_HEAD_DIM), lambda b, i: (b, 0, i, 0)),
                   pl.BlockSpec((1, DSA_KV_HEADS, ps, DSA_HEAD_DIM), lambda b, i: (b, 0, i, 0)),
                   pl.BlockSpec((1, ps, LANES), lambda b, i: (b, i, 0))),
        compiler_params=_ARB2,
        name="dsa_prep",
    )(y, y, yx, cos_q, sin_q, cos_i, sin_i)

    t_blk = DSA_BLOCK
    qtab = pl.BlockSpec((1, t_blk, LANES), lambda b, i: (b, i, 0))
    kv_spec = pl.BlockSpec((1, DSA_KV_HEADS, l, DSA_HEAD_DIM), lambda b, i: (b, 0, 0, 0))
    return pl.pallas_call(
        functools.partial(_dsa_kernel, topk=topk, seq=l),
        out_shape=jax.ShapeDtypeStruct((bsz, l, DSA_Q_DIM), BF16),
        grid=(bsz, l // t_blk),
        in_specs=[pl.BlockSpec((1, t_blk, DSA_Q_DIM), lambda b, i: (b, i, 0)),
                  pl.BlockSpec((1, t_blk, IDX_HEADS * IDX_HEAD_DIM),
                               lambda b, i: (b, i, (DSA_Q_DIM + 2 * DSA_KV_DIM) // (IDX_HEADS * IDX_HEAD_DIM))),
                  pl.BlockSpec((1, t_blk, LANES), lambda b, i: (b, i, ki_blk)),
                  qtab, qtab, qtab, qtab,
                  kv_spec, kv_spec,
                  pl.BlockSpec((1, l, LANES), lambda b, i: (b, 0, 0))],
        out_specs=pl.BlockSpec((1, t_blk, DSA_Q_DIM), lambda b, i: (b, i, 0)),
        scratch_shapes=[pltpu.VMEM((t_blk, l), jnp.int32), pltpu.VMEM((l, t_blk), jnp.int32)],
        compiler_params=_ARB2,
        name="dsa_attn",
    )(y, y, yx, cos_q, sin_q, cos_i, sin_i, kb, vb, ki2)


def _dsa_mixer(res, gain, bsz, positions, w_in, w_out, layer):
    l = res.shape[0] // bsz
    n_main = DSA_Q_DIM + 2 * DSA_KV_DIM + IDX_HEADS * IDX_HEAD_DIM
    w_in_t = _transposed(w_in)
    u2, yx = _rmsnorm(res, gain, BF16, tail=(w_in_t, layer, n_main))
    y = _proj(u2, w_in_t, layer, n_out=n_main, w_t=True)
    o = _dsa_core(y.reshape(bsz, l, -1), yx.reshape(bsz, l, -1), positions)
    return _proj(o.reshape(bsz * l, DSA_Q_DIM), w_out, layer, mode="residual", res=res)


def kernel(x, p, positions, norm_mix, norm_mlp, norm_ple, norm_final, gdn_w_in, gdn_conv_w, gdn_a_log,
           gdn_dt_bias, gdn_norm, gdn_w_out, ret_w_in, ret_w_out, dsa_w_in, dsa_w_out, mlp_w_up, mlp_w_down,
           ple_w_gate, ple_w_proj):
    bsz, l, d = x.shape
    m = bsz * l
    h = x.reshape(m, d)
    for i in range(DEPTH):
        kind, j = i % N_MIXERS, i // N_MIXERS
        if kind == 0:
            h = _gdn_mixer(h, norm_mix[i], bsz, gdn_w_in, gdn_conv_w, j, gdn_a_log[j], gdn_dt_bias[j], gdn_norm[j],
                           gdn_w_out)
        elif kind == 1:
            u = _rmsnorm(h, norm_mix[i], BF16).reshape(bsz, l, d)
            h = _retention_mixer(u, positions, ret_w_in, ret_w_out, j, h)
        else:
            h = _dsa_mixer(h, norm_mix[i], bsz, positions, dsa_w_in, dsa_w_out, j)
        u = _rmsnorm(h, norm_mlp[i], BF16)
        a = _proj(u, mlp_w_up, i, mode="relu2", out_dtype=BF16)
        h = _proj(a, mlp_w_down, i, mode="residual", res=h)
        u = _rmsnorm(h, norm_ple[i], BF16)
        h = _proj(u, ple_w_gate, i, mode="ple", res=h, p=p.reshape(DEPTH, m, -1), wp=ple_w_proj)
    return _rmsnorm(h, norm_final, x.dtype).reshape(bsz, l, d)
```

```python
import functools

import jax
import jax.numpy as jnp
from jax import lax
from jax.experimental import pallas as pl
from jax.experimental.pallas import tpu as pltpu

F32 = jnp.float32
BF16 = jnp.bfloat16

D_MODEL = 2048
DEPTH = 4
N_MIXERS = 3
NORM_EPS = 1e-6
ROPE_THETA = 10000.0

GDN_HEAD_DIM = 128
GDN_QK_HEADS = D_MODEL // 128
GDN_V_HEADS = 2 * GDN_QK_HEADS
GDN_CONV = 4
GDN_CHUNK = 64
GDN_KEY_DIM = GDN_QK_HEADS * GDN_HEAD_DIM
GDN_VAL_DIM = GDN_V_HEADS * GDN_HEAD_DIM
GDN_CONV_DIM = 2 * GDN_KEY_DIM + GDN_VAL_DIM
GDN_SUPER = 256
GDN_STEP_QK = 2

RET_HEADS = 8
RET_QK_DIM = D_MODEL
RET_V_DIM = 2 * D_MODEL
RET_QK_HEAD = RET_QK_DIM // RET_HEADS
RET_V_HEAD = RET_V_DIM // RET_HEADS
RET_CHUNK = 128
RET_STEP = 256

DSA_HEAD_DIM = 128
DSA_HEADS = D_MODEL // DSA_HEAD_DIM
DSA_KV_HEADS = 4
IDX_HEADS = 16
IDX_HEAD_DIM = 64
DSA_TOPK_MAX = 256
DSA_BLOCK = 128
DSA_KV_STEP = 512
DSA_Q_DIM = DSA_HEADS * DSA_HEAD_DIM
DSA_KV_DIM = DSA_KV_HEADS * DSA_HEAD_DIM

LANES = 128
SUBLANES = 8
VMEM_LIMIT_BYTES = 56 * 1024 * 1024

_ARB2 = pltpu.CompilerParams(dimension_semantics=("arbitrary", "arbitrary"), vmem_limit_bytes=VMEM_LIMIT_BYTES)


def _bdot(a, b):
    return jnp.dot(a.astype(BF16), b.astype(BF16), preferred_element_type=F32)


def _bdot_nt(a, b):
    return lax.dot_general(a.astype(BF16), b.astype(BF16), (((1,), (1,)), ((), ())), preferred_element_type=F32)


def _bdot_tn(a, b):
    return lax.dot_general(a.astype(BF16), b.astype(BF16), (((0,), (0,)), ((), ())), preferred_element_type=F32)


def _lane_iota(shape):
    return lax.broadcasted_iota(jnp.int32, shape, 1)


def _silu(x):
    return x * jax.nn.sigmoid(x)


def _mm_kernel(*refs, nk, mode, valid, w_t, cache):
    if cache:
        x_ref, w_ref, o_ref, wb_ref = refs

        @pl.when(pl.program_id(1) == 0)
        def _():
            wb_ref[...] = w_ref[...].astype(BF16)

        prod = _bdot_nt(x_ref[...], wb_ref[...]) if w_t else _bdot(x_ref[...], wb_ref[...])
        if mode == "relu2":
            prod = jnp.square(jnp.maximum(prod, 0.0))
        o_ref[...] = prod.astype(o_ref.dtype)
        return
    x_ref, w_ref, o_ref = refs[0], refs[1], refs[-1]
    extra = refs[2:-1]
    if w_t:
        w = w_ref[...]
        if mode == "tail":
            w = jnp.where(lax.broadcasted_iota(jnp.int32, w.shape, 0) < valid, w, 0.0)
        o_ref[...] = _bdot_nt(x_ref[...], w)
        return
    if nk > 1:
        @pl.when(pl.program_id(2) == 0)
        def _():
            o_ref[...] = extra[0][...]

        half = o_ref.shape[1] // 2
        for cs in (slice(0, half), slice(half, 2 * half)):
            o_ref[:, cs] += _bdot(x_ref[...], w_ref[:, cs])
        return
    if mode == "tail":
        w = jnp.where(_lane_iota(w_ref.shape) < valid, w_ref[...], 0.0)
        o_ref[...] = _bdot(x_ref[...], w)
        return
    prod = _bdot(x_ref[...], w_ref[...])
    if mode == "plain":
        o_ref[...] = prod.astype(o_ref.dtype)
    elif mode == "relu2":
        o_ref[...] = jnp.square(jnp.maximum(prod, 0.0)).astype(o_ref.dtype)
    elif mode == "ple":
        res_ref, p_ref, wp_ref = extra
        o_ref[...] = res_ref[...] + jax.nn.sigmoid(prod) * _bdot(p_ref[...], wp_ref[...])
    else:
        o_ref[...] = extra[0][...] + prod


def _matmul(x, w, layer, *, n_out, bm, bn, bk, n_outer, col0=0, mode="plain", out_dtype=F32, res=None, p=None,
            wp=None, w_t=False):
    m, kdim = x.shape
    bm = min(bm, m)
    n_total = w.shape[1] if w_t else w.shape[2]
    assert m % bm == 0 and n_out % bn == 0 and col0 % bn == 0 and kdim % bk == 0, (x.shape, w.shape, bm, bn, bk)
    assert col0 + n_out <= n_total or (mode == "tail" and n_out == bn)
    nk = kdim // bk
    assert nk == 1 or mode == "residual"
    assert not w_t or mode in ("plain", "tail")
    cb0 = col0 // bn
    if n_outer:
        grid = (n_out // bn, m // bm, nk)
        mi, ni = (lambda j, i, k: i), (lambda j, i, k: j)
    else:
        grid = (m // bm, n_out // bn, nk)
        mi, ni = (lambda i, j, k: i), (lambda i, j, k: j)
    in_specs = [pl.BlockSpec((bm, bk), lambda a, b, k: (mi(a, b, k), k)),
                pl.BlockSpec((None, bn, bk), lambda a, b, k: (layer, cb0 + ni(a, b, k), k)) if w_t else
                pl.BlockSpec((None, bk, bn), lambda a, b, k: (layer, k, cb0 + ni(a, b, k)))]
    args = [x, w]
    o_map = lambda a, b, k: (mi(a, b, k), ni(a, b, k))
    if mode in ("residual", "ple"):
        in_specs.append(pl.BlockSpec((bm, bn), o_map))
        args.append(res)
    if mode == "ple":
        kp = p.shape[2]
        in_specs.append(pl.BlockSpec((None, bm, kp), lambda a, b, k: (layer, mi(a, b, k), 0)))
        in_specs.append(pl.BlockSpec((None, kp, bn), lambda a, b, k: (layer, 0, ni(a, b, k))))
        args += [p, wp]
    cache = n_outer and nk == 1 and mode in ("plain", "relu2") and m // bm > 1
    return pl.pallas_call(
        functools.partial(_mm_kernel, nk=nk, mode=mode, valid=n_total - col0, w_t=w_t, cache=cache),
        out_shape=jax.ShapeDtypeStruct((m, n_out), out_dtype),
        grid=grid,
        in_specs=in_specs,
        out_specs=pl.BlockSpec((bm, bn), o_map),
        scratch_shapes=[pltpu.VMEM((bn, bk) if w_t else (bk, bn), BF16)] if cache else [],
        compiler_params=pltpu.CompilerParams(
            dimension_semantics=("arbitrary", "arbitrary", "arbitrary"), vmem_limit_bytes=VMEM_LIMIT_BYTES),
        name="matmul_" + mode,
    )(*args)


def _proj(x, w, layer, n_out=None, **kw):
    w_t = kw.get("w_t", False)
    kdim = w.shape[2] if w_t else w.shape[1]
    n_out = (w.shape[1] if w_t else w.shape[2]) if n_out is None else n_out
    if kdim <= 2048:
        bm = 512 if kw.get("mode") == "ple" else 1024
        return _matmul(x, w, layer, n_out=n_out, bm=bm, bn=min(n_out, 1024), bk=kdim, n_outer=True, **kw)
    return _matmul(x, w, layer, n_out=n_out, bm=2048, bn=1024, bk=1024, n_outer=False, **kw)


def _transposed(w):
    return jnp.swapaxes(w, 1, 2)


def _rmsnorm_kernel(x_ref, g_ref, *refs, valid):
    o_ref = refs[-2] if valid else refs[-1]
    x = x_ref[...]
    y = (x * lax.rsqrt(jnp.mean(x * x, axis=1, keepdims=True) + NORM_EPS) * g_ref[...]).astype(o_ref.dtype)
    o_ref[...] = y
    if valid:
        w = refs[0][...]
        w = jnp.where(lax.broadcasted_iota(jnp.int32, w.shape, 0) < valid, w, 0.0)
        refs[-1][...] = _bdot_nt(y, w)


def _rmsnorm(x, gain, out_dtype, tail=None):
    m, d = x.shape
    bm = min(m, 512)
    in_specs = [pl.BlockSpec((bm, d), lambda i: (i, 0)), pl.BlockSpec((1, d), lambda i: (0, 0))]
    out_shape = [jax.ShapeDtypeStruct((m, d), out_dtype)]
    out_specs = [pl.BlockSpec((bm, d), lambda i: (i, 0))]
    args, valid = [x, gain.reshape(1, d)], 0
    if tail is not None:
        w_t, layer, start = tail
        valid = w_t.shape[1] - start
        in_specs.append(pl.BlockSpec((None, LANES, d), lambda i: (layer, start // LANES, 0)))
        out_shape.append(jax.ShapeDtypeStruct((m, LANES), F32))
        out_specs.append(pl.BlockSpec((bm, LANES), lambda i: (i, 0)))
        args.append(w_t)
    out = pl.pallas_call(
        functools.partial(_rmsnorm_kernel, valid=valid),
        out_shape=out_shape,
        grid=(m // bm,),
        in_specs=in_specs,
        out_specs=out_specs,
        compiler_params=pltpu.CompilerParams(dimension_semantics=("arbitrary",), vmem_limit_bytes=VMEM_LIMIT_BYTES),
        name="rmsnorm",
    )(*args)
    return out if tail is not None else out[0]


def _gdn_gates_kernel(gb_ref, alog_ref, dtb_ref, o_ref):
    x = gb_ref[0]
    lane = _lane_iota(x.shape)
    row = lax.broadcasted_iota(jnp.int32, x.shape, 0) % GDN_CHUNK
    g = -jnp.exp(alog_ref[...]) * jax.nn.softplus(x + dtb_ref[...])
    sh = 1
    while sh < GDN_CHUNK:
        g = g + jnp.where(row >= sh, pltpu.roll(g, sh, 0), 0.0)
        sh *= 2
    o_ref[0] = jnp.where(lane < GDN_V_HEADS, jax.nn.sigmoid(x), g)


def _conv_silu_tile(x, tail, w):
    rows8 = lax.broadcasted_iota(jnp.int32, tail.shape, 0)
    acc = x * w[GDN_CONV - 1:GDN_CONV]
    for sh in range(1, GDN_CONV):
        xs = pltpu.roll(x, sh, 0)
        head = jnp.where(rows8 < sh, pltpu.roll(tail, sh, 0), xs[:SUBLANES])
        xs = jnp.concatenate([head, xs[SUBLANES:]], axis=0)
        acc = acc + xs * w[GDN_CONV - 1 - sh:GDN_CONV - sh]
    return _silu(acc)


def _gdn_kernel(q_ref, k_ref, v_ref, z_ref, gt_ref, wq_ref, wk_ref, wv_ref, nw_ref, o_ref,
                s_ref, kn_st, wk_st, qg_st, gl_st, in_st, *, seq):
    hq = pl.program_id(1)
    r_blk, c_blk, d = GDN_SUPER, GDN_CHUNK, GDN_HEAD_DIM
    ii = lax.broadcasted_iota(jnp.int32, (r_blk, r_blk), 0)
    jj = lax.broadcasted_iota(jnp.int32, (r_blk, r_blk), 1)
    blk_masks = [(ii >> sh) == (jj >> sh) for sh in (3, 4, 5, 6)]
    blk8 = blk_masks[0]
    tri_incl = blk_masks[3] & (ii >= jj)
    diag = ii == jj
    eye = jnp.where(diag, 1.0, 0.0)
    lane = _lane_iota((r_blk, LANES))
    s_ref[...] = jnp.zeros_like(s_ref)
    wq, wk, wv, nw = wq_ref[...], wk_ref[...], wv_ref[...], nw_ref[...]
    n_steps = seq // r_blk
    rep = GDN_V_HEADS // GDN_QK_HEADS

    def fetch(s):
        r0 = pl.multiple_of(s * r_blk, r_blk)
        rt = pl.multiple_of(jnp.maximum(r0 - SUBLANES, 0), SUBLANES)
        tail = lambda ref: jnp.where(s > 0, ref[0, pl.ds(rt, SUBLANES), :], 0.0)
        rows = pl.ds(r0, r_blk)
        return (q_ref[0, rows, :], tail(q_ref), k_ref[0, rows, :], tail(k_ref), v_ref[0, rows, :], tail(v_ref),
                gt_ref[0, rows, :])

    heads = range(GDN_STEP_QK * rep)
    qk_of = lambda j: j // rep

    def solve(raw, slot):
        xq, tq, xk, tk, xv, tv, gt = raw
        qc, kc, vc = _conv_silu_tile(xq, tq, wq), _conv_silu_tile(xk, tk, wk), _conv_silu_tile(xv, tv, wv)
        qn, kn = [], []
        for i in range(GDN_STEP_QK):
            qi, ki = qc[:, i * d:(i + 1) * d], kc[:, i * d:(i + 1) * d]
            qn.append(qi * lax.rsqrt(jnp.sum(qi * qi, axis=1, keepdims=True) + NORM_EPS) * d ** -0.5)
            kn.append(ki * lax.rsqrt(jnp.sum(ki * ki, axis=1, keepdims=True) + NORM_EPS))
            kn_st[slot, i] = kn[i]
        hv0 = len(heads) * hq
        beta = [jnp.sum(jnp.where(lane == hv0 + j, gt, 0.0), axis=1, keepdims=True) for j in heads]
        g = [jnp.sum(jnp.where(lane == GDN_V_HEADS + hv0 + j, gt, 0.0), axis=1, keepdims=True) for j in heads]
        gi = [jnp.broadcast_to(g[j], (r_blk, r_blk)) for j in heads]
        dec = [jnp.exp(jnp.where(tri_incl, gi[j] - jnp.sum(jnp.where(diag, gi[j], 0.0), axis=0, keepdims=True),
                                 -jnp.inf)) for j in heads]
        kb = [kn[qk_of(j)] * beta[j] for j in heads]
        for j in heads:
            qg_st[slot, j] = qn[qk_of(j)] * jnp.exp(g[j])
            gl_st[slot, j] = gi[j][:, :d]
        qk = [_bdot_nt(qn[i], kn[i]) for i in range(GDN_STEP_QK)]
        kk = [_bdot_nt(kb[j], kn[qk_of(j)]) for j in heads]
        yield
        for j in heads:
            in_st[slot, j] = qk[qk_of(j)] * dec[j]
        a = [jnp.where(diag, 0.0, kk[j] * dec[j]) for j in heads]
        a8 = [jnp.where(blk8, a[j], 0.0) for j in heads]
        a8_2 = [_bdot(a8[j], a8[j]) for j in heads]
        yield
        inv = [eye - a8[j] for j in heads]
        t = [_bdot(inv[j], a8_2[j]) for j in heads]
        a8_4 = [_bdot(a8_2[j], a8_2[j]) for j in heads]
        yield
        inv = [inv[j] + t[j] for j in heads]
        t = [_bdot(inv[j], a8_4[j]) for j in heads]
        yield
        inv = [inv[j] + t[j] for j in heads]
        for lvl in range(3):
            off = [jnp.where(blk_masks[lvl + 1] & ~blk_masks[lvl], a[j], 0.0) for j in heads]
            t = [_bdot(off[j], inv[j]) for j in heads]
            yield
            t = [_bdot(inv[j], t[j]) for j in heads]
            yield
            inv = [inv[j] - t[j] for j in heads]
        rhs = [jnp.concatenate([vc[:, j * d:(j + 1) * d] * beta[j], kb[j] * jnp.exp(g[j])], axis=1) for j in heads]
        x = [_bdot(inv[j], rhs[j]) for j in heads]
        yield
        for j in heads:
            wk_st[slot, j] = x[j]

    def recur(s, slot):
        r0 = pl.multiple_of(s * r_blk, r_blk)
        state = [s_ref[j] for j in heads]
        for c in range(r_blk // c_blk):
            rs = slice(c * c_blk, (c + 1) * c_blk)
            gl = [gl_st[slot, j, rs, :] for j in heads]
            g_last = [gl[j][c_blk - 1:c_blk, :] for j in heads]
            both = [_bdot(jnp.concatenate([wk_st[slot, j, rs, d:], qg_st[slot, j, rs, :]], axis=0), state[j])
                    for j in heads]
            yield
            v_new = [wk_st[slot, j, rs, :d] - both[j][:c_blk] for j in heads]
            kd = [kn_st[slot, qk_of(j), rs, :] * jnp.exp(g_last[j] - gl[j]) for j in heads]
            upd = [_bdot_tn(kd[j], v_new[j]) for j in heads]
            pad = []
            for j in heads:
                parts = []
                if c > 0:
                    parts.append(jnp.zeros((c * c_blk, d), F32))
                parts.append(v_new[j])
                if (c + 1) * c_blk < r_blk:
                    parts.append(jnp.zeros((r_blk - (c + 1) * c_blk, d), F32))
                pad.append(jnp.concatenate(parts, axis=0))
            o_in = [_bdot(in_st[slot, j, rs, :], pad[j]) for j in heads]
            yield
            state = [state[j] * jnp.exp(g_last[j]) + upd[j] for j in heads]
            rows = pl.ds(r0 + c * c_blk, c_blk)
            for j in heads:
                o_c = both[j][c_blk:] + o_in[j]
                on = o_c * lax.rsqrt(jnp.mean(o_c * o_c, axis=1, keepdims=True) + NORM_EPS) * nw
                zc = z_ref[0, rows, j * d:(j + 1) * d]
                o_ref[0, rows, j * d:(j + 1) * d] = (on * _silu(zc)).astype(o_ref.dtype)
        for j in heads:
            s_ref[j] = state[j]

    def run(*gens):
        live = list(gens)
        while live:
            live = [gen for gen in live if next(gen, StopIteration) is not StopIteration]

    run(solve(fetch(0), 0))

    def body(i, carry):
        s = 2 * i
        run(solve(fetch(s + 1), 1), recur(s, 0))
        run(solve(fetch(s + 2), 0), recur(s + 1, 1))
        return carry

    lax.fori_loop(0, n_steps // 2 - 1, body, 0)
    run(solve(fetch(n_steps - 1), 1), recur(n_steps - 2, 0))
    run(recur(n_steps - 1, 1))


def _gdn_core(y, yg, conv_w, layer, a_log, dt_bias, norm_w):
    bsz, l, _ = y.shape
    assert l % (2 * GDN_SUPER) == 0
    nh, d = GDN_QK_HEADS, GDN_HEAD_DIM
    rep = GDN_V_HEADS // GDN_QK_HEADS
    lane_vec = lambda t: jnp.zeros((1, LANES), F32).at[0, GDN_V_HEADS:2 * GDN_V_HEADS].set(t.astype(F32))
    gates = pl.pallas_call(
        _gdn_gates_kernel,
        out_shape=jax.ShapeDtypeStruct((bsz, l, LANES), F32),
        grid=(bsz,),
        in_specs=[pl.BlockSpec((1, l, LANES), lambda b: (b, 0, 0)),
                  pl.BlockSpec((1, LANES), lambda b: (0, 0)), pl.BlockSpec((1, LANES), lambda b: (0, 0))],
        out_specs=pl.BlockSpec((1, l, LANES), lambda b: (b, 0, 0)),
        compiler_params=pltpu.CompilerParams(dimension_semantics=("arbitrary",), vmem_limit_bytes=VMEM_LIMIT_BYTES),
        name="gdn_gates",
    )(yg, lane_vec(a_log), lane_vec(dt_bias))
    r_blk = GDN_SUPER
    nq = GDN_STEP_QK
    qw, vw = nq * d, nq * rep * d
    nv = nq * rep
    k0, v0, z0 = GDN_KEY_DIM // qw, 2 * GDN_KEY_DIM // vw, GDN_CONV_DIM // vw
    return pl.pallas_call(
        functools.partial(_gdn_kernel, seq=l),
        out_shape=jax.ShapeDtypeStruct((bsz, l, GDN_VAL_DIM), BF16),
        grid=(bsz, nh // nq),
        in_specs=[pl.BlockSpec((1, l, qw), lambda b, h: (b, 0, h)),
                  pl.BlockSpec((1, l, qw), lambda b, h: (b, 0, k0 + h)),
                  pl.BlockSpec((1, l, vw), lambda b, h: (b, 0, v0 + h)),
                  pl.BlockSpec((1, l, vw), lambda b, h: (b, 0, z0 + h)),
                  pl.BlockSpec((1, l, LANES), lambda b, h: (b, 0, 0)),
                  pl.BlockSpec((None, GDN_CONV, qw), lambda b, h: (layer, 0, h)),
                  pl.BlockSpec((None, GDN_CONV, qw), lambda b, h: (layer, 0, k0 + h)),
                  pl.BlockSpec((None, GDN_CONV, vw), lambda b, h: (layer, 0, v0 + h)),
                  pl.BlockSpec((1, d), lambda b, h: (0, 0))],
        out_specs=pl.BlockSpec((1, l, vw), lambda b, h: (b, 0, h)),
        scratch_shapes=[pltpu.VMEM((nv, d, d), F32),
                        pltpu.VMEM((2, nq, r_blk, d), F32),
                        pltpu.VMEM((2, nv, r_blk, 2 * d), F32),
                        pltpu.VMEM((2, nv, r_blk, d), F32),
                        pltpu.VMEM((2, nv, r_blk, d), F32),
                        pltpu.VMEM((2, nv, r_blk, r_blk), F32)],
        compiler_params=_ARB2,
        name="gdn_core",
    )(y, y, y, y, gates, conv_w, conv_w, conv_w, norm_w.reshape(1, d))


def _gdn_mixer(res, gain, bsz, w_in, conv_w, layer, a_log, dt_bias, norm_w, w_out):
    l = res.shape[0] // bsz
    n_main = GDN_CONV_DIM + GDN_VAL_DIM
    w_in_t = _transposed(w_in)
    u2, yg = _rmsnorm(res, gain, BF16, tail=(w_in_t, layer, n_main))
    y = _proj(u2, w_in_t, layer, n_out=n_main, w_t=True)
    o = _gdn_core(y.reshape(bsz, l, -1), yg.reshape(bsz, l, -1), conv_w, layer, a_log, dt_bias, norm_w)
    return _proj(o.reshape(bsz * l, GDN_VAL_DIM), w_out, layer, mode="residual", res=res)


def _ret_kernel(q_ref, k_ref, v_ref, gate_ref, cos_ref, sin_ref, dec_ref, qd_ref, kd_ref, cd_ref, o_ref, s_ref):
    @pl.when(pl.program_id(1) == 0)
    def _():
        s_ref[...] = jnp.zeros_like(s_ref)

    half = RET_QK_HEAD // 2

    def rope(ref, rs, h, c, s):
        x1 = ref[0, rs, h * RET_QK_HEAD:h * RET_QK_HEAD + half]
        x2 = ref[0, rs, h * RET_QK_HEAD + half:(h + 1) * RET_QK_HEAD]
        return jnp.concatenate([x1 * c - x2 * s, x2 * c + x1 * s], axis=1)

    heads = range(RET_HEADS)
    vs = [slice(h * RET_V_HEAD, (h + 1) * RET_V_HEAD) for h in heads]
    for c in range(RET_STEP // RET_CHUNK):
        rs = slice(c * RET_CHUNK, (c + 1) * RET_CHUNK)
        cos, sin = cos_ref[0, rs, :], sin_ref[0, rs, :]
        qr = [rope(q_ref, rs, h, cos, sin) for h in heads]
        kr = [rope(k_ref, rs, h, cos, sin) * RET_QK_HEAD ** -0.5 for h in heads]
        sc = [_bdot_nt(qr[h], kr[h]) for h in heads]
        cross = [_bdot(qr[h] * jnp.concatenate([qd_ref[h], qd_ref[h]], axis=1), s_ref[h]) for h in heads]
        upd = [_bdot_tn(kr[h] * jnp.concatenate([kd_ref[h], kd_ref[h]], axis=1), v_ref[0, rs, vs[h]]) for h in heads]
        inner = [_bdot(sc[h] * dec_ref[h], v_ref[0, rs, vs[h]]) for h in heads]
        for h in heads:
            s_ref[h] = s_ref[h] * cd_ref[h] + upd[h]
            o = inner[h] + cross[h]
            on = o * lax.rsqrt(jnp.mean(o * o, axis=1, keepdims=True) + NORM_EPS)
            o_ref[0, rs, vs[h]] = (_silu(gate_ref[0, rs, vs[h]]) * on).astype(o_ref.dtype)


def _ret_core(y, positions):
    bsz, l, _ = y.shape
    c = RET_CHUNK
    inv_freq = ROPE_THETA ** (-jnp.arange(0, RET_QK_HEAD, 2, dtype=F32) / RET_QK_HEAD)
    ang = positions.astype(F32)[..., None] * inv_freq
    cos, sin = jnp.cos(ang), jnp.sin(ang)
    log_gamma = jnp.log1p(-jnp.exp2(-5.0 - jnp.arange(RET_HEADS, dtype=F32)))
    idx = jnp.arange(c, dtype=F32)
    rel = idx[:, None] - idx[None, :]
    inner_decay = jnp.where(rel >= 0, jnp.exp(log_gamma[:, None, None] * jnp.maximum(rel, 0.0)), 0.0)
    lanes = lambda t: jnp.broadcast_to(t[:, :, None], (RET_HEADS, c, LANES))
    q_decay = lanes(jnp.exp(log_gamma[:, None] * (idx[None, :] + 1.0)))
    k_decay = lanes(jnp.exp(log_gamma[:, None] * (c - 1.0 - idx[None, :])))
    chunk_decay = jnp.exp(log_gamma * c)
    t = RET_STEP
    full = lambda shape: pl.BlockSpec(shape, lambda b, i: (0,) * len(shape))
    return pl.pallas_call(
        _ret_kernel,
        out_shape=jax.ShapeDtypeStruct((bsz, l, RET_V_DIM), BF16),
        grid=(bsz, l // t),
        in_specs=[pl.BlockSpec((1, t, RET_QK_DIM), lambda b, i: (b, i, 0)),
                  pl.BlockSpec((1, t, RET_QK_DIM), lambda b, i: (b, i, 1)),
                  pl.BlockSpec((1, t, RET_V_DIM), lambda b, i: (b, i, 1)),
                  pl.BlockSpec((1, t, RET_V_DIM), lambda b, i: (b, i, 2)),
                  pl.BlockSpec((1, t, LANES), lambda b, i: (b, i, 0)),
                  pl.BlockSpec((1, t, LANES), lambda b, i: (b, i, 0)),
                  full((RET_HEADS, c, c)), full((RET_HEADS, c, LANES)), full((RET_HEADS, c, LANES)),
                  pl.BlockSpec(memory_space=pltpu.SMEM)],
        out_specs=pl.BlockSpec((1, t, RET_V_DIM), lambda b, i: (b, i, 0)),
        scratch_shapes=[pltpu.VMEM((RET_HEADS, RET_QK_HEAD, RET_V_HEAD), F32)],
        compiler_params=_ARB2,
        name="ret_core",
    )(y, y, y, y, cos, sin, inner_decay, q_decay, k_decay, chunk_decay)


def _retention_mixer(u, positions, w_in, w_out, layer, res):
    bsz, l, dm = u.shape
    y = _proj(u.reshape(bsz * l, dm), w_in, layer)
    o = _ret_core(y.reshape(bsz, l, -1), positions)
    return _proj(o.reshape(bsz * l, RET_V_DIM), w_out, layer, mode="residual", res=res)


def _rope_tables(positions, dim):
    inv_freq = ROPE_THETA ** (-jnp.arange(0, dim, 2, dtype=F32) / dim)
    ang = positions.astype(F32)[..., None] * inv_freq
    c, s = jnp.cos(ang), jnp.sin(ang)
    reps = LANES // dim
    cos_full = jnp.tile(jnp.concatenate([c, c], axis=-1), (1, 1, reps))
    sin_signed = jnp.tile(jnp.concatenate([-s, s], axis=-1), (1, 1, reps))
    return cos_full, sin_signed


def _rope128(x, cos, sin_signed):
    return x * cos + pltpu.roll(x, 64, 1) * sin_signed


def _rope64(x, cos, sin_signed, low_half):
    partner = jnp.where(low_half, pltpu.roll(x, 96, 1), pltpu.roll(x, 32, 1))
    return x * cos + partner * sin_signed


def _dsa_prep_kernel(k_ref, v_ref, ki_ref, ck_ref, sk_ref, cik_ref, sik_ref, kb_ref, vb_ref, ki2_ref):
    ck, sk = ck_ref[0], sk_ref[0]
    for g in range(DSA_KV_HEADS):
        sl = slice(g * DSA_HEAD_DIM, (g + 1) * DSA_HEAD_DIM)
        kb_ref[0, g] = _rope128(k_ref[0, :, sl], ck, sk).astype(BF16)
        vb_ref[0, g] = v_ref[0, :, sl].astype(BF16)
    x = ki_ref[0]
    lane = _lane_iota(x.shape)
    xr = _rope64(x, cik_ref[0], sik_ref[0], (lane % IDX_HEAD_DIM) < IDX_HEAD_DIM // 2)
    ki2_ref[0] = jnp.where(lane < IDX_HEAD_DIM, xr, pltpu.roll(xr, IDX_HEAD_DIM, 1)).astype(BF16)


def _dsa_block(s_eff, q_ref, qi_ref, wi_ref, cq_ref, sq_ref, ciq_ref, siq_ref, kb_ref, vb_ref, ki2_ref, o_ref,
               key_ref, keyt_ref, *, topk):
    t_blk = DSA_BLOCK
    qb = pl.program_id(1)
    lane = _lane_iota((t_blk, LANES))
    low_half = (lane % IDX_HEAD_DIM) < IDX_HEAD_DIM // 2
    lo64 = lane < IDX_HEAD_DIM

    wi = wi_ref[0] * (IDX_HEADS ** -0.5 * IDX_HEAD_DIM ** -0.5)
    ciq, siq = ciq_ref[0], siq_ref[0]
    ki2 = ki2_ref[0, :s_eff, :]
    score = jnp.zeros((t_blk, s_eff), F32)
    for quad in range(IDX_HEADS // 4):
        rows = []
        for pr in (2 * quad, 2 * quad + 1):
            xr = _rope64(qi_ref[0, :, pr * LANES:(pr + 1) * LANES], ciq, siq, low_half)
            rows.append(jnp.where(lo64, xr, 0.0))
            rows.append(jnp.where(lo64, 0.0, xr))
        s4 = _bdot_nt(jnp.concatenate(rows, axis=0), ki2)
        for j in range(4):
            hh = 4 * quad + j
            w_col = wi[:, IDX_HEAD_DIM + hh:IDX_HEAD_DIM + hh + 1]
            score = score + w_col * jnp.maximum(s4[j * t_blk:(j + 1) * t_blk], 0.0)

    t_idx = qb * t_blk + lax.broadcasted_iota(jnp.int32, (t_blk, 1), 0)
    s_idx = _lane_iota((t_blk, s_eff))
    score = jnp.where(s_idx <= t_idx, score, -jnp.inf)
    bits = lax.bitcast_convert_type(score, jnp.int32)
    keys = bits ^ ((bits >> 31) & jnp.int32(0x7FFFFFFF))
    key_ref[:, :s_eff] = keys
    keyt_ref[:s_eff, :] = keys.T

    kk = jnp.minimum(qb * t_blk + _lane_iota((1, t_blk)) + 1, topk).astype(F32)
    int_min = jnp.int32(-2 ** 31)

    n_chain = 8
    rows = s_eff // n_chain

    def bisect(i, t_u):
        cand_u = t_u | lax.shift_left(jnp.int32(1), 31 - i)
        cand = jnp.broadcast_to(cand_u ^ int_min, (SUBLANES, t_blk))
        acc = [jnp.zeros((SUBLANES, t_blk), F32) for _ in range(n_chain)]
        for r in range(0, rows, SUBLANES):
            for c in range(n_chain):
                hit = keyt_ref[c * rows + r:c * rows + r + SUBLANES, :] >= cand
                acc[c] = acc[c] + jnp.where(hit, 1.0, 0.0)
        while len(acc) > 1:
            acc = [acc[k] + acc[k + 1] for k in range(0, len(acc), 2)]
        cnt = jnp.sum(acc[0], axis=0, keepdims=True)
        return jnp.where(cnt >= kk, cand_u, t_u)

    t_u = lax.fori_loop(0, 32, bisect, jnp.zeros((1, t_blk), jnp.int32))
    thr_row = t_u ^ int_min

    kt = keyt_ref[:s_eff, :]
    n_ge = jnp.sum(jnp.where(kt >= thr_row, 1.0, 0.0), axis=0, keepdims=True)

    @pl.when(jnp.max(n_ge - kk) > 0.0)
    def _():
        tied = jnp.where(kt == thr_row, 1.0, 0.0)
        need = kk - jnp.sum(jnp.where(kt > thr_row, 1.0, 0.0), axis=0, keepdims=True)
        row = lax.broadcasted_iota(jnp.int32, kt.shape, 0)
        before = tied
        sh = 1
        while sh < s_eff:
            before = before + jnp.where(row >= sh, pltpu.roll(before, sh, 0), 0.0)
            sh *= 2
        surplus = jnp.where((tied > 0.0) & (before > need), 1.0, 0.0)
        key_ref[:, :s_eff] = jnp.where(surplus.T > 0.0, int_min, key_ref[:, :s_eff])

    thr = jnp.broadcast_to(thr_row, (t_blk, t_blk)).T[:, :1]
    sel = key_ref[:, :s_eff] >= thr

    cq, sq = cq_ref[0], sq_ref[0]
    group = DSA_HEADS // DSA_KV_HEADS
    def group_logits(g):
        qs = []
        for j in range(group):
            hh = group * g + j
            qs.append(_rope128(q_ref[0, :, hh * DSA_HEAD_DIM:(hh + 1) * DSA_HEAD_DIM], cq, sq) * DSA_HEAD_DIM ** -0.5)
        return _bdot_nt(jnp.concatenate(qs, axis=0), kb_ref[0, g, :s_eff, :])

    next_logits = group_logits(0)
    for g in range(DSA_KV_HEADS):
        logits = next_logits
        if g + 1 < DSA_KV_HEADS:
            next_logits = group_logits(g + 1)
        ps, dens = [], []
        for j in range(group):
            lj = jnp.where(sel, logits[j * t_blk:(j + 1) * t_blk], -jnp.inf)
            pj = jnp.exp(lj - jnp.max(lj, axis=1, keepdims=True))
            dens.append(jnp.sum(pj, axis=1, keepdims=True))
            ps.append(pj)
        pv = _bdot(jnp.concatenate(ps, axis=0), vb_ref[0, g, :s_eff, :])
        for j in range(group):
            hh = group * g + j
            o_ref[0, :, hh * DSA_HEAD_DIM:(hh + 1) * DSA_HEAD_DIM] = (
                pv[j * t_blk:(j + 1) * t_blk] / dens[j]).astype(o_ref.dtype)


def _dsa_kernel(q_ref, qi_ref, wi_ref, cq_ref, sq_ref, ciq_ref, siq_ref, kb_ref, vb_ref, ki2_ref, o_ref, key_ref,
                keyt_ref, *, topk, seq):
    qb = pl.program_id(1)
    blocks_per_step = DSA_KV_STEP // DSA_BLOCK
    for br in range(seq // DSA_KV_STEP):
        @pl.when(qb // blocks_per_step == br)
        def _():
            _dsa_block((br + 1) * DSA_KV_STEP, q_ref, qi_ref, wi_ref, cq_ref, sq_ref, ciq_ref, siq_ref, kb_ref,
                       vb_ref, ki2_ref, o_ref, key_ref, keyt_ref, topk=topk)


def _dsa_core(y, yx, positions):
    bsz, l, _ = y.shape
    topk = min(DSA_TOPK_MAX, l // 4)
    cos_q, sin_q = _rope_tables(positions, DSA_HEAD_DIM)
    cos_i, sin_i = _rope_tables(positions, IDX_HEAD_DIM)
    k_blk = DSA_Q_DIM // DSA_KV_DIM
    ki_blk = 0
    ps = DSA_KV_STEP
    tab = pl.BlockSpec((1, ps, LANES), lambda b, i: (b, i, 0))
    kb, vb, ki2 = pl.pallas_call(
        _dsa_prep_kernel,
        out_shape=(jax.ShapeDtypeStruct((bsz, DSA_KV_HEADS, l, DSA_HEAD_DIM), BF16),
                   jax.ShapeDtypeStruct((bsz, DSA_KV_HEADS, l, DSA_HEAD_DIM), BF16),
                   jax.ShapeDtypeStruct((bsz, l, LANES), BF16)),
        grid=(bsz, l // ps),
        in_specs=[pl.BlockSpec((1, ps, DSA_KV_DIM), lambda b, i: (b, i, k_blk)),
                  pl.BlockSpec((1, ps, DSA_KV_DIM), lambda b, i: (b, i, k_blk + 1)),
                  pl.BlockSpec((1, ps, LANES), lambda b, i: (b, i, ki_blk)),
                  tab, tab, tab, tab],
        out_specs=(pl.BlockSpec((1, DSA_KV_HEADS, ps, DSA_HEAD_DIM), lambda b, i: (b, 0, i, 0)),
                   pl.BlockSpec((1, DSA_KV_HEADS, ps, DSA_HEAD_DIM), lambda b, i: (b, 0, i, 0)),
                   pl.BlockSpec((1, ps, LANES), lambda b, i: (b, i, 0))),
        compiler_params=_ARB2,
        name="dsa_prep",
    )(y, y, yx, cos_q, sin_q, cos_i, sin_i)

    t_blk = DSA_BLOCK
    qtab = pl.BlockSpec((1, t_blk, LANES), lambda b, i: (b, i, 0))
    kv_spec = pl.BlockSpec((1, DSA_KV_HEADS, l, DSA_HEAD_DIM), lambda b, i: (b, 0, 0, 0))
    return pl.pallas_call(
        functools.partial(_dsa_kernel, topk=topk, seq=l),
        out_shape=jax.ShapeDtypeStruct((bsz, l, DSA_Q_DIM), BF16),
        grid=(bsz, l // t_blk),
        in_specs=[pl.BlockSpec((1, t_blk, DSA_Q_DIM), lambda b, i: (b, i, 0)),
                  pl.BlockSpec((1, t_blk, IDX_HEADS * IDX_HEAD_DIM),
                               lambda b, i: (b, i, (DSA_Q_DIM + 2 * DSA_KV_DIM) // (IDX_HEADS * IDX_HEAD_DIM))),
                  pl.BlockSpec((1, t_blk, LANES), lambda b, i: (b, i, ki_blk)),
                  qtab, qtab, qtab, qtab,
                  kv_spec, kv_spec,
                  pl.BlockSpec((1, l, LANES), lambda b, i: (b, 0, 0))],
        out_specs=pl.BlockSpec((1, t_blk, DSA_Q_DIM), lambda b, i: (b, i, 0)),
        scratch_shapes=[pltpu.VMEM((t_blk, l), jnp.int32), pltpu.VMEM((l, t_blk), jnp.int32)],
        compiler_params=_ARB2,
        name="dsa_attn",
    )(y, y, yx, cos_q, sin_q, cos_i, sin_i, kb, vb, ki2)


def _dsa_mixer(res, gain, bsz, positions, w_in, w_out, layer):
    l = res.shape[0] // bsz
    n_main = DSA_Q_DIM + 2 * DSA_KV_DIM + IDX_HEADS * IDX_HEAD_DIM
    w_in_t = _transposed(w_in)
    u2, yx = _rmsnorm(res, gain, BF16, tail=(w_in_t, layer, n_main))
    y = _proj(u2, w_in_t, layer, n_out=n_main, w_t=True)
    o = _dsa_core(y.reshape(bsz, l, -1), yx.reshape(bsz, l, -1), positions)
    return _proj(o.reshape(bsz * l, DSA_Q_DIM), w_out, layer, mode="residual", res=res)


def kernel(x, p, positions, norm_mix, norm_mlp, norm_ple, norm_final, gdn_w_in, gdn_conv_w, gdn_a_log,
           gdn_dt_bias, gdn_norm, gdn_w_out, ret_w_in, ret_w_out, dsa_w_in, dsa_w_out, mlp_w_up, mlp_w_down,
           ple_w_gate, ple_w_proj):
    bsz, l, d = x.shape
    m = bsz * l
    h = x.reshape(m, d)
    for i in range(DEPTH):
        kind, j = i % N_MIXERS, i // N_MIXERS
        if kind == 0:
            h = _gdn_mixer(h, norm_mix[i], bsz, gdn_w_in, gdn_conv_w, j, gdn_a_log[j], gdn_dt_bias[j], gdn_norm[j],
                           gdn_w_out)
        elif kind == 1:
            u = _rmsnorm(h, norm_mix[i], BF16).reshape(bsz, l, d)
            h = _retention_mixer(u, positions, ret_w_in, ret_w_out, j, h)
        else:
            h = _dsa_mixer(h, norm_mix[i], bsz, positions, dsa_w_in, dsa_w_out, j)
        u = _rmsnorm(h, norm_mlp[i], BF16)
        a = _proj(u, mlp_w_up, i, mode="relu2", out_dtype=BF16)
        h = _proj(a, mlp_w_down, i, mode="residual", res=h)
        u = _rmsnorm(h, norm_ple[i], BF16)
        h = _proj(u, ple_w_gate, i, mode="ple", res=h, p=p.reshape(DEPTH, m, -1), wp=ple_w_proj)
    return _rmsnorm(h, norm_final, x.dtype).reshape(bsz, l, d)
```
